```python
import jax, jax.numpy as jnp
from jax import lax
import numpy as np

D_MODEL = 1024
BATCH = 2
SEQ = 8192
DEPTH = 2

CHUNK = 64
N_MIXERS = 2
N_POOL_LAYERS = (DEPTH + 1) // 2
N_ATT_LAYERS = DEPTH // 2

POOL_WIDTH = 2 * D_MODEL
POOL_WINDOWS = (2, 4, 8, 16)
N_POOL_GROUPS = len(POOL_WINDOWS)
POOL_GROUP = POOL_WIDTH // N_POOL_GROUPS

HEAD_DIM = 64
ATT_WIDTH = D_MODEL
N_HEADS = ATT_WIDTH // HEAD_DIM
LEFT_CHUNKS = 8
BAND = (LEFT_CHUNKS + 1) * CHUNK
MAX_REL = 256

RMS_EPS = 1e-6

kernel_name = "hybrid_pool_chunkattn_sandwich"


def rms_norm(x, g):
    xf = x.astype(jnp.float32)
    y = xf * lax.rsqrt(jnp.mean(xf * xf, axis=-1, keepdims=True) + RMS_EPS)
    return (y * g.astype(jnp.float32)).astype(x.dtype)


def pool_mixer(h, w_in, w_group, scale, w_out):
    b, s, _ = h.shape
    u = h @ w_in
    a, z = jnp.split(u, 2, axis=-1)
    ag = a.astype(jnp.float32).reshape(b, s, N_POOL_GROUPS, POOL_GROUP)
    cs = jnp.cumsum(ag, axis=1)
    cs0 = jnp.concatenate([jnp.zeros((b, 1, N_POOL_GROUPS, POOL_GROUP), jnp.float32), cs], axis=1)
    pos = jnp.arange(s)
    pooled = []
    for gi, w in enumerate(POOL_WINDOWS):
        cg = cs0[:, :, gi]
        lagged = jnp.concatenate([jnp.zeros((b, w, POOL_GROUP), jnp.float32), cg], axis=1)[:, 1:s + 1]
        cnt = jnp.minimum(pos + 1, w).astype(jnp.float32)[None, :, None]
        pooled.append((cg[:, 1:] - lagged) / cnt)
    mixed = (jnp.stack(pooled, axis=2) - ag).astype(a.dtype)
    mixed = jnp.einsum('bsgc,gcd->bsgd', mixed, w_group).reshape(b, s, POOL_WIDTH) * scale
    return (mixed * jax.nn.silu(z)) @ w_out


def chunk_attention(h, w_in, rel_bias, w_out):
    b, s, _ = h.shape
    nc = s // CHUNK
    pad = LEFT_CHUNKS * CHUNK
    u = h @ w_in
    q, k, v, z = jnp.split(u, 4, axis=-1)
    q = q.reshape(b, s, N_HEADS, HEAD_DIM)
    k = k.reshape(b, s, N_HEADS, HEAD_DIM)
    v = v.reshape(b, s, N_HEADS, HEAD_DIM)
    kp = jnp.pad(k, ((0, 0), (pad, 0), (0, 0), (0, 0)))
    vp = jnp.pad(v, ((0, 0), (pad, 0), (0, 0), (0, 0)))
    qc = q.reshape(b, nc, CHUNK, N_HEADS, HEAD_DIM).transpose(1, 0, 2, 3, 4)
    rel = jnp.arange(CHUNK)[:, None] + pad - jnp.arange(BAND)[None, :]
    idx = jnp.clip(rel, -MAX_REL, MAX_REL) + MAX_REL
    bias = rel_bias.astype(jnp.float32)[:, idx]
    qk_scale = HEAD_DIM ** -0.5

    def one_chunk(args):
        c, qb = args
        start = c * CHUNK
        kb = lax.dynamic_slice_in_dim(kp, start, BAND, axis=1)
        vb = lax.dynamic_slice_in_dim(vp, start, BAND, axis=1)
        sc = jnp.einsum('bqhd,bkhd->bhqk', qb, kb).astype(jnp.float32) * qk_scale + bias
        valid = (start - pad + jnp.arange(BAND)) >= 0
        sc = jnp.where(valid[None, None, None, :], sc, -jnp.inf)
        p = jax.nn.softmax(sc, axis=-1).astype(vb.dtype)
        return jnp.einsum('bhqk,bkhd->bqhd', p, vb)

    o = lax.map(one_chunk, (jnp.arange(nc), qc))
    o = o.transpose(1, 0, 2, 3, 4).reshape(b, s, ATT_WIDTH)
    return (o * jax.nn.silu(z)) @ w_out


def setup_inputs(seed: int = 0) -> dict:
    key = jax.random.key(seed)
    ks = jax.random.split(key, 12)
    f32 = jnp.float32
    x = jax.random.normal(ks[0], (BATCH, SEQ, D_MODEL), f32)
    norm_pre = 1.0 + 0.05 * jax.random.normal(ks[1], (DEPTH, D_MODEL), f32)
    norm_post = 1.0 + 0.05 * jax.random.normal(ks[2], (DEPTH, D_MODEL), f32)
    pool_w_in = jax.random.normal(ks[3], (N_POOL_LAYERS, D_MODEL, 2 * POOL_WIDTH), f32) * D_MODEL ** -0.5
    pool_w_group = jax.random.normal(ks[4], (N_POOL_LAYERS, N_POOL_GROUPS, POOL_GROUP, POOL_GROUP), f32) * POOL_GROUP ** -0.5
    pool_scale = 1.0 + 0.1 * jax.random.normal(ks[5], (N_POOL_LAYERS, POOL_WIDTH), f32)
    pool_w_out = jax.random.normal(ks[6], (N_POOL_LAYERS, POOL_WIDTH, D_MODEL), f32) * POOL_WIDTH ** -0.5
    att_w_in = jax.random.normal(ks[7], (N_ATT_LAYERS, D_MODEL, 4 * ATT_WIDTH), f32) * D_MODEL ** -0.5
    att_rel_bias = 0.5 * jax.random.normal(ks[8], (N_ATT_LAYERS, N_HEADS, 2 * MAX_REL + 1), f32)
    att_w_out = jax.random.normal(ks[9], (N_ATT_LAYERS, ATT_WIDTH, D_MODEL), f32) * ATT_WIDTH ** -0.5
    return {"x": x, "norm_pre": norm_pre, "norm_post": norm_post,
            "pool_w_in": pool_w_in, "pool_w_group": pool_w_group, "pool_scale": pool_scale,
            "pool_w_out": pool_w_out, "att_w_in": att_w_in, "att_rel_bias": att_rel_bias,
            "att_w_out": att_w_out}


def reference(x, norm_pre, norm_post, pool_w_in, pool_w_group, pool_scale, pool_w_out,
              att_w_in, att_rel_bias, att_w_out):
    for i in range(DEPTH):
        h = rms_norm(x, norm_pre[i])
        j = i // N_MIXERS
        if i % N_MIXERS == 0:
            y = pool_mixer(h, pool_w_in[j], pool_w_group[j], pool_scale[j], pool_w_out[j])
        else:
            y = chunk_attention(h, att_w_in[j], att_rel_bias[j], att_w_out[j])
        x = x + rms_norm(y, norm_post[i])
    return x
```

```python
import functools

import jax
import jax.numpy as jnp
from jax import lax
from jax.experimental import pallas as pl
from jax.experimental.pallas import tpu as pltpu

RMS_EPS = 1e-6
POOL_WINDOWS = (2, 4, 8, 16)
POOL_HALO = 16
CHUNK = 64
LEFT_CHUNKS = 8
HEAD_DIM = 64
MAX_REL = 256
BAND = (LEFT_CHUNKS + 1) * CHUNK
HIST = LEFT_CHUNKS * CHUNK

POOL_BLOCK_ROWS = 256
ATT_BLOCK_ROWS = 256
VMEM_LIMIT_BYTES = 56 * 1024 * 1024


def _rms_norm(x, g):
    ms = jnp.mean(x * x, axis=-1, keepdims=True)
    return x * lax.rsqrt(ms + RMS_EPS) * g


def _pool_layer_kernel(x_ref, gpre_ref, gpost_ref, win_ref, wg_ref, scale_ref, wout_ref,
                       o_ref, carry_ref, act_ref):
    j = pl.program_id(1)
    tm = x_ref.shape[1]
    width = act_ref.shape[1]
    n_groups = len(POOL_WINDOWS)
    gw = width // n_groups

    @pl.when(j == 0)
    def _():
        carry_ref[...] = jnp.zeros_like(carry_ref)

    x = x_ref[0]
    hb = _rms_norm(x, gpre_ref[...]).astype(jnp.bfloat16)

    t = j * tm + lax.broadcasted_iota(jnp.int32, (tm, 1), 0)

    for gi, w in enumerate(POOL_WINDOWS):
        cols = slice(gi * gw, (gi + 1) * gw)
        a = jnp.dot(hb, win_ref[:, cols], preferred_element_type=jnp.float32)
        ext = jnp.concatenate([carry_ref[:, cols], a], axis=0)
        carry_ref[:, cols] = a[tm - POOL_HALO:, :]
        s = ext
        k = 1
        while k < w:
            s = s + pltpu.roll(s, k, axis=0)
            k *= 2
        cnt = jnp.minimum(t + 1, w).astype(jnp.float32)
        pooled = s[POOL_HALO:, :] / cnt
        mixed = (pooled - a).astype(jnp.bfloat16)
        m = jnp.dot(mixed, wg_ref[gi], preferred_element_type=jnp.float32) * scale_ref[:, cols]
        z = jnp.dot(hb, win_ref[:, width + gi * gw: width + (gi + 1) * gw],
                    preferred_element_type=jnp.float32)
        act_ref[:, cols] = (m * (z * jax.nn.sigmoid(z))).astype(jnp.bfloat16)

    y = jnp.dot(act_ref[...], wout_ref[...], preferred_element_type=jnp.float32)
    o_ref[0] = x + _rms_norm(y, gpost_ref[...])


def _att_layer_kernel(x_ref, gpre_ref, gpost_ref, win_ref, bias_ref, wout_ref,
                      o_ref, k_ref, v_ref, q_ref, ctx_ref):
    j = pl.program_id(1)
    tq = x_ref.shape[1]
    width = wout_ref.shape[0]
    n_heads = width // HEAD_DIM
    n_chunks = tq // CHUNK

    @pl.when(j == 0)
    def _():
        k_ref[0:HIST, :] = jnp.zeros((HIST, width), k_ref.dtype)
        v_ref[0:HIST, :] = jnp.zeros((HIST, width), v_ref.dtype)

    x = x_ref[0]
    hb = _rms_norm(x, gpre_ref[...]).astype(jnp.bfloat16)
    q = jnp.dot(hb, win_ref[:, 0:width], preferred_element_type=jnp.float32)
    q_ref[...] = (q * (HEAD_DIM ** -0.5)).astype(jnp.bfloat16)
    k = jnp.dot(hb, win_ref[:, width:2 * width], preferred_element_type=jnp.float32)
    k_ref[HIST:HIST + tq, :] = k.astype(jnp.bfloat16)
    v = jnp.dot(hb, win_ref[:, 2 * width:3 * width], preferred_element_type=jnp.float32)
    v_ref[HIST:HIST + tq, :] = v.astype(jnp.bfloat16)

    key_lane = lax.broadcasted_iota(jnp.int32, (CHUNK, BAND), 1)

    def chunk_body(ci, carry):
        row0 = pl.multiple_of(ci * CHUNK, CHUNK)
        first_valid = HIST - (j * tq + ci * CHUNK)
        valid = key_lane >= first_valid
        for h in range(n_heads):
            hc = slice(h * HEAD_DIM, (h + 1) * HEAD_DIM)
            qh = q_ref[pl.ds(row0, CHUNK), hc]
            kh = k_ref[pl.ds(row0, BAND), hc]
            vh = v_ref[pl.ds(row0, BAND), hc]
            s = lax.dot_general(qh, kh, (((1,), (1,)), ((), ())),
                                preferred_element_type=jnp.float32)
            s = jnp.where(valid, s + bias_ref[h], -jnp.inf)
            p = jnp.exp(s - jnp.max(s, axis=-1, keepdims=True))
            denom = jnp.sum(p, axis=-1, keepdims=True)
            o = jnp.dot(p.astype(jnp.bfloat16), vh, preferred_element_type=jnp.float32)
            ctx_ref[pl.ds(row0, CHUNK), hc] = o / denom
        return carry

    lax.fori_loop(0, n_chunks, chunk_body, 0)

    z = jnp.dot(hb, win_ref[:, 3 * width:4 * width], preferred_element_type=jnp.float32)
    gated = (ctx_ref[...] * (z * jax.nn.sigmoid(z))).astype(jnp.bfloat16)
    y = jnp.dot(gated, wout_ref[...], preferred_element_type=jnp.float32)
    o_ref[0] = x + _rms_norm(y, gpost_ref[...])

    k_ref[0:HIST, :] = k_ref[tq:tq + HIST, :]
    v_ref[0:HIST, :] = v_ref[tq:tq + HIST, :]


def _resident(shape):
    zeros = (0,) * len(shape)
    return pl.BlockSpec(shape, lambda b, j: zeros, pipeline_mode=pl.Buffered(1))


def _row_block(rows, d):
    return pl.BlockSpec((1, rows, d), lambda b, j: (b, j, 0))


_COMPILER_PARAMS = pltpu.CompilerParams(
    dimension_semantics=("arbitrary", "arbitrary"), vmem_limit_bytes=VMEM_LIMIT_BYTES)


def _pool_layer(x, g_pre, g_post, w_in, w_group, scale, w_out):
    b, s, d = x.shape
    width = w_out.shape[0]
    tm = POOL_BLOCK_ROWS
    return pl.pallas_call(
        _pool_layer_kernel,
        grid=(b, s // tm),
        in_specs=[_row_block(tm, d), _resident(g_pre.shape), _resident(g_post.shape),
                  _resident(w_in.shape), _resident(w_group.shape), _resident(scale.shape),
                  _resident(w_out.shape)],
        out_specs=_row_block(tm, d),
        out_shape=jax.ShapeDtypeStruct(x.shape, x.dtype),
        scratch_shapes=[pltpu.VMEM((POOL_HALO, width), jnp.float32),
                        pltpu.VMEM((tm, width), jnp.bfloat16)],
        compiler_params=_COMPILER_PARAMS,
        name="pool_layer",
    )(x, g_pre, g_post, w_in, w_group, scale, w_out)


def _att_layer(x, g_pre, g_post, w_in, bias, w_out):
    b, s, d = x.shape
    width = w_out.shape[0]
    tq = ATT_BLOCK_ROWS
    return pl.pallas_call(
        _att_layer_kernel,
        grid=(b, s // tq),
        in_specs=[_row_block(tq, d), _resident(g_pre.shape), _resident(g_post.shape),
                  _resident(w_in.shape), _resident(bias.shape), _resident(w_out.shape)],
        out_specs=_row_block(tq, d),
        out_shape=jax.ShapeDtypeStruct(x.shape, x.dtype),
        scratch_shapes=[pltpu.VMEM((HIST + tq, width), jnp.bfloat16),
                        pltpu.VMEM((HIST + tq, width), jnp.bfloat16),
                        pltpu.VMEM((tq, width), jnp.bfloat16),
                        pltpu.VMEM((tq, width), jnp.float32)],
        compiler_params=_COMPILER_PARAMS,
        name="att_layer",
    )(x, g_pre, g_post, w_in, bias, w_out)


def _band_bias(rel_bias):
    rel = jnp.arange(CHUNK)[:, None] + HIST - jnp.arange(BAND)[None, :]
    idx = jnp.clip(rel, -MAX_REL, MAX_REL) + MAX_REL
    return rel_bias.astype(jnp.float32)[:, idx]


def kernel(x, norm_pre, norm_post, pool_w_in, pool_w_group, pool_scale, pool_w_out,
           att_w_in, att_rel_bias, att_w_out):
    bf16 = jnp.bfloat16
    x = _pool_layer(x, norm_pre[0:1], norm_post[0:1], pool_w_in[0].astype(bf16),
                    pool_w_group[0].astype(bf16), pool_scale[0:1], pool_w_out[0].astype(bf16))
    x = _att_layer(x, norm_pre[1:2], norm_post[1:2], att_w_in[0].astype(bf16),
                   _band_bias(att_rel_bias[0]), att_w_out[0].astype(bf16))
    return x
```

```python
import jax
import jax.numpy as jnp
from jax import lax
from jax.experimental import pallas as pl
from jax.experimental.pallas import tpu as pltpu

RMS_EPS = 1e-6
POOL_WINDOWS = (2, 4, 8, 16)
POOL_HALO = 16
CHUNK = 64
LEFT_CHUNKS = 8
HEAD_DIM = 64
MAX_REL = 256
HIST = LEFT_CHUNKS * CHUNK

LANES = 128
POOL_BLOCK_ROWS = 256
ATT_BLOCK_ROWS = 256
ATT_SPAN = HIST + LANES
REL_LANES = HIST + ATT_BLOCK_ROWS
VMEM_LIMIT_BYTES = 56 * 1024 * 1024


def _rms_norm(x, g):
    ms = jnp.mean(x * x, axis=-1, keepdims=True)
    return x * lax.rsqrt(ms + RMS_EPS) * g


def _pool_layer_kernel(x_ref, gpre_ref, gpost_ref, win_ref, wg_ref, scale_ref, wout_ref,
                       o_ref, carry_ref, act_ref):
    j = pl.program_id(1)
    tm = x_ref.shape[1]
    width = act_ref.shape[1]
    n_groups = len(POOL_WINDOWS)
    gw = width // n_groups

    @pl.when(j == 0)
    def _():
        carry_ref[...] = jnp.zeros_like(carry_ref)

    x = x_ref[0]
    hb = _rms_norm(x, gpre_ref[...]).astype(jnp.bfloat16)

    t = j * tm + lax.broadcasted_iota(jnp.int32, (tm, 1), 0)

    for gi, w in enumerate(POOL_WINDOWS):
        cols = slice(gi * gw, (gi + 1) * gw)
        a = jnp.dot(hb, win_ref[:, cols], preferred_element_type=jnp.float32)
        ext = jnp.concatenate([carry_ref[:, cols], a], axis=0)
        carry_ref[:, cols] = a[tm - POOL_HALO:, :]
        s = ext
        k = 1
        while k < w:
            s = s + pltpu.roll(s, k, axis=0)
            k *= 2
        cnt = jnp.minimum(t + 1, w).astype(jnp.float32)
        pooled = s[POOL_HALO:, :] / cnt
        mixed = (pooled - a).astype(jnp.bfloat16)
        m = jnp.dot(mixed, wg_ref[gi], preferred_element_type=jnp.float32) * scale_ref[:, cols]
        z = jnp.dot(hb, win_ref[:, width + gi * gw: width + (gi + 1) * gw],
                    preferred_element_type=jnp.float32)
        act_ref[:, cols] = (m * (z * jax.nn.sigmoid(z))).astype(jnp.bfloat16)

    y = jnp.dot(act_ref[...], wout_ref[...], preferred_element_type=jnp.float32)
    o_ref[0] = x + _rms_norm(y, gpost_ref[...])


def _build_bias_table(rb_ref, bias_ref):
    n_heads, rows, cols = bias_ref.shape
    lanes = rb_ref.shape[1]
    u = lax.broadcasted_iota(jnp.int32, (1, lanes), 1)
    in_table = (u > cols) & (u < 2 * MAX_REL)
    jj = lax.broadcasted_iota(jnp.int32, (rows, cols), 0)
    c = lax.broadcasted_iota(jnp.int32, (rows, cols), 1)
    kc, qc = jj // CHUNK, c // CHUNK
    in_band = (kc >= qc) & (kc <= qc + LEFT_CHUNKS)
    for h in range(n_heads):
        far = rb_ref[h:h + 1, 2 * MAX_REL:2 * MAX_REL + 1]
        g = jnp.where(in_table, rb_ref[h:h + 1, :], far)
        t = pltpu.roll(jnp.broadcast_to(g, (rows, lanes)), 0, 1, stride=1, stride_axis=0)
        bias_ref[h] = jnp.where(in_band, t[:, 0:cols], -jnp.inf)


def _att_layer_kernel(x_ref, gpre_ref, gpost_ref, wk_ref, wqvz_ref, rb_ref, wout_ref,
                      o_ref, k_ref, vt_ref, gated_ref, bias_ref):
    b = pl.program_id(0)
    j = pl.program_id(1)
    tq = x_ref.shape[1]
    width = wout_ref.shape[0]
    n_heads = width // HEAD_DIM
    win = HIST + tq
    half = tq // 2
    span = bias_ref.shape[1]
    pair = 2 * HEAD_DIM
    bf16 = jnp.bfloat16

    @pl.when((b == 0) & (j == 0))
    def _():
        _build_bias_table(rb_ref, bias_ref)

    n_slots = k_ref.shape[0]
    slots = [lax.rem(j + 1 + i, n_slots) for i in range(n_slots)]

    @pl.when(j == 0)
    def _():
        k_ref[...] = jnp.zeros(k_ref.shape, bf16)
        vt_ref[...] = jnp.zeros(vt_ref.shape, bf16)

    x = x_ref[0]
    h = _rms_norm(x, gpre_ref[...])
    hb = h.astype(bf16)
    hbt = h.T.astype(bf16)
    k = jnp.dot(hb, wk_ref[...], preferred_element_type=jnp.float32)
    k_ref[slots[-1]] = k.astype(bf16)
    qt = jnp.dot(wqvz_ref[0:width, :], hbt, preferred_element_type=jnp.float32)
    qt = (qt * (HEAD_DIM ** -0.5)).astype(bf16)
    vt = jnp.dot(wqvz_ref[width:2 * width, :], hbt, preferred_element_type=jnp.float32)
    vt_ref[slots[-1]] = vt.astype(bf16)
    zt = jnp.dot(wqvz_ref[2 * width:3 * width, :], hbt, preferred_element_type=jnp.float32)

    row = lax.broadcasted_iota(jnp.int32, (win, pair), 0)
    lane = lax.broadcasted_iota(jnp.int32, (win, pair), 1)
    k_invalid = jnp.where((row < HIST - j * tq) & (lane == 0), 1.0, 0.0).astype(bf16)
    r2 = lax.broadcasted_iota(jnp.int32, (pair, tq), 0)
    q_penalty = jnp.where(r2 == 0, -1e30, 0.0).astype(bf16)
    zeros_q = jnp.zeros((HEAD_DIM, tq), bf16)
    zeros_p = jnp.zeros((half, half), bf16)
    ones_v = jnp.ones((16, win), bf16)

    for hd in range(n_heads):
        p0 = (hd // 2) * pair
        rows = slice(hd * HEAD_DIM, (hd + 1) * HEAD_DIM)
        q_h = qt[rows, :]
        q_pair = [q_h, zeros_q] if hd % 2 == 0 else [zeros_q, q_h]
        rhs = jnp.concatenate(q_pair + [q_penalty], axis=0)
        k_win = jnp.concatenate([k_ref[s, :, p0:p0 + pair] for s in slots], axis=0)
        lhs = jnp.concatenate([k_win, k_invalid], axis=1)
        st = jnp.dot(lhs, rhs, preferred_element_type=jnp.float32)
        probs = []
        for m in range(2):
            s = st[m * half:m * half + span, m * half:(m + 1) * half] + bias_ref[hd]
            probs.append(jnp.exp(s - jnp.max(s, axis=0, keepdims=True)).astype(bf16))
        pt = jnp.concatenate([jnp.concatenate([probs[0], zeros_p], axis=0),
                              jnp.concatenate([zeros_p, probs[1]], axis=0)], axis=1)
        vt_win = jnp.concatenate([vt_ref[s, rows, :] for s in slots], axis=1)
        v_ext = jnp.concatenate([vt_win, ones_v], axis=0)
        o_ext = jnp.dot(v_ext, pt, preferred_element_type=jnp.float32)
        inv = 1.0 / o_ext[HEAD_DIM:HEAD_DIM + 8, :]
        o_t = o_ext[0:HEAD_DIM, :] * jnp.concatenate([inv] * (HEAD_DIM // 8), axis=0)
        z_h = zt[rows, :]
        gated_ref[rows, :] = (o_t * (z_h * jax.nn.sigmoid(z_h))).astype(bf16)

    yt = jnp.dot(wout_ref[...], gated_ref[...], preferred_element_type=jnp.float32)
    o_ref[0] = x + _rms_norm(yt.T, gpost_ref[...])


def _resident(shape):
    zeros = (0,) * len(shape)
    return pl.BlockSpec(shape, lambda b, j: zeros, pipeline_mode=pl.Buffered(1))


def _row_block(rows, d):
    return pl.BlockSpec((1, rows, d), lambda b, j: (b, j, 0))


_COMPILER_PARAMS = pltpu.CompilerParams(
    dimension_semantics=("arbitrary", "arbitrary"), vmem_limit_bytes=VMEM_LIMIT_BYTES)


def _pool_layer(x, g_pre, g_post, w_in, w_group, scale, w_out):
    b, s, d = x.shape
    width = w_out.shape[0]
    tm = POOL_BLOCK_ROWS
    return pl.pallas_call(
        _pool_layer_kernel,
        grid=(b, s // tm),
        in_specs=[_row_block(tm, d), _resident(g_pre.shape), _resident(g_post.shape),
                  _resident(w_in.shape), _resident(w_group.shape), _resident(scale.shape),
                  _resident(w_out.shape)],
        out_specs=_row_block(tm, d),
        out_shape=jax.ShapeDtypeStruct(x.shape, x.dtype),
        scratch_shapes=[pltpu.VMEM((POOL_HALO, width), jnp.float32),
                        pltpu.VMEM((tm, width), jnp.bfloat16)],
        compiler_params=_COMPILER_PARAMS,
        name="pool_layer",
    )(x, g_pre, g_post, w_in, w_group, scale, w_out)


def _att_layer(x, g_pre, g_post, w_k, w_qvz_t, rel_bias, w_out_t):
    b, s, d = x.shape
    width = w_out_t.shape[1]
    n_heads = width // HEAD_DIM
    tq = ATT_BLOCK_ROWS
    n_slots = HIST // tq + 1
    return pl.pallas_call(
        _att_layer_kernel,
        grid=(b, s // tq),
        in_specs=[_row_block(tq, d), _resident(g_pre.shape), _resident(g_post.shape),
                  _resident(w_k.shape), _resident(w_qvz_t.shape), _resident(rel_bias.shape),
                  _resident(w_out_t.shape)],
        out_specs=_row_block(tq, d),
        out_shape=jax.ShapeDtypeStruct(x.shape, x.dtype),
        scratch_shapes=[pltpu.VMEM((n_slots, tq, width), jnp.bfloat16),
                        pltpu.VMEM((n_slots, width, tq), jnp.bfloat16),
                        pltpu.VMEM((width, tq), jnp.bfloat16),
                        pltpu.VMEM((n_heads, ATT_SPAN, LANES), jnp.float32)],
        compiler_params=_COMPILER_PARAMS,
        name="att_layer",
    )(x, g_pre, g_post, w_k, w_qvz_t, rel_bias, w_out_t)


def kernel(x, norm_pre, norm_post, pool_w_in, pool_w_group, pool_scale, pool_w_out,
           att_w_in, att_rel_bias, att_w_out):
    bf16 = jnp.bfloat16
    x = _pool_layer(x, norm_pre[0:1], norm_post[0:1], pool_w_in[0].astype(bf16),
                    pool_w_group[0].astype(bf16), pool_scale[0:1], pool_w_out[0].astype(bf16))
    w = att_w_in[0]
    width = att_w_out.shape[1]
    w_k = w[:, width:2 * width].astype(bf16)
    w_qvz_t = jnp.concatenate([w[:, 0:width], w[:, 2 * width:4 * width]], axis=1).T.astype(bf16)
    rel_bias = jnp.pad(att_rel_bias[0], ((0, 0), (0, REL_LANES - att_rel_bias.shape[-1])))
    x = _att_layer(x, norm_pre[1:2], norm_post[1:2], w_k, w_qvz_t, rel_bias,
                   att_w_out[0].T.astype(bf16))
    return x
```

```python
import jax
import jax.numpy as jnp
from jax import lax
from jax.experimental import pallas as pl
from jax.experimental.pallas import tpu as pltpu

RMS_EPS = 1e-6
LOG2_E = 1.4426950408889634
POOL_WINDOWS = (2, 4, 8, 16)
POOL_HALO = 16
CHUNK = 64
LEFT_CHUNKS = 8
HEAD_DIM = 64
MAX_REL = 256
HIST = LEFT_CHUNKS * CHUNK

LANES = 128
POOL_BLOCK_ROWS = 256
ATT_BLOCK_ROWS = 256
ATT_SPAN = HIST + LANES
REL_LANES = HIST + ATT_BLOCK_ROWS
VMEM_LIMIT_BYTES = 56 * 1024 * 1024


def _rms_norm(x, g):
    ms = jnp.mean(x * x, axis=-1, keepdims=True)
    return x * lax.rsqrt(ms + RMS_EPS) * g


def _pool_layer_kernel(x_ref, gpre_ref, gpost_ref, win_ref, wg_ref, scale_ref, wout_ref,
                       o_ref, carry_ref, act_ref):
    j = pl.program_id(1)
    tm = x_ref.shape[1]
    width = act_ref.shape[1]
    n_groups = len(POOL_WINDOWS)
    gw = width // n_groups

    @pl.when(j == 0)
    def _():
        carry_ref[...] = jnp.zeros_like(carry_ref)

    x = x_ref[0]
    hb = _rms_norm(x, gpre_ref[...]).astype(jnp.bfloat16)

    t = j * tm + lax.broadcasted_iota(jnp.int32, (tm, 1), 0)

    for gi, w in enumerate(POOL_WINDOWS):
        cols = slice(gi * gw, (gi + 1) * gw)
        a = jnp.dot(hb, win_ref[:, cols], preferred_element_type=jnp.float32)
        ext = jnp.concatenate([carry_ref[:, cols], a], axis=0)
        carry_ref[:, cols] = a[tm - POOL_HALO:, :]
        s = ext
        k = 1
        while k < w:
            s = s + pltpu.roll(s, k, axis=0)
            k *= 2
        cnt = jnp.minimum(t + 1, w).astype(jnp.float32)
        pooled = s[POOL_HALO:, :] / cnt
        mixed = (pooled - a).astype(jnp.bfloat16)
        m = jnp.dot(mixed, wg_ref[gi], preferred_element_type=jnp.float32) * scale_ref[:, cols]
        z = jnp.dot(hb, win_ref[:, width + gi * gw: width + (gi + 1) * gw],
                    preferred_element_type=jnp.float32)
        act_ref[:, cols] = (m * (z * jax.nn.sigmoid(z))).astype(jnp.bfloat16)

    y = jnp.dot(act_ref[...], wout_ref[...], preferred_element_type=jnp.float32)
    o_ref[0] = x + _rms_norm(y, gpost_ref[...])


def _build_bias_table(rb_ref, bias_ref):
    n_heads, rows, cols = bias_ref.shape
    lanes = rb_ref.shape[1]
    u = lax.broadcasted_iota(jnp.int32, (1, lanes), 1)
    in_table = (u > cols) & (u < 2 * MAX_REL)
    jj = lax.broadcasted_iota(jnp.int32, (rows, cols), 0)
    c = lax.broadcasted_iota(jnp.int32, (rows, cols), 1)
    kc, qc = jj // CHUNK, c // CHUNK
    in_band = (kc >= qc) & (kc <= qc + LEFT_CHUNKS)
    for h in range(n_heads):
        far = rb_ref[h:h + 1, 2 * MAX_REL:2 * MAX_REL + 1]
        g = jnp.where(in_table, rb_ref[h:h + 1, :], far)
        t = pltpu.roll(jnp.broadcast_to(g, (rows, lanes)), 0, 1, stride=1, stride_axis=0)
        bias_ref[h] = jnp.where(in_band, t[:, 0:cols] * LOG2_E, -jnp.inf)


def _att_layer_kernel(x_ref, gpre_ref, gpost_ref, wk_ref, wqvz_ref, rb_ref, wout_ref,
                      o_ref, k_ref, vt_ref, qt_ref, gate_ref, gated_ref, bias_ref, s_ref, pt_ref):
    b = pl.program_id(0)
    j = pl.program_id(1)
    tq = x_ref.shape[1]
    width = wout_ref.shape[0]
    n_heads = width // HEAD_DIM
    win = HIST + tq
    half = tq // 2
    span = bias_ref.shape[1]
    pair = 2 * HEAD_DIM
    bf16 = jnp.bfloat16

    @pl.when((b == 0) & (j == 0))
    def _():
        _build_bias_table(rb_ref, bias_ref)

    n_slots = k_ref.shape[0]
    slots = [lax.rem(j + 1 + i, n_slots) for i in range(n_slots)]

    @pl.when(j == 0)
    def _():
        k_ref[...] = jnp.zeros(k_ref.shape, bf16)
        vt_ref[...] = jnp.zeros(vt_ref.shape, bf16)

    x = x_ref[0]
    h = _rms_norm(x, gpre_ref[...])
    hb = h.astype(bf16)
    hbt = h.T.astype(bf16)
    k = jnp.dot(hb, wk_ref[...], preferred_element_type=jnp.float32)
    k_ref[slots[-1]] = k.astype(bf16)
    qt = jnp.dot(wqvz_ref[0:width, :], hbt, preferred_element_type=jnp.float32)
    qt_ref[...] = (qt * (HEAD_DIM ** -0.5 * LOG2_E)).astype(bf16)
    vt = jnp.dot(wqvz_ref[width:2 * width, :], hbt, preferred_element_type=jnp.float32)
    vt_ref[slots[-1]] = vt.astype(bf16)
    zt = jnp.dot(wqvz_ref[2 * width:3 * width, :], hbt, preferred_element_type=jnp.float32)
    gate_ref[...] = zt * jax.nn.sigmoid(zt)

    row = lax.broadcasted_iota(jnp.int32, (win, pair), 0)
    lane = lax.broadcasted_iota(jnp.int32, (win, pair), 1)
    k_invalid = jnp.where((row < HIST - j * tq) & (lane == 0), 1.0, 0.0).astype(bf16)
    r2 = lax.broadcasted_iota(jnp.int32, (pair, tq), 0)
    q_penalty = jnp.where(r2 == 0, -1e30, 0.0).astype(bf16)
    zeros_q = jnp.zeros((HEAD_DIM, tq), bf16)
    ones_v = jnp.ones((16, win), bf16)

    for buf in range(2):
        for e in range(2):
            pt_ref[buf, e, span:win, 0:half] = jnp.zeros((win - span, half), bf16)
            pt_ref[buf, e, 0:win - span, half:tq] = jnp.zeros((win - span, half), bf16)

    def scores(pr):
        p0 = pr * pair
        buf = pr % 2
        q_a = qt_ref[p0:p0 + HEAD_DIM, :]
        q_b = qt_ref[p0 + HEAD_DIM:p0 + pair, :]
        rhs = jnp.concatenate(
            [jnp.concatenate([q_a, zeros_q, q_penalty], axis=0),
             jnp.concatenate([zeros_q, q_b, q_penalty], axis=0)], axis=1)
        k_win = jnp.concatenate([k_ref[s, :, p0:p0 + pair] for s in slots], axis=0)
        lhs = jnp.concatenate([k_win, k_invalid], axis=1)
        partial = [[None, None], [None, None]]
        for r in range(win // half):
            st = jnp.dot(lhs[r * half:(r + 1) * half, :], rhs,
                         preferred_element_type=jnp.float32)
            for e in range(2):
                for m in range(2):
                    if not 0 <= r - m < span // half:
                        continue
                    kr = slice((r - m) * half, (r - m + 1) * half)
                    c0 = e * tq + m * half
                    s = st[:, c0:c0 + half] + bias_ref[2 * pr + e, kr, :]
                    s_ref[buf, e, m, kr, :] = s
                    for i in range(0, half, 8):
                        tile = s[i:i + 8, :]
                        prev = partial[e][m]
                        partial[e][m] = tile if prev is None else jnp.maximum(prev, tile)
        return [[jnp.max(p, axis=0, keepdims=True) for p in pe] for pe in partial]

    def context(pr, maxima):
        buf = pr % 2
        for e in range(2):
            rows = slice((2 * pr + e) * HEAD_DIM, (2 * pr + e + 1) * HEAD_DIM)
            for m in range(2):
                p = jnp.exp2(s_ref[buf, e, m] - maxima[e][m]).astype(bf16)
                pt_ref[buf, e, m * half:m * half + span, m * half:(m + 1) * half] = p
            vt_win = jnp.concatenate([vt_ref[s, rows, :] for s in slots], axis=1)
            v_ext = jnp.concatenate([vt_win, ones_v], axis=0)
            o_ext = jnp.dot(v_ext, pt_ref[buf, e], preferred_element_type=jnp.float32)
            inv = 1.0 / o_ext[HEAD_DIM:HEAD_DIM + 8, :]
            o_t = o_ext[0:HEAD_DIM, :] * jnp.concatenate([inv] * (HEAD_DIM // 8), axis=0)
            gated_ref[rows, :] = (o_t * gate_ref[rows, :]).astype(bf16)

    n_pairs = n_heads // 2
    maxima = scores(0)
    for pr in range(n_pairs):
        next_maxima = scores(pr + 1) if pr + 1 < n_pairs else None
        context(pr, maxima)
        maxima = next_maxima

    yt = jnp.dot(wout_ref[...], gated_ref[...], preferred_element_type=jnp.float32)
    o_ref[0] = x + _rms_norm(yt.T, gpost_ref[...])


def _resident(shape):
    zeros = (0,) * len(shape)
    return pl.BlockSpec(shape, lambda b, j: zeros, pipeline_mode=pl.Buffered(1))


def _row_block(rows, d):
    return pl.BlockSpec((1, rows, d), lambda b, j: (b, j, 0))


_COMPILER_PARAMS = pltpu.CompilerParams(
    dimension_semantics=("arbitrary", "arbitrary"), vmem_limit_bytes=VMEM_LIMIT_BYTES)


def _pool_layer(x, g_pre, g_post, w_in, w_group, scale, w_out):
    b, s, d = x.shape
    width = w_out.shape[0]
    tm = POOL_BLOCK_ROWS
    return pl.pallas_call(
        _pool_layer_kernel,
        grid=(b, s // tm),
        in_specs=[_row_block(tm, d), _resident(g_pre.shape), _resident(g_post.shape),
                  _resident(w_in.shape), _resident(w_group.shape), _resident(scale.shape),
                  _resident(w_out.shape)],
        out_specs=_row_block(tm, d),
        out_shape=jax.ShapeDtypeStruct(x.shape, x.dtype),
        scratch_shapes=[pltpu.VMEM((POOL_HALO, width), jnp.float32),
                        pltpu.VMEM((tm, width), jnp.bfloat16)],
        compiler_params=_COMPILER_PARAMS,
        name="pool_layer",
    )(x, g_pre, g_post, w_in, w_group, scale, w_out)


def _att_layer(x, g_pre, g_post, w_k, w_qvz_t, rel_bias, w_out_t):
    b, s, d = x.shape
    width = w_out_t.shape[1]
    n_heads = width // HEAD_DIM
    tq = ATT_BLOCK_ROWS
    n_slots = HIST // tq + 1
    return pl.pallas_call(
        _att_layer_kernel,
        grid=(b, s // tq),
        in_specs=[_row_block(tq, d), _resident(g_pre.shape), _resident(g_post.shape),
                  _resident(w_k.shape), _resident(w_qvz_t.shape), _resident(rel_bias.shape),
                  _resident(w_out_t.shape)],
        out_specs=_row_block(tq, d),
        out_shape=jax.ShapeDtypeStruct(x.shape, x.dtype),
        scratch_shapes=[pltpu.VMEM((n_slots, tq, width), jnp.bfloat16),
                        pltpu.VMEM((n_slots, width, tq), jnp.bfloat16),
                        pltpu.VMEM((width, tq), jnp.bfloat16),
                        pltpu.VMEM((width, tq), jnp.float32),
                        pltpu.VMEM((width, tq), jnp.bfloat16),
                        pltpu.VMEM((n_heads, ATT_SPAN, LANES), jnp.float32),
                        pltpu.VMEM((2, 2, 2, ATT_SPAN, LANES), jnp.float32),
                        pltpu.VMEM((2, 2, HIST + tq, tq), jnp.bfloat16)],
        compiler_params=_COMPILER_PARAMS,
        name="att_layer",
    )(x, g_pre, g_post, w_k, w_qvz_t, rel_bias, w_out_t)


def kernel(x, norm_pre, norm_post, pool_w_in, pool_w_group, pool_scale, pool_w_out,
           att_w_in, att_rel_bias, att_w_out):
    bf16 = jnp.bfloat16
    x = _pool_layer(x, norm_pre[0:1], norm_post[0:1], pool_w_in[0].astype(bf16),
                    pool_w_group[0].astype(bf16), pool_scale[0:1], pool_w_out[0].astype(bf16))
    w = att_w_in[0]
    width = att_w_out.shape[1]
    w_k = w[:, width:2 * width].astype(bf16)
    w_qvz_t = jnp.concatenate([w[:, 0:width], w[:, 2 * width:4 * width]], axis=1).T.astype(bf16)
    rel_bias = jnp.pad(att_rel_bias[0], ((0, 0), (0, REL_LANES - att_rel_bias.shape[-1])))
    x = _att_layer(x, norm_pre[1:2], norm_post[1:2], w_k, w_qvz_t, rel_bias,
                   att_w_out[0].T.astype(bf16))
    return x
```

```python
import jax
import jax.numpy as jnp
from jax import lax
from jax.experimental import pallas as pl
from jax.experimental.pallas import tpu as pltpu

RMS_EPS = 1e-6
LOG2_E = 1.4426950408889634
POOL_WINDOWS = (2, 4, 8, 16)
POOL_HALO = 16
CHUNK = 64
LEFT_CHUNKS = 8
HEAD_DIM = 64
MAX_REL = 256
HIST = LEFT_CHUNKS * CHUNK

LANES = 128
POOL_BLOCK_ROWS = 512
POOL_SUB_ROWS = 256
ATT_BLOCK_ROWS = 256
ATT_SPAN = HIST + LANES
REL_LANES = HIST + ATT_BLOCK_ROWS
VMEM_LIMIT_BYTES = 56 * 1024 * 1024


def _rms_norm(x, g):
    ms = jnp.mean(x * x, axis=-1, keepdims=True)
    return x * lax.rsqrt(ms + RMS_EPS) * g


def _pool_layer_kernel(x_ref, gpre_ref, gpost_ref, win_ref, wg_ref, scale_ref, wout_ref,
                       o_ref, carry_ref, act_ref):
    j = pl.program_id(1)
    tm = x_ref.shape[1]
    n_sub, sub, width = act_ref.shape
    n_groups = len(POOL_WINDOWS)
    gw = width // n_groups

    @pl.when(j == 0)
    def _():
        carry_ref[...] = jnp.zeros_like(carry_ref)

    for sb in range(n_sub):
        r0 = sb * sub
        x = x_ref[0, r0:r0 + sub, :]
        hb = _rms_norm(x, gpre_ref[...]).astype(jnp.bfloat16)

        t = j * tm + r0 + lax.broadcasted_iota(jnp.int32, (sub, 1), 0)

        def project(gi):
            a = jnp.dot(hb, win_ref[:, gi * gw:(gi + 1) * gw], preferred_element_type=jnp.float32)
            z = jnp.dot(hb, win_ref[:, width + gi * gw: width + (gi + 1) * gw],
                        preferred_element_type=jnp.float32)
            return a, z

        projected = project(0)
        for gi, w in enumerate(POOL_WINDOWS):
            cols = slice(gi * gw, (gi + 1) * gw)
            a, z = projected
            if gi + 1 < n_groups:
                projected = project(gi + 1)
            ext = jnp.concatenate([carry_ref[:, cols], a], axis=0)
            carry_ref[:, cols] = a[sub - POOL_HALO:, :]
            s = ext
            k = 1
            while k < w:
                s = s + pltpu.roll(s, k, axis=0)
                k *= 2
            cnt = jnp.minimum(t + 1, w).astype(jnp.float32)
            pooled = s[POOL_HALO:, :] / cnt
            mixed = (pooled - a).astype(jnp.bfloat16)
            m = jnp.dot(mixed, wg_ref[gi], preferred_element_type=jnp.float32) * scale_ref[:, cols]
            act_ref[sb, :, cols] = (m * (z * jax.nn.sigmoid(z))).astype(jnp.bfloat16)

        y = jnp.dot(act_ref[sb], wout_ref[...], preferred_element_type=jnp.float32)
        o_ref[0, r0:r0 + sub, :] = x + _rms_norm(y, gpost_ref[...])


def _build_bias_table(rb_ref, bias_ref):
    n_heads, rows, cols = bias_ref.shape
    lanes = rb_ref.shape[1]
    u = lax.broadcasted_iota(jnp.int32, (1, lanes), 1)
    in_table = (u > cols) & (u < 2 * MAX_REL)
    jj = lax.broadcasted_iota(jnp.int32, (rows, cols), 0)
    c = lax.broadcasted_iota(jnp.int32, (rows, cols), 1)
    kc, qc = jj // CHUNK, c // CHUNK
    in_band = (kc >= qc) & (kc <= qc + LEFT_CHUNKS)
    for h in range(n_heads):
        far = rb_ref[h:h + 1, 2 * MAX_REL:2 * MAX_REL + 1]
        g = jnp.where(in_table, rb_ref[h:h + 1, :], far)
        t = pltpu.roll(jnp.broadcast_to(g, (rows, lanes)), 0, 1, stride=1, stride_axis=0)
        bias_ref[h] = jnp.where(in_band, t[:, 0:cols] * LOG2_E, -jnp.inf)


def _att_layer_kernel(x_ref, gpre_ref, gpost_ref, wk_ref, wqvz_ref, rb_ref, wout_ref,
                      o_ref, k_ref, vt_ref, qt_ref, gate_ref, gated_ref, bias_ref, s_ref, pt_ref):
    b = pl.program_id(0)
    j = pl.program_id(1)
    tq = x_ref.shape[1]
    width = wout_ref.shape[0]
    n_heads = width // HEAD_DIM
    win = HIST + tq
    half = tq // 2
    span = bias_ref.shape[1]
    pair = 2 * HEAD_DIM
    bf16 = jnp.bfloat16

    @pl.when((b == 0) & (j == 0))
    def _():
        _build_bias_table(rb_ref, bias_ref)

    n_slots = k_ref.shape[0]
    slots = [lax.rem(j + 1 + i, n_slots) for i in range(n_slots)]

    @pl.when(j == 0)
    def _():
        k_ref[...] = jnp.zeros(k_ref.shape, bf16)
        vt_ref[...] = jnp.zeros(vt_ref.shape, bf16)

    x = x_ref[0]
    h = _rms_norm(x, gpre_ref[...])
    hb = h.astype(bf16)
    hbt = h.T.astype(bf16)
    k = jnp.dot(hb, wk_ref[...], preferred_element_type=jnp.float32)
    k_ref[slots[-1]] = k.astype(bf16)
    qt = jnp.dot(wqvz_ref[0:width, :], hbt, preferred_element_type=jnp.float32)
    qt_ref[...] = (qt * (HEAD_DIM ** -0.5 * LOG2_E)).astype(bf16)
    vt = jnp.dot(wqvz_ref[width:2 * width, :], hbt, preferred_element_type=jnp.float32)
    vt_ref[slots[-1]] = vt.astype(bf16)
    zt = jnp.dot(wqvz_ref[2 * width:3 * width, :], hbt, preferred_element_type=jnp.float32)
    gate_ref[...] = zt * jax.nn.sigmoid(zt)

    row = lax.broadcasted_iota(jnp.int32, (win, pair), 0)
    lane = lax.broadcasted_iota(jnp.int32, (win, pair), 1)
    k_invalid = jnp.where((row < HIST - j * tq) & (lane == 0), 1.0, 0.0).astype(bf16)
    r2 = lax.broadcasted_iota(jnp.int32, (pair, tq), 0)
    q_penalty = jnp.where(r2 == 0, -1e30, 0.0).astype(bf16)
    zeros_q = jnp.zeros((HEAD_DIM, tq), bf16)
    ones_v = jnp.ones((16, win), bf16)

    for buf in range(2):
        for e in range(2):
            pt_ref[buf, e, span:win, 0:half] = jnp.zeros((win - span, half), bf16)
            pt_ref[buf, e, 0:win - span, half:tq] = jnp.zeros((win - span, half), bf16)

    def scores(pr):
        p0 = pr * pair
        buf = pr % 2
        q_a = qt_ref[p0:p0 + HEAD_DIM, :]
        q_b = qt_ref[p0 + HEAD_DIM:p0 + pair, :]
        rhs = jnp.concatenate(
            [jnp.concatenate([q_a, zeros_q, q_penalty], axis=0),
             jnp.concatenate([zeros_q, q_b, q_penalty], axis=0)], axis=1)
        k_win = jnp.concatenate([k_ref[s, :, p0:p0 + pair] for s in slots], axis=0)
        lhs = jnp.concatenate([k_win, k_invalid], axis=1)
        partial = [[None, None], [None, None]]
        for r in range(win // half):
            st = jnp.dot(lhs[r * half:(r + 1) * half, :], rhs,
                         preferred_element_type=jnp.float32)
            for e in range(2):
                for m in range(2):
                    if not 0 <= r - m < span // half:
                        continue
                    kr = slice((r - m) * half, (r - m + 1) * half)
                    c0 = e * tq + m * half
                    s = st[:, c0:c0 + half] + bias_ref[2 * pr + e, kr, :]
                    s_ref[buf, e, m, kr, :] = s
                    for i in range(0, half, 8):
                        tile = s[i:i + 8, :]
                        prev = partial[e][m]
                        partial[e][m] = tile if prev is None else jnp.maximum(prev, tile)
        return [[jnp.max(p, axis=0, keepdims=True) for p in pe] for pe in partial]

    def context(pr, maxima):
        buf = pr % 2
        for e in range(2):
            rows = slice((2 * pr + e) * HEAD_DIM, (2 * pr + e + 1) * HEAD_DIM)
            for m in range(2):
                p = jnp.exp2(s_ref[buf, e, m] - maxima[e][m]).astype(bf16)
                pt_ref[buf, e, m * half:m * half + span, m * half:(m + 1) * half] = p
            vt_win = jnp.concatenate([vt_ref[s, rows, :] for s in slots], axis=1)
            v_ext = jnp.concatenate([vt_win, ones_v], axis=0)
            o_ext = jnp.dot(v_ext, pt_ref[buf, e], preferred_element_type=jnp.float32)
            inv = 1.0 / o_ext[HEAD_DIM:HEAD_DIM + 8, :]
            o_t = o_ext[0:HEAD_DIM, :] * jnp.concatenate([inv] * (HEAD_DIM // 8), axis=0)
            gated_ref[rows, :] = (o_t * gate_ref[rows, :]).astype(bf16)

    n_pairs = n_heads // 2
    maxima = scores(0)
    for pr in range(n_pairs):
        next_maxima = scores(pr + 1) if pr + 1 < n_pairs else None
        context(pr, maxima)
        maxima = next_maxima

    yt = jnp.dot(wout_ref[...], gated_ref[...], preferred_element_type=jnp.float32)
    o_ref[0] = x + _rms_norm(yt.T, gpost_ref[...])


def _resident(shape):
    zeros = (0,) * len(shape)
    return pl.BlockSpec(shape, lambda b, j: zeros, pipeline_mode=pl.Buffered(1))


def _row_block(rows, d):
    return pl.BlockSpec((1, rows, d), lambda b, j: (b, j, 0))


_COMPILER_PARAMS = pltpu.CompilerParams(
    dimension_semantics=("arbitrary", "arbitrary"), vmem_limit_bytes=VMEM_LIMIT_BYTES)


def _pool_layer(x, g_pre, g_post, w_in, w_group, scale, w_out):
    b, s, d = x.shape
    width = w_out.shape[0]
    tm = POOL_BLOCK_ROWS
    return pl.pallas_call(
        _pool_layer_kernel,
        grid=(b, s // tm),
        in_specs=[_row_block(tm, d), _resident(g_pre.shape), _resident(g_post.shape),
                  _resident(w_in.shape), _resident(w_group.shape), _resident(scale.shape),
                  _resident(w_out.shape)],
        out_specs=_row_block(tm, d),
        out_shape=jax.ShapeDtypeStruct(x.shape, x.dtype),
        scratch_shapes=[pltpu.VMEM((POOL_HALO, width), jnp.float32),
                        pltpu.VMEM((tm // POOL_SUB_ROWS, POOL_SUB_ROWS, width), jnp.bfloat16)],
        compiler_params=_COMPILER_PARAMS,
        name="pool_layer",
    )(x, g_pre, g_post, w_in, w_group, scale, w_out)


def _att_layer(x, g_pre, g_post, w_k, w_qvz_t, rel_bias, w_out_t):
    b, s, d = x.shape
    width = w_out_t.shape[1]
    n_heads = width // HEAD_DIM
    tq = ATT_BLOCK_ROWS
    n_slots = HIST // tq + 1
    return pl.pallas_call(
        _att_layer_kernel,
        grid=(b, s // tq),
        in_specs=[_row_block(tq, d), _resident(g_pre.shape), _resident(g_post.shape),
                  _resident(w_k.shape), _resident(w_qvz_t.shape), _resident(rel_bias.shape),
                  _resident(w_out_t.shape)],
        out_specs=_row_block(tq, d),
        out_shape=jax.ShapeDtypeStruct(x.shape, x.dtype),
        scratch_shapes=[pltpu.VMEM((n_slots, tq, width), jnp.bfloat16),
                        pltpu.VMEM((n_slots, width, tq), jnp.bfloat16),
                        pltpu.VMEM((width, tq), jnp.bfloat16),
                        pltpu.VMEM((width, tq), jnp.float32),
                        pltpu.VMEM((width, tq), jnp.bfloat16),
                        pltpu.VMEM((n_heads, ATT_SPAN, LANES), jnp.float32),
                        pltpu.VMEM((2, 2, 2, ATT_SPAN, LANES), jnp.float32),
                        pltpu.VMEM((2, 2, HIST + tq, tq), jnp.bfloat16)],
        compiler_params=_COMPILER_PARAMS,
        name="att_layer",
    )(x, g_pre, g_post, w_k, w_qvz_t, rel_bias, w_out_t)


def kernel(x, norm_pre, norm_post, pool_w_in, pool_w_group, pool_scale, pool_w_out,
           att_w_in, att_rel_bias, att_w_out):
    bf16 = jnp.bfloat16
    x = _pool_layer(x, norm_pre[0:1], norm_post[0:1], pool_w_in[0].astype(bf16),
                    pool_w_group[0].astype(bf16), pool_scale[0:1], pool_w_out[0].astype(bf16))
    w = att_w_in[0]
    width = att_w_out.shape[1]
    w_k = w[:, width:2 * width].astype(bf16)
    w_qvz_t = jnp.concatenate([w[:, 0:width], w[:, 2 * width:4 * width]], axis=1).T.astype(bf16)
    rel_bias = jnp.pad(att_rel_bias[0], ((0, 0), (0, REL_LANES - att_rel_bias.shape[-1])))
    x = _att_layer(x, norm_pre[1:2], norm_post[1:2], w_k, w_qvz_t, rel_bias,
                   att_w_out[0].T.astype(bf16))
    return x
```

```python
import jax
import jax.numpy as jnp
from jax import lax
from jax.experimental import pallas as pl
from jax.experimental.pallas import tpu as pltpu

RMS_EPS = 1e-6
LOG2_E = 1.4426950408889634
POOL_WINDOWS = (2, 4, 8, 16)
POOL_HALO = 16
CHUNK = 64
LEFT_CHUNKS = 8
HEAD_DIM = 64
MAX_REL = 256
HIST = LEFT_CHUNKS * CHUNK

LANES = 128
POOL_BLOCK_ROWS = 512
POOL_SUB_ROWS = 256
ATT_BLOCK_ROWS = 256
ATT_SPAN = HIST + LANES
REL_LANES = HIST + ATT_BLOCK_ROWS
VMEM_LIMIT_BYTES = 56 * 1024 * 1024


def _rms_norm(x, g):
    ms = jnp.mean(x * x, axis=-1, keepdims=True)
    return x * lax.rsqrt(ms + RMS_EPS) * g


def _pool_layer_kernel(x_ref, gpre_ref, gpost_ref, win_ref, wg_ref, scale_ref, wout_ref,
                       o_ref, carry_ref, act_ref):
    j = pl.program_id(1)
    tm = x_ref.shape[1]
    n_sub, sub, width = act_ref.shape
    n_groups = len(POOL_WINDOWS)
    gw = width // n_groups

    @pl.when(j == 0)
    def _():
        carry_ref[...] = jnp.zeros_like(carry_ref)

    for sb in range(n_sub):
        r0 = sb * sub
        x = x_ref[0, r0:r0 + sub, :]
        hb = _rms_norm(x, gpre_ref[...]).astype(jnp.bfloat16)

        t = j * tm + r0 + lax.broadcasted_iota(jnp.int32, (sub, 1), 0)

        def project(gi):
            a = jnp.dot(hb, win_ref[:, gi * gw:(gi + 1) * gw], preferred_element_type=jnp.float32)
            z = jnp.dot(hb, win_ref[:, width + gi * gw: width + (gi + 1) * gw],
                        preferred_element_type=jnp.float32)
            return a, z

        projected = project(0)
        for gi, w in enumerate(POOL_WINDOWS):
            cols = slice(gi * gw, (gi + 1) * gw)
            a, z = projected
            if gi + 1 < n_groups:
                projected = project(gi + 1)
            ext = jnp.concatenate([carry_ref[:, cols], a], axis=0)
            carry_ref[:, cols] = a[sub - POOL_HALO:, :]
            s = ext
            k = 1
            while k < w:
                s = s + pltpu.roll(s, k, axis=0)
                k *= 2
            cnt = jnp.minimum(t + 1, w).astype(jnp.float32)
            pooled = s[POOL_HALO:, :] / cnt
            mixed = (pooled - a).astype(jnp.bfloat16)
            m = jnp.dot(mixed, wg_ref[gi], preferred_element_type=jnp.float32) * scale_ref[:, cols]
            act_ref[sb, :, cols] = (m * (z * jax.nn.sigmoid(z))).astype(jnp.bfloat16)

        y = jnp.dot(act_ref[sb], wout_ref[...], preferred_element_type=jnp.float32)
        o_ref[0, r0:r0 + sub, :] = x + _rms_norm(y, gpost_ref[...])


def _build_bias_table(rb_ref, bias_ref):
    n_heads, rows, cols = bias_ref.shape
    lanes = rb_ref.shape[1]
    u = lax.broadcasted_iota(jnp.int32, (1, lanes), 1)
    in_table = (u > cols) & (u < 2 * MAX_REL)
    jj = lax.broadcasted_iota(jnp.int32, (rows, cols), 0)
    c = lax.broadcasted_iota(jnp.int32, (rows, cols), 1)
    kc, qc = jj // CHUNK, c // CHUNK
    in_band = (kc >= qc) & (kc <= qc + LEFT_CHUNKS)
    for h in range(n_heads):
        far = rb_ref[h:h + 1, 2 * MAX_REL:2 * MAX_REL + 1]
        g = jnp.where(in_table, rb_ref[h:h + 1, :], far)
        t = pltpu.roll(jnp.broadcast_to(g, (rows, lanes)), 0, 1, stride=1, stride_axis=0)
        bias_ref[h] = jnp.where(in_band, t[:, 0:cols] * LOG2_E, -jnp.inf)


def _att_layer_kernel(x_ref, gpre_ref, gpost_ref, wk_ref, wqvz_ref, rb_ref, wout_ref,
                      o_ref, k_ref, vt_ref, qt_ref, gate_ref, gated_ref, bias_ref, s_ref, pt_ref):
    b = pl.program_id(0)
    j = pl.program_id(1)
    tq = x_ref.shape[1]
    width = wout_ref.shape[0]
    n_heads = width // HEAD_DIM
    win = HIST + tq
    half = tq // 2
    span = bias_ref.shape[1]
    pair = 2 * HEAD_DIM
    bf16 = jnp.bfloat16

    @pl.when((b == 0) & (j == 0))
    def _():
        _build_bias_table(rb_ref, bias_ref)

    n_slots = k_ref.shape[0]
    slots = [lax.rem(j + 1 + i, n_slots) for i in range(n_slots)]

    @pl.when(j == 0)
    def _():
        k_ref[...] = jnp.zeros(k_ref.shape, bf16)
        vt_ref[...] = jnp.zeros(vt_ref.shape, bf16)

    x = x_ref[0]
    h = _rms_norm(x, gpre_ref[...])
    hb = h.astype(bf16)
    hbt = h.T.astype(bf16)
    k = jnp.dot(hb, wk_ref[...], preferred_element_type=jnp.float32)
    k_ref[slots[-1]] = k.astype(bf16)
    qt = jnp.dot(wqvz_ref[0:width, :], hbt, preferred_element_type=jnp.float32)
    qt_ref[...] = (qt * (HEAD_DIM ** -0.5 * LOG2_E)).astype(bf16)
    vt = jnp.dot(wqvz_ref[width:2 * width, :], hbt, preferred_element_type=jnp.float32)
    vt_ref[slots[-1]] = vt.astype(bf16)
    zt = jnp.dot(wqvz_ref[2 * width:3 * width, :], hbt, preferred_element_type=jnp.float32)
    gate_ref[...] = zt * jax.nn.sigmoid(zt)

    row = lax.broadcasted_iota(jnp.int32, (win, pair), 0)
    lane = lax.broadcasted_iota(jnp.int32, (win, pair), 1)
    k_invalid = jnp.where((row < HIST - j * tq) & (lane == 0), 1.0, 0.0).astype(bf16)
    r2 = lax.broadcasted_iota(jnp.int32, (pair, half), 0)
    q_penalty = jnp.where(r2 == 0, -1e30, 0.0).astype(bf16)
    zeros_q = jnp.zeros((HEAD_DIM, half), bf16)
    ones_v = jnp.ones((16, win), bf16)

    for buf in range(2):
        for e in range(2):
            pt_ref[buf, e, span:win, 0:half] = jnp.zeros((win - span, half), bf16)
            pt_ref[buf, e, 0:win - span, half:tq] = jnp.zeros((win - span, half), bf16)

    def scores(pr):
        p0 = pr * pair
        buf = pr % 2
        q_a = qt_ref[p0:p0 + HEAD_DIM, :]
        q_b = qt_ref[p0 + HEAD_DIM:p0 + pair, :]
        rhs = jnp.concatenate(
            [jnp.concatenate(blk, axis=0)
             for m in range(2)
             for blk in ([q_a[:, m * half:(m + 1) * half], zeros_q, q_penalty],
                         [zeros_q, q_b[:, m * half:(m + 1) * half], q_penalty])], axis=1)
        k_win = jnp.concatenate([k_ref[s, :, p0:p0 + pair] for s in slots], axis=0)
        lhs = jnp.concatenate([k_win, k_invalid], axis=1)
        partial = [[None, None], [None, None]]
        for r in range(win // half):
            ms = [m for m in range(2) if 0 <= r - m < span // half]
            st = jnp.dot(lhs[r * half:(r + 1) * half, :], rhs[:, ms[0] * tq:(ms[-1] + 1) * tq],
                         preferred_element_type=jnp.float32)
            for mi, m in enumerate(ms):
                for e in range(2):
                    kr = slice((r - m) * half, (r - m + 1) * half)
                    c0 = mi * tq + e * half
                    s = st[:, c0:c0 + half] + bias_ref[2 * pr + e, kr, :]
                    s_ref[buf, e, m, kr, :] = s
                    for i in range(0, half, 8):
                        tile = s[i:i + 8, :]
                        prev = partial[e][m]
                        partial[e][m] = tile if prev is None else jnp.maximum(prev, tile)
        return [[jnp.max(p, axis=0, keepdims=True) for p in pe] for pe in partial]

    def context(pr, maxima):
        buf = pr % 2
        for e in range(2):
            rows = slice((2 * pr + e) * HEAD_DIM, (2 * pr + e + 1) * HEAD_DIM)
            for m in range(2):
                p = jnp.exp2(s_ref[buf, e, m] - maxima[e][m]).astype(bf16)
                pt_ref[buf, e, m * half:m * half + span, m * half:(m + 1) * half] = p
            vt_win = jnp.concatenate([vt_ref[s, rows, :] for s in slots], axis=1)
            v_ext = jnp.concatenate([vt_win, ones_v], axis=0)
            o_ext = jnp.dot(v_ext, pt_ref[buf, e], preferred_element_type=jnp.float32)
            inv = 1.0 / o_ext[HEAD_DIM:HEAD_DIM + 8, :]
            o_t = o_ext[0:HEAD_DIM, :] * jnp.concatenate([inv] * (HEAD_DIM // 8), axis=0)
            gated_ref[rows, :] = (o_t * gate_ref[rows, :]).astype(bf16)

    n_pairs = n_heads // 2
    maxima = scores(0)
    for pr in range(n_pairs):
        next_maxima = scores(pr + 1) if pr + 1 < n_pairs else None
        context(pr, maxima)
        maxima = next_maxima

    d_half = wout_ref.shape[0] // 2
    gated = gated_ref[...]
    yt = jnp.concatenate(
        [jnp.dot(wout_ref[i * d_half:(i + 1) * d_half, :], gated, preferred_element_type=jnp.float32)
         for i in range(2)], axis=0)
    o_ref[0] = x + _rms_norm(yt.T, gpost_ref[...])


def _resident(shape):
    zeros = (0,) * len(shape)
    return pl.BlockSpec(shape, lambda b, j: zeros, pipeline_mode=pl.Buffered(1))


def _row_block(rows, d):
    return pl.BlockSpec((1, rows, d), lambda b, j: (b, j, 0))


_COMPILER_PARAMS = pltpu.CompilerParams(
    dimension_semantics=("arbitrary", "arbitrary"), vmem_limit_bytes=VMEM_LIMIT_BYTES)


def _pool_layer(x, g_pre, g_post, w_in, w_group, scale, w_out):
    b, s, d = x.shape
    width = w_out.shape[0]
    tm = POOL_BLOCK_ROWS
    return pl.pallas_call(
        _pool_layer_kernel,
        grid=(b, s // tm),
        in_specs=[_row_block(tm, d), _resident(g_pre.shape), _resident(g_post.shape),
                  _resident(w_in.shape), _resident(w_group.shape), _resident(scale.shape),
                  _resident(w_out.shape)],
        out_specs=_row_block(tm, d),
        out_shape=jax.ShapeDtypeStruct(x.shape, x.dtype),
        scratch_shapes=[pltpu.VMEM((POOL_HALO, width), jnp.float32),
                        pltpu.VMEM((tm // POOL_SUB_ROWS, POOL_SUB_ROWS, width), jnp.bfloat16)],
        compiler_params=_COMPILER_PARAMS,
        name="pool_layer",
    )(x, g_pre, g_post, w_in, w_group, scale, w_out)


def _att_layer(x, g_pre, g_post, w_k, w_qvz_t, rel_bias, w_out_t):
    b, s, d = x.shape
    width = w_out_t.shape[1]
    n_heads = width // HEAD_DIM
    tq = ATT_BLOCK_ROWS
    n_slots = HIST // tq + 1
    return pl.pallas_call(
        _att_layer_kernel,
        grid=(b, s // tq),
        in_specs=[_row_block(tq, d), _resident(g_pre.shape), _resident(g_post.shape),
                  _resident(w_k.shape), _resident(w_qvz_t.shape), _resident(rel_bias.shape),
                  _resident(w_out_t.shape)],
        out_specs=_row_block(tq, d),
        out_shape=jax.ShapeDtypeStruct(x.shape, x.dtype),
        scratch_shapes=[pltpu.VMEM((n_slots, tq, width), jnp.bfloat16),
                        pltpu.VMEM((n_slots, width, tq), jnp.bfloat16),
                        pltpu.VMEM((width, tq), jnp.bfloat16),
                        pltpu.VMEM((width, tq), jnp.float32),
                        pltpu.VMEM((width, tq), jnp.bfloat16),
                        pltpu.VMEM((n_heads, ATT_SPAN, LANES), jnp.float32),
                        pltpu.VMEM((2, 2, 2, ATT_SPAN, LANES), jnp.float32),
                        pltpu.VMEM((2, 2, HIST + tq, tq), jnp.bfloat16)],
        compiler_params=_COMPILER_PARAMS,
        name="att_layer",
    )(x, g_pre, g_post, w_k, w_qvz_t, rel_bias, w_out_t)


def kernel(x, norm_pre, norm_post, pool_w_in, pool_w_group, pool_scale, pool_w_out,
           att_w_in, att_rel_bias, att_w_out):
    bf16 = jnp.bfloat16
    x = _pool_layer(x, norm_pre[0:1], norm_post[0:1], pool_w_in[0].astype(bf16),
                    pool_w_group[0].astype(bf16), pool_scale[0:1], pool_w_out[0].astype(bf16))
    w = att_w_in[0]
    width = att_w_out.shape[1]
    w_k = w[:, width:2 * width].astype(bf16)
    w_qvz_t = jnp.concatenate([w[:, 0:width], w[:, 2 * width:4 * width]], axis=1).T.astype(bf16)
    rel_bias = jnp.pad(att_rel_bias[0], ((0, 0), (0, REL_LANES - att_rel_bias.shape[-1])))
    x = _att_layer(x, norm_pre[1:2], norm_post[1:2], w_k, w_qvz_t, rel_bias,
                   att_w_out[0].T.astype(bf16))
    return x
```

```python
import jax
import jax.numpy as jnp
from jax import lax
from jax.experimental import pallas as pl
from jax.experimental.pallas import tpu as pltpu

RMS_EPS = 1e-6
LOG2_E = 1.4426950408889634
POOL_WINDOWS = (2, 4, 8, 16)
POOL_HALO = 16
CHUNK = 64
LEFT_CHUNKS = 8
HEAD_DIM = 64
MAX_REL = 256
HIST = LEFT_CHUNKS * CHUNK

LANES = 128
POOL_BLOCK_ROWS = 512
POOL_SUB_ROWS = 256
ATT_BLOCK_ROWS = 256
ATT_SPAN = HIST + LANES
REL_LANES = HIST + ATT_BLOCK_ROWS
VMEM_LIMIT_BYTES = 56 * 1024 * 1024


def _rms_norm(x, g):
    ms = jnp.mean(x * x, axis=-1, keepdims=True)
    return x * lax.rsqrt(ms + RMS_EPS) * g


def _pool_layer_kernel(x_ref, gpre_ref, gpost_ref, win_ref, wg_ref, scale_ref, wout_ref,
                       o_ref, carry_ref, act_ref):
    j = pl.program_id(1)
    tm = x_ref.shape[1]
    n_sub, sub, width = act_ref.shape
    n_groups = len(POOL_WINDOWS)
    gw = width // n_groups

    @pl.when(j == 0)
    def _():
        carry_ref[...] = jnp.zeros_like(carry_ref)

    for sb in range(n_sub):
        r0 = sb * sub
        x = x_ref[0, r0:r0 + sub, :]
        hb = _rms_norm(x, gpre_ref[...]).astype(jnp.bfloat16)

        t = j * tm + r0 + lax.broadcasted_iota(jnp.int32, (sub, 1), 0)

        def project(gi):
            a = jnp.dot(hb, win_ref[:, gi * gw:(gi + 1) * gw], preferred_element_type=jnp.float32)
            z = jnp.dot(hb, win_ref[:, width + gi * gw: width + (gi + 1) * gw],
                        preferred_element_type=jnp.float32)
            return a, z

        projected = project(0)
        for gi, w in enumerate(POOL_WINDOWS):
            cols = slice(gi * gw, (gi + 1) * gw)
            a, z = projected
            if gi + 1 < n_groups:
                projected = project(gi + 1)
            ext = jnp.concatenate([carry_ref[:, cols], a], axis=0)
            carry_ref[:, cols] = a[sub - POOL_HALO:, :]
            s = ext
            k = 1
            while k < w:
                s = s + pltpu.roll(s, k, axis=0)
                k *= 2
            cnt = jnp.minimum(t + 1, w).astype(jnp.float32)
            pooled = s[POOL_HALO:, :] / cnt
            mixed = (pooled - a).astype(jnp.bfloat16)
            m = jnp.dot(mixed, wg_ref[gi], preferred_element_type=jnp.float32) * scale_ref[:, cols]
            act_ref[sb, :, cols] = (m * (z * jax.nn.sigmoid(z))).astype(jnp.bfloat16)

        y = jnp.dot(act_ref[sb], wout_ref[...], preferred_element_type=jnp.float32)
        o_ref[0, r0:r0 + sub, :] = x + _rms_norm(y, gpost_ref[...])


def _build_bias_table(rb_ref, bias_ref):
    n_heads, rows, cols = bias_ref.shape
    lanes = rb_ref.shape[1]
    u = lax.broadcasted_iota(jnp.int32, (1, lanes), 1)
    in_table = (u > cols) & (u < 2 * MAX_REL)
    jj = lax.broadcasted_iota(jnp.int32, (rows, cols), 0)
    c = lax.broadcasted_iota(jnp.int32, (rows, cols), 1)
    kc, qc = jj // CHUNK, c // CHUNK
    in_band = (kc >= qc) & (kc <= qc + LEFT_CHUNKS)
    for h in range(n_heads):
        far = rb_ref[h:h + 1, 2 * MAX_REL:2 * MAX_REL + 1]
        g = jnp.where(in_table, rb_ref[h:h + 1, :], far)
        t = pltpu.roll(jnp.broadcast_to(g, (rows, lanes)), 0, 1, stride=1, stride_axis=0)
        bias_ref[h] = jnp.where(in_band, t[:, 0:cols] * LOG2_E, -jnp.inf)


def _att_layer_kernel(x_ref, gpre_ref, gpost_ref, wk_ref, wqvz_ref, rb_ref, wout_ref,
                      o_ref, k_ref, vt_ref, qt_ref, gate_ref, gated_ref, bias_ref, s_ref, pt_ref):
    b = pl.program_id(0)
    j = pl.program_id(1)
    tq = x_ref.shape[1]
    width = wout_ref.shape[0]
    n_heads = width // HEAD_DIM
    win = HIST + tq
    half = tq // 2
    span = bias_ref.shape[1]
    pair = 2 * HEAD_DIM
    bf16 = jnp.bfloat16

    @pl.when((b == 0) & (j == 0))
    def _():
        _build_bias_table(rb_ref, bias_ref)

    n_slots = k_ref.shape[0]
    slots = [lax.rem(j + 1 + i, n_slots) for i in range(n_slots)]

    @pl.when(j == 0)
    def _():
        k_ref[...] = jnp.zeros(k_ref.shape, bf16)
        vt_ref[...] = jnp.zeros(vt_ref.shape, bf16)

    x = x_ref[0]
    h = _rms_norm(x, gpre_ref[...])
    hb = h.astype(bf16)
    hbt = h.T.astype(bf16)
    k = jnp.dot(hb, wk_ref[...], preferred_element_type=jnp.float32)
    k_ref[slots[-1]] = k.astype(bf16)
    qt = jnp.dot(wqvz_ref[0:width, :], hbt, preferred_element_type=jnp.float32)
    qt_ref[...] = (qt * (HEAD_DIM ** -0.5 * LOG2_E)).astype(bf16)
    vt = jnp.dot(wqvz_ref[width:2 * width, :], hbt, preferred_element_type=jnp.float32)
    vt_ref[slots[-1]] = vt.astype(bf16)
    zt = jnp.dot(wqvz_ref[2 * width:3 * width, :], hbt, preferred_element_type=jnp.float32)
    gate_ref[...] = zt * jax.nn.sigmoid(zt)

    row = lax.broadcasted_iota(jnp.int32, (win, pair), 0)
    lane = lax.broadcasted_iota(jnp.int32, (win, pair), 1)
    k_invalid = jnp.where((row < HIST - j * tq) & (lane == 0), 1.0, 0.0).astype(bf16)
    r2 = lax.broadcasted_iota(jnp.int32, (pair, half), 0)
    q_penalty = jnp.where(r2 == 0, -1e30, 0.0).astype(bf16)
    zeros_q = jnp.zeros((HEAD_DIM, half), bf16)
    ones_v = jnp.ones((16, win), bf16)

    for buf in range(2):
        for e in range(2):
            pt_ref[buf, e, span:win, 0:half] = jnp.zeros((win - span, half), bf16)
            pt_ref[buf, e, 0:win - span, half:tq] = jnp.zeros((win - span, half), bf16)

    def scores(pr):
        p0 = pr * pair
        buf = pr % 2
        q_a = qt_ref[p0:p0 + HEAD_DIM, :]
        q_b = qt_ref[p0 + HEAD_DIM:p0 + pair, :]
        rhs = jnp.concatenate(
            [jnp.concatenate(blk, axis=0)
             for m in range(2)
             for blk in ([q_a[:, m * half:(m + 1) * half], zeros_q, q_penalty],
                         [zeros_q, q_b[:, m * half:(m + 1) * half], q_penalty])], axis=1)
        k_win = jnp.concatenate([k_ref[s, :, p0:p0 + pair] for s in slots], axis=0)
        lhs = jnp.concatenate([k_win, k_invalid], axis=1)
        partial = [[None, None], [None, None]]
        for r in range(win // half):
            ms = [m for m in range(2) if 0 <= r - m < span // half]
            st = jnp.dot(lhs[r * half:(r + 1) * half, :], rhs[:, ms[0] * tq:(ms[-1] + 1) * tq],
                         preferred_element_type=jnp.float32)
            for mi, m in enumerate(ms):
                for e in range(2):
                    kr = slice((r - m) * half, (r - m + 1) * half)
                    c0 = mi * tq + e * half
                    s = st[:, c0:c0 + half] + bias_ref[2 * pr + e, kr, :]
                    s_ref[buf, e, m, kr, :] = s
                    for i in range(0, half, 8):
                        tile = s[i:i + 8, :]
                        prev = partial[e][m]
                        partial[e][m] = tile if prev is None else jnp.maximum(prev, tile)
        return [[jnp.max(p, axis=0, keepdims=True) for p in pe] for pe in partial]

    def context(pr, maxima):
        buf = pr % 2
        gated_t = []
        for e in range(2):
            rows = slice((2 * pr + e) * HEAD_DIM, (2 * pr + e + 1) * HEAD_DIM)
            for m in range(2):
                p = jnp.exp2(s_ref[buf, e, m] - maxima[e][m]).astype(bf16)
                pt_ref[buf, e, m * half:m * half + span, m * half:(m + 1) * half] = p
            vt_win = jnp.concatenate([vt_ref[s, rows, :] for s in slots], axis=1)
            v_ext = jnp.concatenate([vt_win, ones_v], axis=0)
            o_ext = jnp.dot(v_ext, pt_ref[buf, e], preferred_element_type=jnp.float32)
            inv = 1.0 / o_ext[HEAD_DIM:HEAD_DIM + 8, :]
            o_t = o_ext[0:HEAD_DIM, :] * jnp.concatenate([inv] * (HEAD_DIM // 8), axis=0)
            gated_t.append(o_t * gate_ref[rows, :])
        gated_ref[:, pr * pair:(pr + 1) * pair] = jnp.concatenate(gated_t, axis=0).T.astype(bf16)

    n_pairs = n_heads // 2
    maxima = scores(0)
    for pr in range(n_pairs):
        next_maxima = scores(pr + 1) if pr + 1 < n_pairs else None
        context(pr, maxima)
        maxima = next_maxima

    for r0 in range(0, tq, half):
        y = jnp.dot(gated_ref[r0:r0 + half, :], wout_ref[...], preferred_element_type=jnp.float32)
        o_ref[0, r0:r0 + half, :] = x[r0:r0 + half, :] + _rms_norm(y, gpost_ref[...])


def _resident(shape):
    zeros = (0,) * len(shape)
    return pl.BlockSpec(shape, lambda b, j: zeros, pipeline_mode=pl.Buffered(1))


def _row_block(rows, d):
    return pl.BlockSpec((1, rows, d), lambda b, j: (b, j, 0))


_COMPILER_PARAMS = pltpu.CompilerParams(
    dimension_semantics=("arbitrary", "arbitrary"), vmem_limit_bytes=VMEM_LIMIT_BYTES)


def _pool_layer(x, g_pre, g_post, w_in, w_group, scale, w_out):
    b, s, d = x.shape
    width = w_out.shape[0]
    tm = POOL_BLOCK_ROWS
    return pl.pallas_call(
        _pool_layer_kernel,
        grid=(b, s // tm),
        in_specs=[_row_block(tm, d), _resident(g_pre.shape), _resident(g_post.shape),
                  _resident(w_in.shape), _resident(w_group.shape), _resident(scale.shape),
                  _resident(w_out.shape)],
        out_specs=_row_block(tm, d),
        out_shape=jax.ShapeDtypeStruct(x.shape, x.dtype),
        scratch_shapes=[pltpu.VMEM((POOL_HALO, width), jnp.float32),
                        pltpu.VMEM((tm // POOL_SUB_ROWS, POOL_SUB_ROWS, width), jnp.bfloat16)],
        compiler_params=_COMPILER_PARAMS,
        name="pool_layer",
    )(x, g_pre, g_post, w_in, w_group, scale, w_out)


def _att_layer(x, g_pre, g_post, w_k, w_qvz_t, rel_bias, w_out):
    b, s, d = x.shape
    width = w_out.shape[0]
    n_heads = width // HEAD_DIM
    tq = ATT_BLOCK_ROWS
    n_slots = HIST // tq + 1
    return pl.pallas_call(
        _att_layer_kernel,
        grid=(b, s // tq),
        in_specs=[_row_block(tq, d), _resident(g_pre.shape), _resident(g_post.shape),
                  _resident(w_k.shape), _resident(w_qvz_t.shape), _resident(rel_bias.shape),
                  _resident(w_out.shape)],
        out_specs=_row_block(tq, d),
        out_shape=jax.ShapeDtypeStruct(x.shape, x.dtype),
        scratch_shapes=[pltpu.VMEM((n_slots, tq, width), jnp.bfloat16),
                        pltpu.VMEM((n_slots, width, tq), jnp.bfloat16),
                        pltpu.VMEM((width, tq), jnp.bfloat16),
                        pltpu.VMEM((width, tq), jnp.float32),
                        pltpu.VMEM((tq, width), jnp.bfloat16),
                        pltpu.VMEM((n_heads, ATT_SPAN, LANES), jnp.float32),
                        pltpu.VMEM((2, 2, 2, ATT_SPAN, LANES), jnp.float32),
                        pltpu.VMEM((2, 2, HIST + tq, tq), jnp.bfloat16)],
        compiler_params=_COMPILER_PARAMS,
        name="att_layer",
    )(x, g_pre, g_post, w_k, w_qvz_t, rel_bias, w_out)


def kernel(x, norm_pre, norm_post, pool_w_in, pool_w_group, pool_scale, pool_w_out,
           att_w_in, att_rel_bias, att_w_out):
    bf16 = jnp.bfloat16
    x = _pool_layer(x, norm_pre[0:1], norm_post[0:1], pool_w_in[0].astype(bf16),
                    pool_w_group[0].astype(bf16), pool_scale[0:1], pool_w_out[0].astype(bf16))
    w = att_w_in[0]
    width = att_w_out.shape[1]
    w_k = w[:, width:2 * width].astype(bf16)
    w_qvz_t = jnp.concatenate([w[:, 0:width], w[:, 2 * width:4 * width]], axis=1).T.astype(bf16)
    rel_bias = jnp.pad(att_rel_bias[0], ((0, 0), (0, REL_LANES - att_rel_bias.shape[-1])))
    x = _att_layer(x, norm_pre[1:2], norm_post[1:2], w_k, w_qvz_t, rel_bias,
                   att_w_out[0].astype(bf16))
    return x
```

```python
import jax
import jax.numpy as jnp
from jax import lax
from jax.experimental import pallas as pl
from jax.experimental.pallas import tpu as pltpu

RMS_EPS = 1e-6
LOG2_E = 1.4426950408889634
SOFTMAX_SUM_LIMIT = 2.0 ** 100
POOL_WINDOWS = (2, 4, 8, 16)
POOL_HALO = 16
CHUNK = 64
LEFT_CHUNKS = 8
HEAD_DIM = 64
MAX_REL = 256
HIST = LEFT_CHUNKS * CHUNK

LANES = 128
POOL_BLOCK_ROWS = 512
POOL_SUB_ROWS = 256
ATT_BLOCK_ROWS = 256
ATT_STAGES = 4
ATT_SPAN = HIST + LANES
REL_LANES = HIST + ATT_BLOCK_ROWS
VMEM_LIMIT_BYTES = 56 * 1024 * 1024


def _rms_norm(x, g):
    ms = jnp.mean(x * x, axis=-1, keepdims=True)
    return x * lax.rsqrt(ms + RMS_EPS) * g


def _pool_layer_kernel(x_ref, gpre_ref, gpost_ref, win_ref, wg_ref, scale_ref, wout_ref,
                       o_ref, carry_ref, act_ref):
    j = pl.program_id(1)
    tm = x_ref.shape[1]
    n_sub, sub, width = act_ref.shape
    n_groups = len(POOL_WINDOWS)
    gw = width // n_groups

    @pl.when(j == 0)
    def _():
        carry_ref[...] = jnp.zeros_like(carry_ref)

    for sb in range(n_sub):
        r0 = sb * sub
        x = x_ref[0, r0:r0 + sub, :]
        hb = _rms_norm(x, gpre_ref[...]).astype(jnp.bfloat16)

        t = j * tm + r0 + lax.broadcasted_iota(jnp.int32, (sub, 1), 0)

        def project(gi):
            a = jnp.dot(hb, win_ref[:, gi * gw:(gi + 1) * gw], preferred_element_type=jnp.float32)
            z = jnp.dot(hb, win_ref[:, width + gi * gw: width + (gi + 1) * gw],
                        preferred_element_type=jnp.float32)
            return a, z

        projected = project(0)
        for gi, w in enumerate(POOL_WINDOWS):
            cols = slice(gi * gw, (gi + 1) * gw)
            a, z = projected
            if gi + 1 < n_groups:
                projected = project(gi + 1)
            ext = jnp.concatenate([carry_ref[:, cols], a], axis=0)
            carry_ref[:, cols] = a[sub - POOL_HALO:, :]
            s = ext
            k = 1
            while k < w:
                s = s + pltpu.roll(s, k, axis=0)
                k *= 2
            cnt = jnp.minimum(t + 1, w).astype(jnp.float32)
            pooled = s[POOL_HALO:, :] / cnt
            mixed = (pooled - a).astype(jnp.bfloat16)
            m = jnp.dot(mixed, wg_ref[gi], preferred_element_type=jnp.float32) * scale_ref[:, cols]
            act_ref[sb, :, cols] = (m * (z * jax.nn.sigmoid(z))).astype(jnp.bfloat16)

        y = jnp.dot(act_ref[sb], wout_ref[...], preferred_element_type=jnp.float32)
        o_ref[0, r0:r0 + sub, :] = x + _rms_norm(y, gpost_ref[...])


def _build_bias_table(rb_ref, bias_ref):
    n_heads, rows, cols = bias_ref.shape
    lanes = rb_ref.shape[1]
    u = lax.broadcasted_iota(jnp.int32, (1, lanes), 1)
    in_table = (u > cols) & (u < 2 * MAX_REL)
    jj = lax.broadcasted_iota(jnp.int32, (rows, cols), 0)
    c = lax.broadcasted_iota(jnp.int32, (rows, cols), 1)
    kc, qc = jj // CHUNK, c // CHUNK
    in_band = (kc >= qc) & (kc <= qc + LEFT_CHUNKS)
    for h in range(n_heads):
        far = rb_ref[h:h + 1, 2 * MAX_REL:2 * MAX_REL + 1]
        g = jnp.where(in_table, rb_ref[h:h + 1, :], far)
        t = pltpu.roll(jnp.broadcast_to(g, (rows, lanes)), 0, 1, stride=1, stride_axis=0)
        bias_ref[h] = jnp.where(in_band, t[:, 0:cols] * LOG2_E, -jnp.inf)


def _att_layer_kernel(x_ref, gpre_ref, gpost_ref, wk_ref, wqvz_ref, rb_ref, wout_ref,
                      o_ref, k_ref, vt_ref, qt_ref, gate_ref, gated_ref, bias_ref, own_ref, s_ref, pt_ref):
    b = pl.program_id(0)
    j = pl.program_id(1)
    tq = x_ref.shape[1]
    width = wout_ref.shape[0]
    n_heads = width // HEAD_DIM
    win = HIST + tq
    half = tq // 2
    span = bias_ref.shape[1]
    pair = 2 * HEAD_DIM
    bf16 = jnp.bfloat16

    @pl.when((b == 0) & (j == 0))
    def _():
        _build_bias_table(rb_ref, bias_ref)

    n_slots = k_ref.shape[0]
    slots = [lax.rem(j + 1 + i, n_slots) for i in range(n_slots)]

    @pl.when(j == 0)
    def _():
        k_ref[...] = jnp.zeros(k_ref.shape, bf16)
        vt_ref[...] = jnp.zeros(vt_ref.shape, bf16)

    x = x_ref[0]
    h = _rms_norm(x, gpre_ref[...])
    hb = h.astype(bf16)
    hbt = h.T.astype(bf16)
    k = jnp.dot(hb, wk_ref[...], preferred_element_type=jnp.float32)
    k_ref[slots[-1]] = k.astype(bf16)
    qt = jnp.dot(wqvz_ref[0:width, :], hbt, preferred_element_type=jnp.float32)
    qt_ref[...] = (qt * (HEAD_DIM ** -0.5 * LOG2_E)).astype(bf16)
    vt = jnp.dot(wqvz_ref[width:2 * width, :], hbt, preferred_element_type=jnp.float32)
    vt_ref[slots[-1]] = vt.astype(bf16)
    zt = jnp.dot(wqvz_ref[2 * width:3 * width, :], hbt, preferred_element_type=jnp.float32)
    gate_ref[...] = zt * jax.nn.sigmoid(zt)

    row = lax.broadcasted_iota(jnp.int32, (win, pair), 0)
    lane = lax.broadcasted_iota(jnp.int32, (win, pair), 1)
    k_aux = jnp.where(lane == 0, jnp.where(row < HIST - j * tq, 1.0, 0.0),
                      jnp.where(lane == 1, 1.0, 0.0)).astype(bf16)
    r2 = lax.broadcasted_iota(jnp.int32, (pair, half), 0)
    zeros_q = jnp.zeros((HEAD_DIM, half), bf16)
    ones_v = jnp.ones((16, win), bf16)
    n_tiles = win // half
    own = HIST // half

    def penalty(shift=None):
        sub = 0.0 if shift is None else jnp.where(r2 == 1, -shift, 0.0)
        return jnp.where(r2 == 0, -1e30, sub).astype(bf16)

    def rhs_tile(pr, m, shifts):
        p0 = pr * pair
        cols = slice(m * half, (m + 1) * half)
        q_a = qt_ref[p0:p0 + HEAD_DIM, cols]
        q_b = qt_ref[p0 + HEAD_DIM:p0 + pair, cols]
        pen = [penalty(None if shifts is None else shifts[e][m]) for e in range(2)]
        return jnp.concatenate([jnp.concatenate([q_a, zeros_q, pen[0]], axis=0),
                                jnp.concatenate([zeros_q, q_b, pen[1]], axis=0)], axis=1)

    def lhs_tile(pr, r):
        s = slots[r // (tq // half)]
        r0 = (r % (tq // half)) * half
        return jnp.concatenate([k_ref[s, r0:r0 + half, pr * pair:(pr + 1) * pair],
                                k_aux[r * half:(r + 1) * half, :]], axis=1)

    def score_dot(lhs, rhs):
        if rhs.shape[1] > tq:
            return jnp.dot(lhs, rhs, preferred_element_type=jnp.float32)
        h0 = lhs.shape[0] // 2
        return jnp.concatenate([jnp.dot(lhs[0:h0, :], rhs, preferred_element_type=jnp.float32),
                                jnp.dot(lhs[h0:, :], rhs, preferred_element_type=jnp.float32)], axis=0)

    def visible(r, skip_own=False):
        return [m for m in range(2)
                if 0 <= r - m < span // half and not (skip_own and r == own + m)]

    n_stage = pt_ref.shape[0]
    for buf in range(n_stage):
        for e in range(2):
            pt_ref[buf, e, span:win, 0:half] = jnp.zeros((win - span, half), bf16)
            pt_ref[buf, e, 0:win - span, half:tq] = jnp.zeros((win - span, half), bf16)

    def attend(pr):
        buf = pr % n_stage
        gated_t, denoms = [], []
        for e in range(2):
            rows = slice((2 * pr + e) * HEAD_DIM, (2 * pr + e + 1) * HEAD_DIM)
            vt_win = jnp.concatenate([vt_ref[s, rows, :] for s in slots], axis=1)
            v_ext = jnp.concatenate([vt_win, ones_v], axis=0)
            o_ext = jnp.dot(v_ext, pt_ref[buf, e], preferred_element_type=jnp.float32)
            denom = o_ext[HEAD_DIM:HEAD_DIM + 8, :]
            o_t = o_ext[0:HEAD_DIM, :] * jnp.concatenate([1.0 / denom] * (HEAD_DIM // 8), axis=0)
            gated_t.append(o_t * gate_ref[rows, :])
            denoms.append(denom)
        gated_ref[:, pr * pair:(pr + 1) * pair] = jnp.concatenate(gated_t, axis=0).T.astype(bf16)
        return denoms

    def project_out():
        for r0 in range(0, tq, half):
            y = jnp.dot(gated_ref[r0:r0 + half, :], wout_ref[...], preferred_element_type=jnp.float32)
            o_ref[0, r0:r0 + half, :] = x[r0:r0 + half, :] + _rms_norm(y, gpost_ref[...])

    def own_scores(pr):
        buf = pr % n_stage
        maxima = [[None, None], [None, None]]
        for m in range(2):
            st = score_dot(lhs_tile(pr, own + m), rhs_tile(pr, m, None))
            for e in range(2):
                s = st[:, e * half:(e + 1) * half] + bias_ref[2 * pr + e, HIST:HIST + half, :]
                own_ref[buf, e, m] = s
                maxima[e][m] = jnp.max(s, axis=0, keepdims=True)
        return maxima

    def fast_probs(pr, maxima):
        buf = pr % n_stage
        shifts = [[mx.astype(bf16).astype(jnp.float32) for mx in me] for me in maxima]
        rhs = jnp.concatenate([rhs_tile(pr, m, shifts) for m in range(2)], axis=1)
        for r in range(n_tiles):
            ms = visible(r, skip_own=True)
            if not ms:
                continue
            st = score_dot(lhs_tile(pr, r), rhs[:, ms[0] * tq:(ms[-1] + 1) * tq])
            for mi, m in enumerate(ms):
                for e in range(2):
                    kr = slice((r - m) * half, (r - m + 1) * half)
                    c0 = mi * tq + e * half
                    p = jnp.exp2(st[:, c0:c0 + half] + bias_ref[2 * pr + e, kr, :])
                    pt_ref[buf, e, r * half:(r + 1) * half, m * half:(m + 1) * half] = p.astype(bf16)
        for m in range(2):
            for e in range(2):
                p = jnp.exp2(own_ref[buf, e, m] - shifts[e][m])
                pt_ref[buf, e, (own + m) * half:(own + m + 1) * half,
                       m * half:(m + 1) * half] = p.astype(bf16)

    n_pairs = n_heads // 2
    overflow = jnp.zeros((8, tq), jnp.float32)
    maxima = {pr: own_scores(pr) for pr in range(min(2, n_pairs))}
    fast_probs(0, maxima.pop(0))
    for pr in range(n_pairs):
        if pr + 2 < n_pairs:
            maxima[pr + 2] = own_scores(pr + 2)
        if pr + 1 < n_pairs:
            fast_probs(pr + 1, maxima.pop(pr + 1))
        for denom in attend(pr):
            overflow = jnp.maximum(overflow, jnp.where(denom < SOFTMAX_SUM_LIMIT, 0.0, 1.0))
    project_out()

    def exact_scores(pr):
        rhs = jnp.concatenate([rhs_tile(pr, m, None) for m in range(2)], axis=1)
        partial = [[None, None], [None, None]]
        for r in range(n_tiles):
            ms = visible(r)
            st = score_dot(lhs_tile(pr, r), rhs[:, ms[0] * tq:(ms[-1] + 1) * tq])
            for mi, m in enumerate(ms):
                for e in range(2):
                    kr = slice((r - m) * half, (r - m + 1) * half)
                    c0 = mi * tq + e * half
                    s = st[:, c0:c0 + half] + bias_ref[2 * pr + e, kr, :]
                    s_ref[e, m, kr, :] = s
                    for i in range(0, half, 8):
                        tile = s[i:i + 8, :]
                        prev = partial[e][m]
                        partial[e][m] = tile if prev is None else jnp.maximum(prev, tile)
        return [[jnp.max(p, axis=0, keepdims=True) for p in pe] for pe in partial]

    @pl.when(jnp.max(overflow) > 0.0)
    def _():
        for pr in range(n_pairs):
            exact_maxima = exact_scores(pr)
            for e in range(2):
                for m in range(2):
                    p = jnp.exp2(s_ref[e, m] - exact_maxima[e][m]).astype(bf16)
                    pt_ref[pr % n_stage, e, m * half:m * half + span, m * half:(m + 1) * half] = p
            attend(pr)
        project_out()


def _resident(shape):
    zeros = (0,) * len(shape)
    return pl.BlockSpec(shape, lambda b, j: zeros, pipeline_mode=pl.Buffered(1))


def _row_block(rows, d):
    return pl.BlockSpec((1, rows, d), lambda b, j: (b, j, 0))


_COMPILER_PARAMS = pltpu.CompilerParams(
    dimension_semantics=("arbitrary", "arbitrary"), vmem_limit_bytes=VMEM_LIMIT_BYTES)


def _pool_layer(x, g_pre, g_post, w_in, w_group, scale, w_out):
    b, s, d = x.shape
    width = w_out.shape[0]
    tm = POOL_BLOCK_ROWS
    return pl.pallas_call(
        _pool_layer_kernel,
        grid=(b, s // tm),
        in_specs=[_row_block(tm, d), _resident(g_pre.shape), _resident(g_post.shape),
                  _resident(w_in.shape), _resident(w_group.shape), _resident(scale.shape),
                  _resident(w_out.shape)],
        out_specs=_row_block(tm, d),
        out_shape=jax.ShapeDtypeStruct(x.shape, x.dtype),
        scratch_shapes=[pltpu.VMEM((POOL_HALO, width), jnp.float32),
                        pltpu.VMEM((tm // POOL_SUB_ROWS, POOL_SUB_ROWS, width), jnp.bfloat16)],
        compiler_params=_COMPILER_PARAMS,
        name="pool_layer",
    )(x, g_pre, g_post, w_in, w_group, scale, w_out)


def _att_layer(x, g_pre, g_post, w_k, w_qvz_t, rel_bias, w_out):
    b, s, d = x.shape
    width = w_out.shape[0]
    n_heads = width // HEAD_DIM
    tq = ATT_BLOCK_ROWS
    n_slots = HIST // tq + 1
    return pl.pallas_call(
        _att_layer_kernel,
        grid=(b, s // tq),
        in_specs=[_row_block(tq, d), _resident(g_pre.shape), _resident(g_post.shape),
                  _resident(w_k.shape), _resident(w_qvz_t.shape), _resident(rel_bias.shape),
                  _resident(w_out.shape)],
        out_specs=_row_block(tq, d),
        out_shape=jax.ShapeDtypeStruct(x.shape, x.dtype),
        scratch_shapes=[pltpu.VMEM((n_slots, tq, width), jnp.bfloat16),
                        pltpu.VMEM((n_slots, width, tq), jnp.bfloat16),
                        pltpu.VMEM((width, tq), jnp.bfloat16),
                        pltpu.VMEM((width, tq), jnp.float32),
                        pltpu.VMEM((tq, width), jnp.bfloat16),
                        pltpu.VMEM((n_heads, ATT_SPAN, LANES), jnp.float32),
                        pltpu.VMEM((ATT_STAGES, 2, 2, LANES, LANES), jnp.float32),
                        pltpu.VMEM((2, 2, ATT_SPAN, LANES), jnp.float32),
                        pltpu.VMEM((ATT_STAGES, 2, HIST + tq, tq), jnp.bfloat16)],
        compiler_params=_COMPILER_PARAMS,
        name="att_layer",
    )(x, g_pre, g_post, w_k, w_qvz_t, rel_bias, w_out)


def kernel(x, norm_pre, norm_post, pool_w_in, pool_w_group, pool_scale, pool_w_out,
           att_w_in, att_rel_bias, att_w_out):
    bf16 = jnp.bfloat16
    x = _pool_layer(x, norm_pre[0:1], norm_post[0:1], pool_w_in[0].astype(bf16),
                    pool_w_group[0].astype(bf16), pool_scale[0:1], pool_w_out[0].astype(bf16))
    w = att_w_in[0]
    width = att_w_out.shape[1]
    w_k = w[:, width:2 * width].astype(bf16)
    w_qvz_t = jnp.concatenate([w[:, 0:width], w[:, 2 * width:4 * width]], axis=1).T.astype(bf16)
    rel_bias = jnp.pad(att_rel_bias[0], ((0, 0), (0, REL_LANES - att_rel_bias.shape[-1])))
    x = _att_layer(x, norm_pre[1:2], norm_post[1:2], w_k, w_qvz_t, rel_bias,
                   att_w_out[0].astype(bf16))
    return x
```

```python
import jax
import jax.numpy as jnp
from jax import lax
from jax.experimental import pallas as pl
from jax.experimental.pallas import tpu as pltpu

RMS_EPS = 1e-6
LOG2_E = 1.4426950408889634
SOFTMAX_SUM_LIMIT = 2.0 ** 100
POOL_WINDOWS = (2, 4, 8, 16)
POOL_HALO = 16
CHUNK = 64
LEFT_CHUNKS = 8
HEAD_DIM = 64
MAX_REL = 256
HIST = LEFT_CHUNKS * CHUNK

LANES = 128
POOL_BLOCK_ROWS = 512
POOL_SUB_ROWS = 256
ATT_BLOCK_ROWS = 256
ATT_T_BLOCKS = (0, 2, 3)
ATT_STAGES = 4
ATT_SPAN = HIST + LANES
REL_LANES = HIST + ATT_BLOCK_ROWS
VMEM_LIMIT_BYTES = 56 * 1024 * 1024


def _rms_norm(x, g):
    ms = jnp.mean(x * x, axis=-1, keepdims=True)
    return x * lax.rsqrt(ms + RMS_EPS) * g


def _pool_layer_kernel(x_ref, gpre_ref, gpost_ref, win_ref, wg_ref, scale_ref, wout_ref,
                       o_ref, carry_ref, act_ref):
    j = pl.program_id(1)
    tm = x_ref.shape[1]
    n_sub, sub, width = act_ref.shape
    n_groups = len(POOL_WINDOWS)
    gw = width // n_groups

    @pl.when(j == 0)
    def _():
        carry_ref[...] = jnp.zeros_like(carry_ref)

    for sb in range(n_sub):
        r0 = sb * sub
        x = x_ref[0, r0:r0 + sub, :]
        hb = _rms_norm(x, gpre_ref[...]).astype(jnp.bfloat16)

        t = j * tm + r0 + lax.broadcasted_iota(jnp.int32, (sub, 1), 0)

        def project(gi):
            a = jnp.dot(hb, win_ref[:, gi * gw:(gi + 1) * gw], preferred_element_type=jnp.float32)
            z = jnp.dot(hb, win_ref[:, width + gi * gw: width + (gi + 1) * gw],
                        preferred_element_type=jnp.float32)
            return a, z

        projected = project(0)
        for gi, w in enumerate(POOL_WINDOWS):
            cols = slice(gi * gw, (gi + 1) * gw)
            a, z = projected
            if gi + 1 < n_groups:
                projected = project(gi + 1)
            ext = jnp.concatenate([carry_ref[:, cols], a], axis=0)
            carry_ref[:, cols] = a[sub - POOL_HALO:, :]
            s = ext
            k = 1
            while k < w:
                s = s + pltpu.roll(s, k, axis=0)
                k *= 2
            cnt = jnp.minimum(t + 1, w).astype(jnp.float32)
            pooled = s[POOL_HALO:, :] / cnt
            mixed = (pooled - a).astype(jnp.bfloat16)
            m = jnp.dot(mixed, wg_ref[gi], preferred_element_type=jnp.float32) * scale_ref[:, cols]
            act_ref[sb, :, cols] = (m * (z * jax.nn.sigmoid(z))).astype(jnp.bfloat16)

        y = jnp.dot(act_ref[sb], wout_ref[...], preferred_element_type=jnp.float32)
        o_ref[0, r0:r0 + sub, :] = x + _rms_norm(y, gpost_ref[...])


def _build_bias_table(rb_ref, bias_ref):
    n_heads, rows, cols = bias_ref.shape
    lanes = rb_ref.shape[1]
    u = lax.broadcasted_iota(jnp.int32, (1, lanes), 1)
    in_table = (u > cols) & (u < 2 * MAX_REL)
    jj = lax.broadcasted_iota(jnp.int32, (rows, cols), 0)
    c = lax.broadcasted_iota(jnp.int32, (rows, cols), 1)
    kc, qc = jj // CHUNK, c // CHUNK
    in_band = (kc >= qc) & (kc <= qc + LEFT_CHUNKS)
    for h in range(n_heads):
        far = rb_ref[h:h + 1, 2 * MAX_REL:2 * MAX_REL + 1]
        g = jnp.where(in_table, rb_ref[h:h + 1, :], far)
        t = pltpu.roll(jnp.broadcast_to(g, (rows, lanes)), 0, 1, stride=1, stride_axis=0)
        bias_ref[h] = jnp.where(in_band, t[:, 0:cols] * LOG2_E, -jnp.inf)


def _att_layer_kernel(x_ref, gpre_ref, gpost_ref, win_ref, rb_ref, wout_ref,
                      o_ref, wt_ref, k_ref, vt_ref, qt_ref, gate_ref, gated_ref, bias_ref, own_ref, s_ref, pt_ref):
    b = pl.program_id(0)
    j = pl.program_id(1)
    tq = x_ref.shape[1]
    width = wout_ref.shape[0]
    n_heads = width // HEAD_DIM
    win = HIST + tq
    half = tq // 2
    span = bias_ref.shape[1]
    pair = 2 * HEAD_DIM
    bf16 = jnp.bfloat16

    @pl.when((b == 0) & (j == 0))
    def _():
        _build_bias_table(rb_ref, bias_ref)
        for t, blk in enumerate(ATT_T_BLOCKS):
            for c0 in range(0, width, tq):
                w_blk = win_ref[:, blk * width + c0:blk * width + c0 + tq]
                wt_ref[t, c0:c0 + tq, :] = w_blk.astype(jnp.float32).T.astype(bf16)

    n_slots = k_ref.shape[0]
    slots = [lax.rem(j + 1 + i, n_slots) for i in range(n_slots)]

    @pl.when(j == 0)
    def _():
        k_ref[...] = jnp.zeros(k_ref.shape, bf16)
        vt_ref[...] = jnp.zeros(vt_ref.shape, bf16)

    x = x_ref[0]
    h = _rms_norm(x, gpre_ref[...])
    hb = h.astype(bf16)
    hbt = h.T.astype(bf16)
    k = jnp.dot(hb, win_ref[:, width:2 * width], preferred_element_type=jnp.float32)
    k_ref[slots[-1]] = k.astype(bf16)
    qt = jnp.dot(wt_ref[0], hbt, preferred_element_type=jnp.float32)
    qt_ref[...] = (qt * (HEAD_DIM ** -0.5 * LOG2_E)).astype(bf16)
    vt = jnp.dot(wt_ref[1], hbt, preferred_element_type=jnp.float32)
    vt_ref[slots[-1]] = vt.astype(bf16)
    zt = jnp.dot(wt_ref[2], hbt, preferred_element_type=jnp.float32)
    gate_ref[...] = zt * jax.nn.sigmoid(zt)

    row = lax.broadcasted_iota(jnp.int32, (win, pair), 0)
    lane = lax.broadcasted_iota(jnp.int32, (win, pair), 1)
    k_aux = jnp.where(lane == 0, jnp.where(row < HIST - j * tq, 1.0, 0.0),
                      jnp.where(lane == 1, 1.0, 0.0)).astype(bf16)
    r2 = lax.broadcasted_iota(jnp.int32, (pair, half), 0)
    zeros_q = jnp.zeros((HEAD_DIM, half), bf16)
    ones_v = jnp.ones((16, win), bf16)
    n_tiles = win // half
    own = HIST // half

    def penalty(shift=None):
        sub = 0.0 if shift is None else jnp.where(r2 == 1, -shift, 0.0)
        return jnp.where(r2 == 0, -1e30, sub).astype(bf16)

    def rhs_tile(pr, m, shifts):
        p0 = pr * pair
        cols = slice(m * half, (m + 1) * half)
        q_a = qt_ref[p0:p0 + HEAD_DIM, cols]
        q_b = qt_ref[p0 + HEAD_DIM:p0 + pair, cols]
        pen = [penalty(None if shifts is None else shifts[e][m]) for e in range(2)]
        return jnp.concatenate([jnp.concatenate([q_a, zeros_q, pen[0]], axis=0),
                                jnp.concatenate([zeros_q, q_b, pen[1]], axis=0)], axis=1)

    def lhs_tile(pr, r):
        s = slots[r // (tq // half)]
        r0 = (r % (tq // half)) * half
        return jnp.concatenate([k_ref[s, r0:r0 + half, pr * pair:(pr + 1) * pair],
                                k_aux[r * half:(r + 1) * half, :]], axis=1)

    def score_dot(lhs, rhs):
        if rhs.shape[1] > tq:
            return jnp.dot(lhs, rhs, preferred_element_type=jnp.float32)
        h0 = lhs.shape[0] // 2
        return jnp.concatenate([jnp.dot(lhs[0:h0, :], rhs, preferred_element_type=jnp.float32),
                                jnp.dot(lhs[h0:, :], rhs, preferred_element_type=jnp.float32)], axis=0)

    def visible(r, skip_own=False):
        return [m for m in range(2)
                if 0 <= r - m < span // half and not (skip_own and r == own + m)]

    n_stage = pt_ref.shape[0]
    for buf in range(n_stage):
        for e in range(2):
            pt_ref[buf, e, span:win, 0:half] = jnp.zeros((win - span, half), bf16)
            pt_ref[buf, e, 0:win - span, half:tq] = jnp.zeros((win - span, half), bf16)

    def attend(pr):
        buf = pr % n_stage
        gated_t, denoms = [], []
        for e in range(2):
            rows = slice((2 * pr + e) * HEAD_DIM, (2 * pr + e + 1) * HEAD_DIM)
            vt_win = jnp.concatenate([vt_ref[s, rows, :] for s in slots], axis=1)
            v_ext = jnp.concatenate([vt_win, ones_v], axis=0)
            o_ext = jnp.dot(v_ext, pt_ref[buf, e], preferred_element_type=jnp.float32)
            denom = o_ext[HEAD_DIM:HEAD_DIM + 8, :]
            o_t = o_ext[0:HEAD_DIM, :] * jnp.concatenate([1.0 / denom] * (HEAD_DIM // 8), axis=0)
            gated_t.append(o_t * gate_ref[rows, :])
            denoms.append(denom)
        gated_ref[:, pr * pair:(pr + 1) * pair] = jnp.concatenate(gated_t, axis=0).T.astype(bf16)
        return denoms

    def project_out():
        for r0 in range(0, tq, half):
            y = jnp.dot(gated_ref[r0:r0 + half, :], wout_ref[...], preferred_element_type=jnp.float32)
            o_ref[0, r0:r0 + half, :] = x[r0:r0 + half, :] + _rms_norm(y, gpost_ref[...])

    def own_scores(pr):
        buf = pr % n_stage
        maxima = [[None, None], [None, None]]
        for m in range(2):
            st = score_dot(lhs_tile(pr, own + m), rhs_tile(pr, m, None))
            for e in range(2):
                s = st[:, e * half:(e + 1) * half] + bias_ref[2 * pr + e, HIST:HIST + half, :]
                own_ref[buf, e, m] = s
                maxima[e][m] = jnp.max(s, axis=0, keepdims=True)
        return maxima

    def fast_probs(pr, maxima):
        buf = pr % n_stage
        shifts = [[mx.astype(bf16).astype(jnp.float32) for mx in me] for me in maxima]
        rhs = jnp.concatenate([rhs_tile(pr, m, shifts) for m in range(2)], axis=1)
        for r in range(n_tiles):
            ms = visible(r, skip_own=True)
            if not ms:
                continue
            st = score_dot(lhs_tile(pr, r), rhs[:, ms[0] * tq:(ms[-1] + 1) * tq])
            for mi, m in enumerate(ms):
                for e in range(2):
                    kr = slice((r - m) * half, (r - m + 1) * half)
                    c0 = mi * tq + e * half
                    p = jnp.exp2(st[:, c0:c0 + half] + bias_ref[2 * pr + e, kr, :])
                    pt_ref[buf, e, r * half:(r + 1) * half, m * half:(m + 1) * half] = p.astype(bf16)
        for m in range(2):
            for e in range(2):
                p = jnp.exp2(own_ref[buf, e, m] - shifts[e][m])
                pt_ref[buf, e, (own + m) * half:(own + m + 1) * half,
                       m * half:(m + 1) * half] = p.astype(bf16)

    n_pairs = n_heads // 2
    overflow = jnp.zeros((8, tq), jnp.float32)
    maxima = {pr: own_scores(pr) for pr in range(min(2, n_pairs))}
    fast_probs(0, maxima.pop(0))
    for pr in range(n_pairs):
        if pr + 2 < n_pairs:
            maxima[pr + 2] = own_scores(pr + 2)
        if pr + 1 < n_pairs:
            fast_probs(pr + 1, maxima.pop(pr + 1))
        for denom in attend(pr):
            overflow = jnp.maximum(overflow, jnp.where(denom < SOFTMAX_SUM_LIMIT, 0.0, 1.0))
    project_out()

    def exact_scores(pr):
        rhs = jnp.concatenate([rhs_tile(pr, m, None) for m in range(2)], axis=1)
        partial = [[None, None], [None, None]]
        for r in range(n_tiles):
            ms = visible(r)
            st = score_dot(lhs_tile(pr, r), rhs[:, ms[0] * tq:(ms[-1] + 1) * tq])
            for mi, m in enumerate(ms):
                for e in range(2):
                    kr = slice((r - m) * half, (r - m + 1) * half)
                    c0 = mi * tq + e * half
                    s = st[:, c0:c0 + half] + bias_ref[2 * pr + e, kr, :]
                    s_ref[e, m, kr, :] = s
                    for i in range(0, half, 8):
                        tile = s[i:i + 8, :]
                        prev = partial[e][m]
                        partial[e][m] = tile if prev is None else jnp.maximum(prev, tile)
        return [[jnp.max(p, axis=0, keepdims=True) for p in pe] for pe in partial]

    @pl.when(jnp.max(overflow) > 0.0)
    def _():
        for pr in range(n_pairs):
            exact_maxima = exact_scores(pr)
            for e in range(2):
                for m in range(2):
                    p = jnp.exp2(s_ref[e, m] - exact_maxima[e][m]).astype(bf16)
                    pt_ref[pr % n_stage, e, m * half:m * half + span, m * half:(m + 1) * half] = p
            attend(pr)
        project_out()


def _resident(shape):
    zeros = (0,) * len(shape)
    return pl.BlockSpec(shape, lambda b, j: zeros, pipeline_mode=pl.Buffered(1))


def _row_block(rows, d):
    return pl.BlockSpec((1, rows, d), lambda b, j: (b, j, 0))


_COMPILER_PARAMS = pltpu.CompilerParams(
    dimension_semantics=("arbitrary", "arbitrary"), vmem_limit_bytes=VMEM_LIMIT_BYTES)


def _pool_layer(x, g_pre, g_post, w_in, w_group, scale, w_out):
    b, s, d = x.shape
    width = w_out.shape[0]
    tm = POOL_BLOCK_ROWS
    return pl.pallas_call(
        _pool_layer_kernel,
        grid=(b, s // tm),
        in_specs=[_row_block(tm, d), _resident(g_pre.shape), _resident(g_post.shape),
                  _resident(w_in.shape), _resident(w_group.shape), _resident(scale.shape),
                  _resident(w_out.shape)],
        out_specs=_row_block(tm, d),
        out_shape=jax.ShapeDtypeStruct(x.shape, x.dtype),
        scratch_shapes=[pltpu.VMEM((POOL_HALO, width), jnp.float32),
                        pltpu.VMEM((tm // POOL_SUB_ROWS, POOL_SUB_ROWS, width), jnp.bfloat16)],
        compiler_params=_COMPILER_PARAMS,
        name="pool_layer",
    )(x, g_pre, g_post, w_in, w_group, scale, w_out)


def _att_layer(x, g_pre, g_post, w_in, rel_bias, w_out):
    b, s, d = x.shape
    width = w_out.shape[0]
    n_heads = width // HEAD_DIM
    tq = ATT_BLOCK_ROWS
    n_slots = HIST // tq + 1
    return pl.pallas_call(
        _att_layer_kernel,
        grid=(b, s // tq),
        in_specs=[_row_block(tq, d), _resident(g_pre.shape), _resident(g_post.shape),
                  _resident(w_in.shape), _resident(rel_bias.shape),
                  _resident(w_out.shape)],
        out_specs=_row_block(tq, d),
        out_shape=jax.ShapeDtypeStruct(x.shape, x.dtype),
        scratch_shapes=[pltpu.VMEM((len(ATT_T_BLOCKS), width, d), jnp.bfloat16),
                        pltpu.VMEM((n_slots, tq, width), jnp.bfloat16),
                        pltpu.VMEM((n_slots, width, tq), jnp.bfloat16),
                        pltpu.VMEM((width, tq), jnp.bfloat16),
                        pltpu.VMEM((width, tq), jnp.float32),
                        pltpu.VMEM((tq, width), jnp.bfloat16),
                        pltpu.VMEM((n_heads, ATT_SPAN, LANES), jnp.float32),
                        pltpu.VMEM((ATT_STAGES, 2, 2, LANES, LANES), jnp.float32),
                        pltpu.VMEM((2, 2, ATT_SPAN, LANES), jnp.float32),
                        pltpu.VMEM((ATT_STAGES, 2, HIST + tq, tq), jnp.bfloat16)],
        compiler_params=_COMPILER_PARAMS,
        name="att_layer",
    )(x, g_pre, g_post, w_in, rel_bias, w_out)


def kernel(x, norm_pre, norm_post, pool_w_in, pool_w_group, pool_scale, pool_w_out,
           att_w_in, att_rel_bias, att_w_out):
    bf16 = jnp.bfloat16
    x = _pool_layer(x, norm_pre[0:1], norm_post[0:1], pool_w_in[0].astype(bf16),
                    pool_w_group[0].astype(bf16), pool_scale[0:1], pool_w_out[0].astype(bf16))
    rel_bias = jnp.pad(att_rel_bias[0], ((0, 0), (0, REL_LANES - att_rel_bias.shape[-1])))
    x = _att_layer(x, norm_pre[1:2], norm_post[1:2], att_w_in[0].astype(bf16), rel_bias,
                   att_w_out[0].astype(bf16))
    return x
```

```python
import jax
import jax.numpy as jnp
from jax import lax
from jax.experimental import pallas as pl
from jax.experimental.pallas import tpu as pltpu

RMS_EPS = 1e-6
LOG2_E = 1.4426950408889634
SOFTMAX_SUM_LIMIT = 2.0 ** 100
POOL_WINDOWS = (2, 4, 8, 16)
POOL_HALO = 16
CHUNK = 64
LEFT_CHUNKS = 8
HEAD_DIM = 64
MAX_REL = 256
HIST = LEFT_CHUNKS * CHUNK

LANES = 128
POOL_BLOCK_ROWS = 512
POOL_SUB_ROWS = 256
ATT_BLOCK_ROWS = 256
ATT_T_BLOCKS = (0, 2, 3)
ATT_STAGES = 4
ATT_SPAN = HIST + LANES
REL_LANES = HIST + ATT_BLOCK_ROWS
VMEM_LIMIT_BYTES = 56 * 1024 * 1024


def _rms_norm(x, g):
    ms = jnp.mean(x * x, axis=-1, keepdims=True)
    return x * lax.rsqrt(ms + RMS_EPS) * g


def _pool_layer_kernel(x_ref, gpre_ref, gpost_ref, win_ref, wg_ref, scale_ref, wout_ref,
                       o_ref, wmix_ref, carry_ref, act_ref):
    b = pl.program_id(0)
    j = pl.program_id(1)
    tm = x_ref.shape[1]
    n_sub, sub, width = act_ref.shape
    n_groups = len(POOL_WINDOWS)
    gw = width // n_groups

    @pl.when((b == 0) & (j == 0))
    def _():
        for gi in range(n_groups):
            cols = slice(gi * gw, (gi + 1) * gw)
            wmix_ref[:, cols] = jnp.dot(win_ref[:, cols], wg_ref[gi],
                                        preferred_element_type=jnp.float32).astype(jnp.bfloat16)

    @pl.when(j == 0)
    def _():
        carry_ref[...] = jnp.zeros_like(carry_ref)

    for sb in range(n_sub):
        r0 = sb * sub
        x = x_ref[0, r0:r0 + sub, :]
        hb = _rms_norm(x, gpre_ref[...]).astype(jnp.bfloat16)

        t = j * tm + r0 + lax.broadcasted_iota(jnp.int32, (sub, 1), 0)

        def project(gi):
            a = jnp.dot(hb, wmix_ref[:, gi * gw:(gi + 1) * gw], preferred_element_type=jnp.float32)
            z = jnp.dot(hb, win_ref[:, width + gi * gw: width + (gi + 1) * gw],
                        preferred_element_type=jnp.float32)
            return a, z

        projected = project(0)
        for gi, w in enumerate(POOL_WINDOWS):
            cols = slice(gi * gw, (gi + 1) * gw)
            a, z = projected
            if gi + 1 < n_groups:
                projected = project(gi + 1)
            ext = jnp.concatenate([carry_ref[:, cols], a], axis=0)
            carry_ref[:, cols] = a[sub - POOL_HALO:, :]
            s = ext
            k = 1
            while k < w:
                s = s + pltpu.roll(s, k, axis=0)
                k *= 2
            cnt = jnp.minimum(t + 1, w).astype(jnp.float32)
            pooled = s[POOL_HALO:, :] / cnt
            m = (pooled - a) * scale_ref[:, cols]
            act_ref[sb, :, cols] = (m * (z * jax.nn.sigmoid(z))).astype(jnp.bfloat16)

        y = jnp.dot(act_ref[sb], wout_ref[...], preferred_element_type=jnp.float32)
        o_ref[0, r0:r0 + sub, :] = x + _rms_norm(y, gpost_ref[...])


def _build_bias_table(rb_ref, bias_ref):
    n_heads, rows, cols = bias_ref.shape
    lanes = rb_ref.shape[1]
    u = lax.broadcasted_iota(jnp.int32, (1, lanes), 1)
    in_table = (u > cols) & (u < 2 * MAX_REL)
    jj = lax.broadcasted_iota(jnp.int32, (rows, cols), 0)
    c = lax.broadcasted_iota(jnp.int32, (rows, cols), 1)
    kc, qc = jj // CHUNK, c // CHUNK
    in_band = (kc >= qc) & (kc <= qc + LEFT_CHUNKS)
    for h in range(n_heads):
        far = rb_ref[h:h + 1, 2 * MAX_REL:2 * MAX_REL + 1]
        g = jnp.where(in_table, rb_ref[h:h + 1, :], far)
        t = pltpu.roll(jnp.broadcast_to(g, (rows, lanes)), 0, 1, stride=1, stride_axis=0)
        bias_ref[h] = jnp.where(in_band, t[:, 0:cols] * LOG2_E, -jnp.inf)


def _att_layer_kernel(x_ref, gpre_ref, gpost_ref, win_ref, rb_ref, wout_ref,
                      o_ref, wt_ref, k_ref, vt_ref, qt_ref, gate_ref, gated_ref, bias_ref, own_ref, s_ref, pt_ref):
    b = pl.program_id(0)
    j = pl.program_id(1)
    tq = x_ref.shape[1]
    width = wout_ref.shape[0]
    n_heads = width // HEAD_DIM
    win = HIST + tq
    half = tq // 2
    span = bias_ref.shape[1]
    pair = 2 * HEAD_DIM
    bf16 = jnp.bfloat16

    @pl.when((b == 0) & (j == 0))
    def _():
        _build_bias_table(rb_ref, bias_ref)
        for t, blk in enumerate(ATT_T_BLOCKS):
            for c0 in range(0, width, tq):
                w_blk = win_ref[:, blk * width + c0:blk * width + c0 + tq]
                wt_ref[t, c0:c0 + tq, :] = w_blk.astype(jnp.float32).T.astype(bf16)

    n_slots = k_ref.shape[0]
    slots = [lax.rem(j + 1 + i, n_slots) for i in range(n_slots)]

    @pl.when(j == 0)
    def _():
        k_ref[...] = jnp.zeros(k_ref.shape, bf16)
        vt_ref[...] = jnp.zeros(vt_ref.shape, bf16)

    x = x_ref[0]
    h = _rms_norm(x, gpre_ref[...])
    hb = h.astype(bf16)
    hbt = h.T.astype(bf16)
    k = jnp.dot(hb, win_ref[:, width:2 * width], preferred_element_type=jnp.float32)
    k_ref[slots[-1]] = k.astype(bf16)
    qt = jnp.dot(wt_ref[0], hbt, preferred_element_type=jnp.float32)
    qt_ref[...] = (qt * (HEAD_DIM ** -0.5 * LOG2_E)).astype(bf16)
    vt = jnp.dot(wt_ref[1], hbt, preferred_element_type=jnp.float32)
    vt_ref[slots[-1]] = vt.astype(bf16)
    zt = jnp.dot(wt_ref[2], hbt, preferred_element_type=jnp.float32)
    gate_ref[...] = zt * jax.nn.sigmoid(zt)

    row = lax.broadcasted_iota(jnp.int32, (win, pair), 0)
    lane = lax.broadcasted_iota(jnp.int32, (win, pair), 1)
    k_aux = jnp.where(lane == 0, jnp.where(row < HIST - j * tq, 1.0, 0.0),
                      jnp.where(lane == 1, 1.0, 0.0)).astype(bf16)
    r2 = lax.broadcasted_iota(jnp.int32, (pair, half), 0)
    zeros_q = jnp.zeros((HEAD_DIM, half), bf16)
    ones_v = jnp.ones((16, win), bf16)
    n_tiles = win // half
    own = HIST // half

    def penalty(shift=None):
        sub = 0.0 if shift is None else jnp.where(r2 == 1, -shift, 0.0)
        return jnp.where(r2 == 0, -1e30, sub).astype(bf16)

    def rhs_tile(pr, m, shifts):
        p0 = pr * pair
        cols = slice(m * half, (m + 1) * half)
        q_a = qt_ref[p0:p0 + HEAD_DIM, cols]
        q_b = qt_ref[p0 + HEAD_DIM:p0 + pair, cols]
        pen = [penalty(None if shifts is None else shifts[e][m]) for e in range(2)]
        return jnp.concatenate([jnp.concatenate([q_a, zeros_q, pen[0]], axis=0),
                                jnp.concatenate([zeros_q, q_b, pen[1]], axis=0)], axis=1)

    def lhs_tile(pr, r):
        s = slots[r // (tq // half)]
        r0 = (r % (tq // half)) * half
        return jnp.concatenate([k_ref[s, r0:r0 + half, pr * pair:(pr + 1) * pair],
                                k_aux[r * half:(r + 1) * half, :]], axis=1)

    def score_dot(lhs, rhs):
        if rhs.shape[1] > tq:
            return jnp.dot(lhs, rhs, preferred_element_type=jnp.float32)
        h0 = lhs.shape[0] // 2
        return jnp.concatenate([jnp.dot(lhs[0:h0, :], rhs, preferred_element_type=jnp.float32),
                                jnp.dot(lhs[h0:, :], rhs, preferred_element_type=jnp.float32)], axis=0)

    def visible(r, skip_own=False):
        return [m for m in range(2)
                if 0 <= r - m < span // half and not (skip_own and r == own + m)]

    n_stage = pt_ref.shape[0]
    for buf in range(n_stage):
        for e in range(2):
            pt_ref[buf, e, span:win, 0:half] = jnp.zeros((win - span, half), bf16)
            pt_ref[buf, e, 0:win - span, half:tq] = jnp.zeros((win - span, half), bf16)

    def attend(pr):
        buf = pr % n_stage
        gated_t, denoms = [], []
        for e in range(2):
            rows = slice((2 * pr + e) * HEAD_DIM, (2 * pr + e + 1) * HEAD_DIM)
            vt_win = jnp.concatenate([vt_ref[s, rows, :] for s in slots], axis=1)
            v_ext = jnp.concatenate([vt_win, ones_v], axis=0)
            o_ext = jnp.dot(v_ext, pt_ref[buf, e], preferred_element_type=jnp.float32)
            denom = o_ext[HEAD_DIM:HEAD_DIM + 8, :]
            o_t = o_ext[0:HEAD_DIM, :] * jnp.concatenate([1.0 / denom] * (HEAD_DIM // 8), axis=0)
            gated_t.append(o_t * gate_ref[rows, :])
            denoms.append(denom)
        gated_ref[:, pr * pair:(pr + 1) * pair] = jnp.concatenate(gated_t, axis=0).T.astype(bf16)
        return denoms

    def project_out():
        for r0 in range(0, tq, half):
            y = jnp.dot(gated_ref[r0:r0 + half, :], wout_ref[...], preferred_element_type=jnp.float32)
            o_ref[0, r0:r0 + half, :] = x[r0:r0 + half, :] + _rms_norm(y, gpost_ref[...])

    def own_scores(pr):
        buf = pr % n_stage
        maxima = [[None, None], [None, None]]
        for m in range(2):
            st = score_dot(lhs_tile(pr, own + m), rhs_tile(pr, m, None))
            for e in range(2):
                s = st[:, e * half:(e + 1) * half] + bias_ref[2 * pr + e, HIST:HIST + half, :]
                own_ref[buf, e, m] = s
                maxima[e][m] = jnp.max(s, axis=0, keepdims=True)
        return maxima

    def fast_probs(pr, maxima):
        buf = pr % n_stage
        shifts = [[mx.astype(bf16).astype(jnp.float32) for mx in me] for me in maxima]
        rhs = jnp.concatenate([rhs_tile(pr, m, shifts) for m in range(2)], axis=1)
        for r in range(n_tiles):
            ms = visible(r, skip_own=True)
            if not ms:
                continue
            st = score_dot(lhs_tile(pr, r), rhs[:, ms[0] * tq:(ms[-1] + 1) * tq])
            for mi, m in enumerate(ms):
                for e in range(2):
                    kr = slice((r - m) * half, (r - m + 1) * half)
                    c0 = mi * tq + e * half
                    p = jnp.exp2(st[:, c0:c0 + half] + bias_ref[2 * pr + e, kr, :])
                    pt_ref[buf, e, r * half:(r + 1) * half, m * half:(m + 1) * half] = p.astype(bf16)
        for m in range(2):
            for e in range(2):
                p = jnp.exp2(own_ref[buf, e, m] - shifts[e][m])
                pt_ref[buf, e, (own + m) * half:(own + m + 1) * half,
                       m * half:(m + 1) * half] = p.astype(bf16)

    n_pairs = n_heads // 2
    overflow = jnp.zeros((8, tq), jnp.float32)
    maxima = {pr: own_scores(pr) for pr in range(min(2, n_pairs))}
    fast_probs(0, maxima.pop(0))
    for pr in range(n_pairs):
        if pr + 2 < n_pairs:
            maxima[pr + 2] = own_scores(pr + 2)
        if pr + 1 < n_pairs:
            fast_probs(pr + 1, maxima.pop(pr + 1))
        for denom in attend(pr):
            overflow = jnp.maximum(overflow, jnp.where(denom < SOFTMAX_SUM_LIMIT, 0.0, 1.0))
    project_out()

    def exact_scores(pr):
        rhs = jnp.concatenate([rhs_tile(pr, m, None) for m in range(2)], axis=1)
        partial = [[None, None], [None, None]]
        for r in range(n_tiles):
            ms = visible(r)
            st = score_dot(lhs_tile(pr, r), rhs[:, ms[0] * tq:(ms[-1] + 1) * tq])
            for mi, m in enumerate(ms):
                for e in range(2):
                    kr = slice((r - m) * half, (r - m + 1) * half)
                    c0 = mi * tq + e * half
                    s = st[:, c0:c0 + half] + bias_ref[2 * pr + e, kr, :]
                    s_ref[e, m, kr, :] = s
                    for i in range(0, half, 8):
                        tile = s[i:i + 8, :]
                        prev = partial[e][m]
                        partial[e][m] = tile if prev is None else jnp.maximum(prev, tile)
        return [[jnp.max(p, axis=0, keepdims=True) for p in pe] for pe in partial]

    @pl.when(jnp.max(overflow) > 0.0)
    def _():
        for pr in range(n_pairs):
            exact_maxima = exact_scores(pr)
            for e in range(2):
                for m in range(2):
                    p = jnp.exp2(s_ref[e, m] - exact_maxima[e][m]).astype(bf16)
                    pt_ref[pr % n_stage, e, m * half:m * half + span, m * half:(m + 1) * half] = p
            attend(pr)
        project_out()


def _resident(shape):
    zeros = (0,) * len(shape)
    return pl.BlockSpec(shape, lambda b, j: zeros, pipeline_mode=pl.Buffered(1))


def _row_block(rows, d):
    return pl.BlockSpec((1, rows, d), lambda b, j: (b, j, 0))


_COMPILER_PARAMS = pltpu.CompilerParams(
    dimension_semantics=("arbitrary", "arbitrary"), vmem_limit_bytes=VMEM_LIMIT_BYTES)


def _pool_layer(x, g_pre, g_post, w_in, w_group, scale, w_out):
    b, s, d = x.shape
    width = w_out.shape[0]
    tm = POOL_BLOCK_ROWS
    return pl.pallas_call(
        _pool_layer_kernel,
        grid=(b, s // tm),
        in_specs=[_row_block(tm, d), _resident(g_pre.shape), _resident(g_post.shape),
                  _resident(w_in.shape), _resident(w_group.shape), _resident(scale.shape),
                  _resident(w_out.shape)],
        out_specs=_row_block(tm, d),
        out_shape=jax.ShapeDtypeStruct(x.shape, x.dtype),
        scratch_shapes=[pltpu.VMEM((d, width), jnp.bfloat16),
                        pltpu.VMEM((POOL_HALO, width), jnp.float32),
                        pltpu.VMEM((tm // POOL_SUB_ROWS, POOL_SUB_ROWS, width), jnp.bfloat16)],
        compiler_params=_COMPILER_PARAMS,
        name="pool_layer",
    )(x, g_pre, g_post, w_in, w_group, scale, w_out)


def _att_layer(x, g_pre, g_post, w_in, rel_bias, w_out):
    b, s, d = x.shape
    width = w_out.shape[0]
    n_heads = width // HEAD_DIM
    tq = ATT_BLOCK_ROWS
    n_slots = HIST // tq + 1
    return pl.pallas_call(
        _att_layer_kernel,
        grid=(b, s // tq),
        in_specs=[_row_block(tq, d), _resident(g_pre.shape), _resident(g_post.shape),
                  _resident(w_in.shape), _resident(rel_bias.shape),
                  _resident(w_out.shape)],
        out_specs=_row_block(tq, d),
        out_shape=jax.ShapeDtypeStruct(x.shape, x.dtype),
        scratch_shapes=[pltpu.VMEM((len(ATT_T_BLOCKS), width, d), jnp.bfloat16),
                        pltpu.VMEM((n_slots, tq, width), jnp.bfloat16),
                        pltpu.VMEM((n_slots, width, tq), jnp.bfloat16),
                        pltpu.VMEM((width, tq), jnp.bfloat16),
                        pltpu.VMEM((width, tq), jnp.float32),
                        pltpu.VMEM((tq, width), jnp.bfloat16),
                        pltpu.VMEM((n_heads, ATT_SPAN, LANES), jnp.float32),
                        pltpu.VMEM((ATT_STAGES, 2, 2, LANES, LANES), jnp.float32),
                        pltpu.VMEM((2, 2, ATT_SPAN, LANES), jnp.float32),
                        pltpu.VMEM((ATT_STAGES, 2, HIST + tq, tq), jnp.bfloat16)],
        compiler_params=_COMPILER_PARAMS,
        name="att_layer",
    )(x, g_pre, g_post, w_in, rel_bias, w_out)


def kernel(x, norm_pre, norm_post, pool_w_in, pool_w_group, pool_scale, pool_w_out,
           att_w_in, att_rel_bias, att_w_out):
    bf16 = jnp.bfloat16
    x = _pool_layer(x, norm_pre[0:1], norm_post[0:1], pool_w_in[0].astype(bf16),
                    pool_w_group[0].astype(bf16), pool_scale[0:1], pool_w_out[0].astype(bf16))
    rel_bias = jnp.pad(att_rel_bias[0], ((0, 0), (0, REL_LANES - att_rel_bias.shape[-1])))
    x = _att_layer(x, norm_pre[1:2], norm_post[1:2], att_w_in[0].astype(bf16), rel_bias,
                   att_w_out[0].astype(bf16))
    return x
```

```python
import jax
import jax.numpy as jnp
from jax import lax
from jax.experimental import pallas as pl
from jax.experimental.pallas import tpu as pltpu

RMS_EPS = 1e-6
LOG2_E = 1.4426950408889634
SOFTMAX_SUM_LIMIT = 2.0 ** 100
POOL_WINDOWS = (2, 4, 8, 16)
POOL_HALO = 16
CHUNK = 64
LEFT_CHUNKS = 8
HEAD_DIM = 64
MAX_REL = 256
HIST = LEFT_CHUNKS * CHUNK

LANES = 128
SUBLANES = 8
BF16_ROWS = 16
V7X_VMEM_BYTES = 64 * 1024 * 1024
MASK_PENALTY = -1e30
POOL_BLOCK_ROWS = 1024
POOL_SUB_ROWS = 256
ATT_BLOCK_ROWS = 256
ATT_T_BLOCKS = (0, 2, 3)
ATT_STAGES = 4
ATT_SPAN = HIST + LANES
REL_LANES = HIST + ATT_BLOCK_ROWS
VMEM_LIMIT_BYTES = V7X_VMEM_BYTES * 7 // 8


def _rms_norm(x, g):
    ms = jnp.mean(x * x, axis=-1, keepdims=True)
    return x * lax.rsqrt(ms + RMS_EPS) * g


def _pool_layer_kernel(x_ref, gpre_ref, gpost_ref, win_ref, wg_ref, scale_ref, wout_ref,
                       o_ref, wmix_ref, carry_ref, act_ref):
    b = pl.program_id(0)
    j = pl.program_id(1)
    tm = x_ref.shape[1]
    n_sub, sub, width = act_ref.shape
    n_groups = len(POOL_WINDOWS)
    gw = width // n_groups

    @pl.when((b == 0) & (j == 0))
    def _():
        for gi in range(n_groups):
            cols = slice(gi * gw, (gi + 1) * gw)
            wmix_ref[:, cols] = jnp.dot(win_ref[:, cols], wg_ref[gi],
                                        preferred_element_type=jnp.float32).astype(jnp.bfloat16)

    @pl.when(j == 0)
    def _():
        carry_ref[...] = jnp.zeros_like(carry_ref)

    for sb in range(n_sub):
        r0 = sb * sub
        x = x_ref[0, r0:r0 + sub, :]
        hb = _rms_norm(x, gpre_ref[...]).astype(jnp.bfloat16)

        t = j * tm + r0 + lax.broadcasted_iota(jnp.int32, (sub, 1), 0)

        def project(gi):
            a = jnp.dot(hb, wmix_ref[:, gi * gw:(gi + 1) * gw], preferred_element_type=jnp.float32)
            z = jnp.dot(hb, win_ref[:, width + gi * gw: width + (gi + 1) * gw],
                        preferred_element_type=jnp.float32)
            return a, z

        projected = project(0)
        for gi, w in enumerate(POOL_WINDOWS):
            cols = slice(gi * gw, (gi + 1) * gw)
            a, z = projected
            if gi + 1 < n_groups:
                projected = project(gi + 1)
            ext = jnp.concatenate([carry_ref[:, cols], a], axis=0)
            carry_ref[:, cols] = a[sub - POOL_HALO:, :]
            s = ext
            k = 1
            while k < w:
                s = s + pltpu.roll(s, k, axis=0)
                k *= 2
            cnt = jnp.minimum(t + 1, w).astype(jnp.float32)
            pooled = s[POOL_HALO:, :] / cnt
            m = (pooled - a) * scale_ref[:, cols]
            act_ref[sb, :, cols] = (m * (z * jax.nn.sigmoid(z))).astype(jnp.bfloat16)

        y = jnp.dot(act_ref[sb], wout_ref[...], preferred_element_type=jnp.float32)
        o_ref[0, r0:r0 + sub, :] = x + _rms_norm(y, gpost_ref[...])


def _build_bias_table(rb_ref, bias_ref):
    n_heads, rows, cols = bias_ref.shape
    lanes = rb_ref.shape[1]
    u = lax.broadcasted_iota(jnp.int32, (1, lanes), 1)
    in_table = (u > cols) & (u < 2 * MAX_REL)
    jj = lax.broadcasted_iota(jnp.int32, (rows, cols), 0)
    c = lax.broadcasted_iota(jnp.int32, (rows, cols), 1)
    kc, qc = jj // CHUNK, c // CHUNK
    in_band = (kc >= qc) & (kc <= qc + LEFT_CHUNKS)
    for h in range(n_heads):
        far = rb_ref[h:h + 1, 2 * MAX_REL:2 * MAX_REL + 1]
        g = jnp.where(in_table, rb_ref[h:h + 1, :], far)
        t = pltpu.roll(jnp.broadcast_to(g, (rows, lanes)), 0, 1, stride=1, stride_axis=0)
        bias_ref[h] = jnp.where(in_band, t[:, 0:cols] * LOG2_E, -jnp.inf)


def _att_layer_kernel(x_ref, gpre_ref, gpost_ref, win_ref, rb_ref, wout_ref,
                      o_ref, wt_ref, k_ref, vt_ref, qt_ref, gate_ref, gated_ref, bias_ref, own_ref, s_ref, pt_ref):
    b = pl.program_id(0)
    j = pl.program_id(1)
    tq = x_ref.shape[1]
    width = wout_ref.shape[0]
    n_heads = width // HEAD_DIM
    win = HIST + tq
    half = tq // 2
    span = bias_ref.shape[1]
    pair = 2 * HEAD_DIM
    bf16 = jnp.bfloat16

    @pl.when((b == 0) & (j == 0))
    def _():
        _build_bias_table(rb_ref, bias_ref)
        for t, blk in enumerate(ATT_T_BLOCKS):
            for c0 in range(0, width, tq):
                w_blk = win_ref[:, blk * width + c0:blk * width + c0 + tq]
                wt_ref[t, c0:c0 + tq, :] = w_blk.astype(jnp.float32).T.astype(bf16)

    n_slots = k_ref.shape[0]
    slots = [lax.rem(j + 1 + i, n_slots) for i in range(n_slots)]

    @pl.when(j == 0)
    def _():
        k_ref[...] = jnp.zeros(k_ref.shape, bf16)
        vt_ref[...] = jnp.zeros(vt_ref.shape, bf16)

    x = x_ref[0]
    h = _rms_norm(x, gpre_ref[...])
    hb = h.astype(bf16)
    hbt = h.T.astype(bf16)
    k = jnp.dot(hb, win_ref[:, width:2 * width], preferred_element_type=jnp.float32)
    k_ref[slots[-1]] = k.astype(bf16)
    qt = jnp.dot(wt_ref[0], hbt, preferred_element_type=jnp.float32)
    qt_ref[...] = (qt * (HEAD_DIM ** -0.5 * LOG2_E)).astype(bf16)
    vt = jnp.dot(wt_ref[1], hbt, preferred_element_type=jnp.float32)
    vt_ref[slots[-1]] = vt.astype(bf16)
    zt = jnp.dot(wt_ref[2], hbt, preferred_element_type=jnp.float32)
    gate_ref[...] = zt * jax.nn.sigmoid(zt)

    row = lax.broadcasted_iota(jnp.int32, (win, pair), 0)
    lane = lax.broadcasted_iota(jnp.int32, (win, pair), 1)
    k_aux = jnp.where(lane == 0, jnp.where(row < HIST - j * tq, 1.0, 0.0),
                      jnp.where(lane == 1, 1.0, 0.0)).astype(bf16)
    r2 = lax.broadcasted_iota(jnp.int32, (pair, half), 0)
    zeros_q = jnp.zeros((HEAD_DIM, half), bf16)
    ones_v = jnp.ones((BF16_ROWS, win), bf16)
    n_tiles = win // half
    own = HIST // half

    def penalty(shift=None):
        sub = 0.0 if shift is None else jnp.where(r2 == 1, -shift, 0.0)
        return jnp.where(r2 == 0, MASK_PENALTY, sub).astype(bf16)

    def rhs_tile(pr, m, shifts):
        p0 = pr * pair
        cols = slice(m * half, (m + 1) * half)
        q_a = qt_ref[p0:p0 + HEAD_DIM, cols]
        q_b = qt_ref[p0 + HEAD_DIM:p0 + pair, cols]
        pen = [penalty(None if shifts is None else shifts[e][m]) for e in range(2)]
        return jnp.concatenate([jnp.concatenate([q_a, zeros_q, pen[0]], axis=0),
                                jnp.concatenate([zeros_q, q_b, pen[1]], axis=0)], axis=1)

    def lhs_tile(pr, r):
        s = slots[r // (tq // half)]
        r0 = (r % (tq // half)) * half
        return jnp.concatenate([k_ref[s, r0:r0 + half, pr * pair:(pr + 1) * pair],
                                k_aux[r * half:(r + 1) * half, :]], axis=1)

    def score_dot(lhs, rhs):
        if rhs.shape[1] > tq:
            return jnp.dot(lhs, rhs, preferred_element_type=jnp.float32)
        h0 = lhs.shape[0] // 2
        return jnp.concatenate([jnp.dot(lhs[0:h0, :], rhs, preferred_element_type=jnp.float32),
                                jnp.dot(lhs[h0:, :], rhs, preferred_element_type=jnp.float32)], axis=0)

    def visible(r, skip_own=False):
        return [m for m in range(2)
                if 0 <= r - m < span // half and not (skip_own and r == own + m)]

    n_stage = pt_ref.shape[0]
    for buf in range(n_stage):
        for e in range(2):
            pt_ref[buf, e, span:win, 0:half] = jnp.zeros((win - span, half), bf16)
            pt_ref[buf, e, 0:win - span, half:tq] = jnp.zeros((win - span, half), bf16)

    def attend(pr):
        buf = pr % n_stage
        gated_t, denoms = [], []
        for e in range(2):
            rows = slice((2 * pr + e) * HEAD_DIM, (2 * pr + e + 1) * HEAD_DIM)
            vt_win = jnp.concatenate([vt_ref[s, rows, :] for s in slots], axis=1)
            v_ext = jnp.concatenate([vt_win, ones_v], axis=0)
            o_ext = jnp.dot(v_ext, pt_ref[buf, e], preferred_element_type=jnp.float32)
            denom = o_ext[HEAD_DIM:HEAD_DIM + SUBLANES, :]
            o_t = o_ext[0:HEAD_DIM, :] * jnp.concatenate([1.0 / denom] * (HEAD_DIM // SUBLANES), axis=0)
            gated_t.append(o_t * gate_ref[rows, :])
            denoms.append(denom)
        gated_ref[:, pr * pair:(pr + 1) * pair] = jnp.concatenate(gated_t, axis=0).T.astype(bf16)
        return denoms

    def project_out():
        for r0 in range(0, tq, half):
            y = jnp.dot(gated_ref[r0:r0 + half, :], wout_ref[...], preferred_element_type=jnp.float32)
            o_ref[0, r0:r0 + half, :] = x[r0:r0 + half, :] + _rms_norm(y, gpost_ref[...])

    def own_scores(pr):
        buf = pr % n_stage
        maxima = [[None, None], [None, None]]
        for m in range(2):
            st = score_dot(lhs_tile(pr, own + m), rhs_tile(pr, m, None))
            for e in range(2):
                s = st[:, e * half:(e + 1) * half] + bias_ref[2 * pr + e, HIST:HIST + half, :]
                own_ref[buf, e, m] = s
                maxima[e][m] = jnp.max(s, axis=0, keepdims=True)
        return maxima

    def fast_probs(pr, maxima):
        buf = pr % n_stage
        shifts = [[mx.astype(bf16).astype(jnp.float32) for mx in me] for me in maxima]
        rhs = jnp.concatenate([rhs_tile(pr, m, shifts) for m in range(2)], axis=1)
        for r in range(n_tiles):
            ms = visible(r, skip_own=True)
            if not ms:
                continue
            st = score_dot(lhs_tile(pr, r), rhs[:, ms[0] * tq:(ms[-1] + 1) * tq])
            for mi, m in enumerate(ms):
                for e in range(2):
                    kr = slice((r - m) * half, (r - m + 1) * half)
                    c0 = mi * tq + e * half
                    p = jnp.exp2(st[:, c0:c0 + half] + bias_ref[2 * pr + e, kr, :])
                    pt_ref[buf, e, r * half:(r + 1) * half, m * half:(m + 1) * half] = p.astype(bf16)
        for m in range(2):
            for e in range(2):
                p = jnp.exp2(own_ref[buf, e, m] - shifts[e][m])
                pt_ref[buf, e, (own + m) * half:(own + m + 1) * half,
                       m * half:(m + 1) * half] = p.astype(bf16)

    n_pairs = n_heads // 2
    overflow = jnp.zeros((SUBLANES, tq), jnp.float32)
    maxima = {pr: own_scores(pr) for pr in range(min(2, n_pairs))}
    fast_probs(0, maxima.pop(0))
    for pr in range(n_pairs):
        if pr + 2 < n_pairs:
            maxima[pr + 2] = own_scores(pr + 2)
        if pr + 1 < n_pairs:
            fast_probs(pr + 1, maxima.pop(pr + 1))
        for denom in attend(pr):
            overflow = jnp.maximum(overflow, jnp.where(denom < SOFTMAX_SUM_LIMIT, 0.0, 1.0))
    project_out()

    def exact_scores(pr):
        rhs = jnp.concatenate([rhs_tile(pr, m, None) for m in range(2)], axis=1)
        partial = [[None, None], [None, None]]
        for r in range(n_tiles):
            ms = visible(r)
            st = score_dot(lhs_tile(pr, r), rhs[:, ms[0] * tq:(ms[-1] + 1) * tq])
            for mi, m in enumerate(ms):
                for e in range(2):
                    kr = slice((r - m) * half, (r - m + 1) * half)
                    c0 = mi * tq + e * half
                    s = st[:, c0:c0 + half] + bias_ref[2 * pr + e, kr, :]
                    s_ref[e, m, kr, :] = s
                    for i in range(0, half, SUBLANES):
                        tile = s[i:i + SUBLANES, :]
                        prev = partial[e][m]
                        partial[e][m] = tile if prev is None else jnp.maximum(prev, tile)
        return [[jnp.max(p, axis=0, keepdims=True) for p in pe] for pe in partial]

    @pl.when(jnp.max(overflow) > 0.0)
    def _():
        for pr in range(n_pairs):
            exact_maxima = exact_scores(pr)
            for e in range(2):
                for m in range(2):
                    p = jnp.exp2(s_ref[e, m] - exact_maxima[e][m]).astype(bf16)
                    pt_ref[pr % n_stage, e, m * half:m * half + span, m * half:(m + 1) * half] = p
            attend(pr)
        project_out()


def _resident(shape):
    zeros = (0,) * len(shape)
    return pl.BlockSpec(shape, lambda b, j: zeros, pipeline_mode=pl.Buffered(1))


def _row_block(rows, d):
    return pl.BlockSpec((1, rows, d), lambda b, j: (b, j, 0))


_COMPILER_PARAMS = pltpu.CompilerParams(
    dimension_semantics=("arbitrary", "arbitrary"), vmem_limit_bytes=VMEM_LIMIT_BYTES)


def _pool_layer(x, g_pre, g_post, w_in, w_group, scale, w_out):
    b, s, d = x.shape
    width = w_out.shape[0]
    tm = POOL_BLOCK_ROWS
    return pl.pallas_call(
        _pool_layer_kernel,
        grid=(b, s // tm),
        in_specs=[_row_block(tm, d), _resident(g_pre.shape), _resident(g_post.shape),
                  _resident(w_in.shape), _resident(w_group.shape), _resident(scale.shape),
                  _resident(w_out.shape)],
        out_specs=_row_block(tm, d),
        out_shape=jax.ShapeDtypeStruct(x.shape, x.dtype),
        scratch_shapes=[pltpu.VMEM((d, width), jnp.bfloat16),
                        pltpu.VMEM((POOL_HALO, width), jnp.float32),
                        pltpu.VMEM((tm // POOL_SUB_ROWS, POOL_SUB_ROWS, width), jnp.bfloat16)],
        compiler_params=_COMPILER_PARAMS,
        name="pool_layer",
    )(x, g_pre, g_post, w_in, w_group, scale, w_out)


def _att_layer(x, g_pre, g_post, w_in, rel_bias, w_out):
    b, s, d = x.shape
    width = w_out.shape[0]
    n_heads = width // HEAD_DIM
    tq = ATT_BLOCK_ROWS
    n_slots = HIST // tq + 1
    return pl.pallas_call(
        _att_layer_kernel,
        grid=(b, s // tq),
        in_specs=[_row_block(tq, d), _resident(g_pre.shape), _resident(g_post.shape),
                  _resident(w_in.shape), _resident(rel_bias.shape),
                  _resident(w_out.shape)],
        out_specs=_row_block(tq, d),
        out_shape=jax.ShapeDtypeStruct(x.shape, x.dtype),
        scratch_shapes=[pltpu.VMEM((len(ATT_T_BLOCKS), width, d), jnp.bfloat16),
                        pltpu.VMEM((n_slots, tq, width), jnp.bfloat16),
                        pltpu.VMEM((n_slots, width, tq), jnp.bfloat16),
                        pltpu.VMEM((width, tq), jnp.bfloat16),
                        pltpu.VMEM((width, tq), jnp.float32),
                        pltpu.VMEM((tq, width), jnp.bfloat16),
                        pltpu.VMEM((n_heads, ATT_SPAN, LANES), jnp.float32),
                        pltpu.VMEM((ATT_STAGES, 2, 2, LANES, LANES), jnp.float32),
                        pltpu.VMEM((2, 2, ATT_SPAN, LANES), jnp.float32),
                        pltpu.VMEM((ATT_STAGES, 2, HIST + tq, tq), jnp.bfloat16)],
        compiler_params=_COMPILER_PARAMS,
        name="att_layer",
    )(x, g_pre, g_post, w_in, rel_bias, w_out)


def kernel(x, norm_pre, norm_post, pool_w_in, pool_w_group, pool_scale, pool_w_out,
           att_w_in, att_rel_bias, att_w_out):
    bf16 = jnp.bfloat16
    x = _pool_layer(x, norm_pre[0:1], norm_post[0:1], pool_w_in[0].astype(bf16),
                    pool_w_group[0].astype(bf16), pool_scale[0:1], pool_w_out[0].astype(bf16))
    rel_bias = jnp.pad(att_rel_bias[0], ((0, 0), (0, REL_LANES - att_rel_bias.shape[-1])))
    x = _att_layer(x, norm_pre[1:2], norm_post[1:2], att_w_in[0].astype(bf16), rel_bias,
                   att_w_out[0].astype(bf16))
    return x
```

```python
import functools

import jax
import jax.numpy as jnp
from jax import lax
from jax.experimental import pallas as pl
from jax.experimental.pallas import tpu as pltpu

RMS_EPS = 1e-6
LOG2_E = 1.4426950408889634
SOFTMAX_SUM_LIMIT = 2.0 ** 100
POOL_WINDOWS = (2, 4, 8, 16)
POOL_HALO = 16
CHUNK = 64
LEFT_CHUNKS = 8
HEAD_DIM = 64
MAX_REL = 256
HIST = LEFT_CHUNKS * CHUNK

LANES = 128
SUBLANES = 8
BF16_ROWS = 16
V7X_VMEM_BYTES = 64 * 1024 * 1024
MASK_PENALTY = -1e30
POOL_BLOCK_ROWS = 512
POOL_SUB_ROWS = 256
ATT_BLOCK_ROWS = 256
ATT_T_BLOCKS = (0, 2, 3)
ATT_STAGES = 4
ATT_SPAN = HIST + LANES
REL_LANES = HIST + ATT_BLOCK_ROWS
VMEM_LIMIT_BYTES = V7X_VMEM_BYTES * 7 // 8


def _rms_norm(x, g):
    ms = jnp.mean(x * x, axis=-1, keepdims=True)
    return x * lax.rsqrt(ms + RMS_EPS) * g


def _pool_layer_kernel(layer, x_ref, gpre_ref, gpost_ref, win_ref, wg_ref, scale_ref, wout_ref,
                       o_ref, wmix_ref, carry_ref, act_ref):
    b = pl.program_id(0)
    j = pl.program_id(1)
    tm = x_ref.shape[1]
    n_sub, sub, width = act_ref.shape
    n_groups = len(POOL_WINDOWS)
    gw = width // n_groups

    @pl.when((b == 0) & (j == 0))
    def _():
        for gi in range(n_groups):
            cols = slice(gi * gw, (gi + 1) * gw)
            wmix_ref[:, cols] = jnp.dot(win_ref[:, cols], wg_ref[gi],
                                        preferred_element_type=jnp.float32).astype(jnp.bfloat16)

    @pl.when(j == 0)
    def _():
        carry_ref[...] = jnp.zeros_like(carry_ref)

    for sb in range(n_sub):
        r0 = sb * sub
        x = x_ref[0, r0:r0 + sub, :]
        hb = _rms_norm(x, gpre_ref[layer:layer + 1, :]).astype(jnp.bfloat16)

        t = j * tm + r0 + lax.broadcasted_iota(jnp.int32, (sub, 1), 0)

        def project(gi):
            a = jnp.dot(hb, wmix_ref[:, gi * gw:(gi + 1) * gw], preferred_element_type=jnp.float32)
            z = jnp.dot(hb, win_ref[:, width + gi * gw: width + (gi + 1) * gw],
                        preferred_element_type=jnp.float32)
            return a, z

        projected = project(0)
        for gi, w in enumerate(POOL_WINDOWS):
            cols = slice(gi * gw, (gi + 1) * gw)
            a, z = projected
            if gi + 1 < n_groups:
                projected = project(gi + 1)
            ext = jnp.concatenate([carry_ref[:, cols], a], axis=0)
            carry_ref[:, cols] = a[sub - POOL_HALO:, :]
            s = ext
            k = 1
            while k < w:
                s = s + pltpu.roll(s, k, axis=0)
                k *= 2
            cnt = jnp.minimum(t + 1, w).astype(jnp.float32)
            pooled = s[POOL_HALO:, :] / cnt
            m = (pooled - a) * scale_ref[:, cols]
            act_ref[sb, :, cols] = (m * (z * jax.nn.sigmoid(z))).astype(jnp.bfloat16)

        y = jnp.dot(act_ref[sb], wout_ref[...], preferred_element_type=jnp.float32)
        o_ref[0, r0:r0 + sub, :] = x + _rms_norm(y, gpost_ref[layer:layer + 1, :])


def _build_bias_table(rb_ref, bias_ref):
    n_heads, rows, cols = bias_ref.shape
    lanes = REL_LANES
    n_rel = 2 * MAX_REL
    u = lax.broadcasted_iota(jnp.int32, (1, lanes), 1)
    in_table = (u > cols) & (u < n_rel)
    jj = lax.broadcasted_iota(jnp.int32, (rows, cols), 0)
    c = lax.broadcasted_iota(jnp.int32, (rows, cols), 1)
    kc, qc = jj // CHUNK, c // CHUNK
    in_band = (kc >= qc) & (kc <= qc + LEFT_CHUNKS)
    for h in range(n_heads):
        far = rb_ref[0, h:h + 1, n_rel:n_rel + 1]
        row = jnp.concatenate([rb_ref[0, h:h + 1, 0:n_rel], jnp.zeros((1, lanes - n_rel), jnp.float32)], axis=1)
        g = jnp.where(in_table, row, far)
        t = pltpu.roll(jnp.broadcast_to(g, (rows, lanes)), 0, 1, stride=1, stride_axis=0)
        bias_ref[h] = jnp.where(in_band, t[:, 0:cols] * LOG2_E, -jnp.inf)


def _att_layer_kernel(layer, x_ref, gpre_ref, gpost_ref, win_ref, rb_ref, wout_ref,
                      o_ref, wt_ref, k_ref, vt_ref, qt_ref, gate_ref, gated_ref, bias_ref, own_ref, s_ref, pt_ref):
    b = pl.program_id(0)
    j = pl.program_id(1)
    tq = x_ref.shape[1]
    width = wout_ref.shape[0]
    n_heads = width // HEAD_DIM
    win = HIST + tq
    half = tq // 2
    span = bias_ref.shape[1]
    pair = 2 * HEAD_DIM
    bf16 = jnp.bfloat16

    @pl.when((b == 0) & (j == 0))
    def _():
        _build_bias_table(rb_ref, bias_ref)
        for t, blk in enumerate(ATT_T_BLOCKS):
            for c0 in range(0, width, tq):
                w_blk = win_ref[:, blk * width + c0:blk * width + c0 + tq]
                wt_ref[t, c0:c0 + tq, :] = w_blk.astype(jnp.float32).T.astype(bf16)

    n_slots = k_ref.shape[0]
    slots = [lax.rem(j + 1 + i, n_slots) for i in range(n_slots)]

    @pl.when(j == 0)
    def _():
        k_ref[...] = jnp.zeros(k_ref.shape, bf16)
        vt_ref[...] = jnp.zeros(vt_ref.shape, bf16)

    x = x_ref[0]
    h = _rms_norm(x, gpre_ref[layer:layer + 1, :])
    hb = h.astype(bf16)
    hbt = h.T.astype(bf16)
    k = jnp.dot(hb, win_ref[:, width:2 * width], preferred_element_type=jnp.float32)
    k_ref[slots[-1]] = k.astype(bf16)
    qt = jnp.dot(wt_ref[0], hbt, preferred_element_type=jnp.float32)
    qt_ref[...] = (qt * (HEAD_DIM ** -0.5 * LOG2_E)).astype(bf16)
    vt = jnp.dot(wt_ref[1], hbt, preferred_element_type=jnp.float32)
    vt_ref[slots[-1]] = vt.astype(bf16)
    zt = jnp.dot(wt_ref[2], hbt, preferred_element_type=jnp.float32)
    gate_ref[...] = zt * jax.nn.sigmoid(zt)

    row = lax.broadcasted_iota(jnp.int32, (win, pair), 0)
    lane = lax.broadcasted_iota(jnp.int32, (win, pair), 1)
    k_aux = jnp.where(lane == 0, jnp.where(row < HIST - j * tq, 1.0, 0.0),
                      jnp.where(lane == 1, 1.0, 0.0)).astype(bf16)
    r2 = lax.broadcasted_iota(jnp.int32, (pair, half), 0)
    zeros_q = jnp.zeros((HEAD_DIM, half), bf16)
    ones_v = jnp.ones((BF16_ROWS, win), bf16)
    n_tiles = win // half
    own = HIST // half

    def penalty(shift=None):
        sub = 0.0 if shift is None else jnp.where(r2 == 1, -shift, 0.0)
        return jnp.where(r2 == 0, MASK_PENALTY, sub).astype(bf16)

    def rhs_tile(pr, m, shifts):
        p0 = pr * pair
        cols = slice(m * half, (m + 1) * half)
        q_a = qt_ref[p0:p0 + HEAD_DIM, cols]
        q_b = qt_ref[p0 + HEAD_DIM:p0 + pair, cols]
        pen = [penalty(None if shifts is None else shifts[e][m]) for e in range(2)]
        return jnp.concatenate([jnp.concatenate([q_a, zeros_q, pen[0]], axis=0),
                                jnp.concatenate([zeros_q, q_b, pen[1]], axis=0)], axis=1)

    def lhs_tile(pr, r):
        s = slots[r // (tq // half)]
        r0 = (r % (tq // half)) * half
        return jnp.concatenate([k_ref[s, r0:r0 + half, pr * pair:(pr + 1) * pair],
                                k_aux[r * half:(r + 1) * half, :]], axis=1)

    def score_dot(lhs, rhs):
        if rhs.shape[1] > tq:
            return jnp.dot(lhs, rhs, preferred_element_type=jnp.float32)
        h0 = lhs.shape[0] // 2
        return jnp.concatenate([jnp.dot(lhs[0:h0, :], rhs, preferred_element_type=jnp.float32),
                                jnp.dot(lhs[h0:, :], rhs, preferred_element_type=jnp.float32)], axis=0)

    def visible(r, skip_own=False):
        return [m for m in range(2)
                if 0 <= r - m < span // half and not (skip_own and r == own + m)]

    n_stage = pt_ref.shape[0]
    for buf in range(n_stage):
        for e in range(2):
            pt_ref[buf, e, span:win, 0:half] = jnp.zeros((win - span, half), bf16)
            pt_ref[buf, e, 0:win - span, half:tq] = jnp.zeros((win - span, half), bf16)

    def attend(pr):
        buf = pr % n_stage
        gated_t, denoms = [], []
        for e in range(2):
            rows = slice((2 * pr + e) * HEAD_DIM, (2 * pr + e + 1) * HEAD_DIM)
            vt_win = jnp.concatenate([vt_ref[s, rows, :] for s in slots], axis=1)
            v_ext = jnp.concatenate([vt_win, ones_v], axis=0)
            o_ext = jnp.dot(v_ext, pt_ref[buf, e], preferred_element_type=jnp.float32)
            denom = o_ext[HEAD_DIM:HEAD_DIM + SUBLANES, :]
            o_t = o_ext[0:HEAD_DIM, :] * jnp.concatenate([1.0 / denom] * (HEAD_DIM // SUBLANES), axis=0)
            gated_t.append(o_t * gate_ref[rows, :])
            denoms.append(denom)
        gated_ref[:, pr * pair:(pr + 1) * pair] = jnp.concatenate(gated_t, axis=0).T.astype(bf16)
        return denoms

    def project_out():
        for r0 in range(0, tq, half):
            y = jnp.dot(gated_ref[r0:r0 + half, :], wout_ref[...], preferred_element_type=jnp.float32)
            o_ref[0, r0:r0 + half, :] = x[r0:r0 + half, :] + _rms_norm(y, gpost_ref[layer:layer + 1, :])

    def own_scores(pr):
        buf = pr % n_stage
        maxima = [[None, None], [None, None]]
        for m in range(2):
            st = score_dot(lhs_tile(pr, own + m), rhs_tile(pr, m, None))
            for e in range(2):
                s = st[:, e * half:(e + 1) * half] + bias_ref[2 * pr + e, HIST:HIST + half, :]
                own_ref[buf, e, m] = s
                maxima[e][m] = jnp.max(s, axis=0, keepdims=True)
        return maxima

    def fast_probs(pr, maxima):
        buf = pr % n_stage
        shifts = [[mx.astype(bf16).astype(jnp.float32) for mx in me] for me in maxima]
        rhs = jnp.concatenate([rhs_tile(pr, m, shifts) for m in range(2)], axis=1)
        for r in range(n_tiles):
            ms = visible(r, skip_own=True)
            if not ms:
                continue
            st = score_dot(lhs_tile(pr, r), rhs[:, ms[0] * tq:(ms[-1] + 1) * tq])
            for mi, m in enumerate(ms):
                for e in range(2):
                    kr = slice((r - m) * half, (r - m + 1) * half)
                    c0 = mi * tq + e * half
                    p = jnp.exp2(st[:, c0:c0 + half] + bias_ref[2 * pr + e, kr, :])
                    pt_ref[buf, e, r * half:(r + 1) * half, m * half:(m + 1) * half] = p.astype(bf16)
        for m in range(2):
            for e in range(2):
                p = jnp.exp2(own_ref[buf, e, m] - shifts[e][m])
                pt_ref[buf, e, (own + m) * half:(own + m + 1) * half,
                       m * half:(m + 1) * half] = p.astype(bf16)

    n_pairs = n_heads // 2
    overflow = jnp.zeros((SUBLANES, tq), jnp.float32)
    maxima = {pr: own_scores(pr) for pr in range(min(2, n_pairs))}
    fast_probs(0, maxima.pop(0))
    for pr in range(n_pairs):
        if pr + 2 < n_pairs:
            maxima[pr + 2] = own_scores(pr + 2)
        if pr + 1 < n_pairs:
            fast_probs(pr + 1, maxima.pop(pr + 1))
        for denom in attend(pr):
            overflow = jnp.maximum(overflow, jnp.where(denom < SOFTMAX_SUM_LIMIT, 0.0, 1.0))
    project_out()

    def exact_scores(pr):
        rhs = jnp.concatenate([rhs_tile(pr, m, None) for m in range(2)], axis=1)
        partial = [[None, None], [None, None]]
        for r in range(n_tiles):
            ms = visible(r)
            st = score_dot(lhs_tile(pr, r), rhs[:, ms[0] * tq:(ms[-1] + 1) * tq])
            for mi, m in enumerate(ms):
                for e in range(2):
                    kr = slice((r - m) * half, (r - m + 1) * half)
                    c0 = mi * tq + e * half
                    s = st[:, c0:c0 + half] + bias_ref[2 * pr + e, kr, :]
                    s_ref[e, m, kr, :] = s
                    for i in range(0, half, SUBLANES):
                        tile = s[i:i + SUBLANES, :]
                        prev = partial[e][m]
                        partial[e][m] = tile if prev is None else jnp.maximum(prev, tile)
        return [[jnp.max(p, axis=0, keepdims=True) for p in pe] for pe in partial]

    @pl.when(jnp.max(overflow) > 0.0)
    def _():
        for pr in range(n_pairs):
            exact_maxima = exact_scores(pr)
            for e in range(2):
                for m in range(2):
                    p = jnp.exp2(s_ref[e, m] - exact_maxima[e][m]).astype(bf16)
                    pt_ref[pr % n_stage, e, m * half:m * half + span, m * half:(m + 1) * half] = p
            attend(pr)
        project_out()


def _resident(shape):
    zeros = (0,) * len(shape)
    return pl.BlockSpec(shape, lambda b, j: zeros, pipeline_mode=pl.Buffered(1))


def _row_block(rows, d):
    return pl.BlockSpec((1, rows, d), lambda b, j: (b, j, 0))


def _compiler_params(weight_operands, n_operands):
    return pltpu.CompilerParams(
        dimension_semantics=("arbitrary", "arbitrary"), vmem_limit_bytes=VMEM_LIMIT_BYTES,
        allow_input_fusion=[i in weight_operands for i in range(n_operands)])


def _pool_layer(x, layer, norm_pre, norm_post, w_in, w_group, scale, w_out):
    b, s, d = x.shape
    width = w_out.shape[0]
    tm = POOL_BLOCK_ROWS
    return pl.pallas_call(
        functools.partial(_pool_layer_kernel, layer),
        grid=(b, s // tm),
        in_specs=[_row_block(tm, d), _resident(norm_pre.shape), _resident(norm_post.shape),
                  _resident(w_in.shape), _resident(w_group.shape), _resident(scale.shape),
                  _resident(w_out.shape)],
        out_specs=_row_block(tm, d),
        out_shape=jax.ShapeDtypeStruct(x.shape, x.dtype),
        scratch_shapes=[pltpu.VMEM((d, width), jnp.bfloat16),
                        pltpu.VMEM((POOL_HALO, width), jnp.float32),
                        pltpu.VMEM((tm // POOL_SUB_ROWS, POOL_SUB_ROWS, width), jnp.bfloat16)],
        compiler_params=_compiler_params((3, 4, 6), 7),
        name="pool_layer",
    )(x, norm_pre, norm_post, w_in, w_group, scale, w_out)


def _att_layer(x, layer, norm_pre, norm_post, w_in, rel_bias, w_out):
    b, s, d = x.shape
    width = w_out.shape[0]
    n_heads = width // HEAD_DIM
    tq = ATT_BLOCK_ROWS
    n_slots = HIST // tq + 1
    return pl.pallas_call(
        functools.partial(_att_layer_kernel, layer),
        grid=(b, s // tq),
        in_specs=[_row_block(tq, d), _resident(norm_pre.shape), _resident(norm_post.shape),
                  _resident(w_in.shape), _resident(rel_bias.shape),
                  _resident(w_out.shape)],
        out_specs=_row_block(tq, d),
        out_shape=jax.ShapeDtypeStruct(x.shape, x.dtype),
        scratch_shapes=[pltpu.VMEM((len(ATT_T_BLOCKS), width, d), jnp.bfloat16),
                        pltpu.VMEM((n_slots, tq, width), jnp.bfloat16),
                        pltpu.VMEM((n_slots, width, tq), jnp.bfloat16),
                        pltpu.VMEM((width, tq), jnp.bfloat16),
                        pltpu.VMEM((width, tq), jnp.float32),
                        pltpu.VMEM((tq, width), jnp.bfloat16),
                        pltpu.VMEM((n_heads, ATT_SPAN, LANES), jnp.float32),
                        pltpu.VMEM((ATT_STAGES, 2, 2, LANES, LANES), jnp.float32),
                        pltpu.VMEM((2, 2, ATT_SPAN, LANES), jnp.float32),
                        pltpu.VMEM((ATT_STAGES, 2, HIST + tq, tq), jnp.bfloat16)],
        compiler_params=_compiler_params((3, 5), 6),
        name="att_layer",
    )(x, norm_pre, norm_post, w_in, rel_bias, w_out)


def kernel(x, norm_pre, norm_post, pool_w_in, pool_w_group, pool_scale, pool_w_out,
           att_w_in, att_rel_bias, att_w_out):
    bf16 = jnp.bfloat16
    x = _pool_layer(x, 0, norm_pre, norm_post, pool_w_in[0].astype(bf16),
                    pool_w_group[0].astype(bf16), pool_scale[0:1], pool_w_out[0].astype(bf16))
    x = _att_layer(x, 1, norm_pre, norm_post, att_w_in[0].astype(bf16), att_rel_bias[0:1],
                   att_w_out[0].astype(bf16))
    return x
```

```python
import functools

import jax
import jax.numpy as jnp
from jax import lax
from jax.experimental import pallas as pl
from jax.experimental.pallas import tpu as pltpu

RMS_EPS = 1e-6
LOG2_E = 1.4426950408889634
SOFTMAX_SUM_LIMIT = 2.0 ** 100
POOL_WINDOWS = (2, 4, 8, 16)
POOL_HALO = 16
CHUNK = 64
LEFT_CHUNKS = 8
HEAD_DIM = 64
MAX_REL = 256
HIST = LEFT_CHUNKS * CHUNK

LANES = 128
SUBLANES = 8
BF16_ROWS = 16
V7X_VMEM_BYTES = 64 * 1024 * 1024
MASK_PENALTY = -1e30
POOL_BLOCK_ROWS = 512
POOL_SUB_ROWS = 256
ATT_BLOCK_ROWS = 256
ATT_T_BLOCKS = (0, 2, 3)
ATT_STAGES = 4
ATT_SPAN = HIST + LANES
REL_LANES = HIST + ATT_BLOCK_ROWS
VMEM_LIMIT_BYTES = V7X_VMEM_BYTES * 7 // 8


def _rms_norm(x, g):
    ms = jnp.mean(x * x, axis=-1, keepdims=True)
    return x * lax.rsqrt(ms + RMS_EPS) * g


def _pool_layer_kernel(layer, x_ref, gpre_ref, gpost_ref, win_ref, wg_ref, scale_ref, wout_ref,
                       o_ref, wmix_ref, carry_ref, act_ref):
    b = pl.program_id(0)
    j = pl.program_id(1)
    tm = x_ref.shape[1]
    n_sub, sub, width = act_ref.shape
    n_groups = len(POOL_WINDOWS)
    gw = width // n_groups

    @pl.when((b == 0) & (j == 0))
    def _():
        for gi in range(n_groups):
            cols = slice(gi * gw, (gi + 1) * gw)
            wmix_ref[:, cols] = jnp.dot(win_ref[:, cols], wg_ref[gi],
                                        preferred_element_type=jnp.float32).astype(jnp.bfloat16)

    @pl.when(j == 0)
    def _():
        carry_ref[...] = jnp.zeros_like(carry_ref)

    for sb in range(n_sub):
        r0 = sb * sub
        x = x_ref[0, r0:r0 + sub, :]
        hb = _rms_norm(x, gpre_ref[layer:layer + 1, :]).astype(jnp.bfloat16)

        t = j * tm + r0 + lax.broadcasted_iota(jnp.int32, (sub, 1), 0)

        def project(gi):
            a = jnp.dot(hb, wmix_ref[:, gi * gw:(gi + 1) * gw], preferred_element_type=jnp.float32)
            z = jnp.dot(hb, win_ref[:, width + gi * gw: width + (gi + 1) * gw],
                        preferred_element_type=jnp.float32)
            return a, z

        projected = project(0)
        for gi, w in enumerate(POOL_WINDOWS):
            cols = slice(gi * gw, (gi + 1) * gw)
            a, z = projected
            if gi + 1 < n_groups:
                projected = project(gi + 1)
            ext = jnp.concatenate([carry_ref[:, cols], a], axis=0)
            carry_ref[:, cols] = a[sub - POOL_HALO:, :]
            s = ext
            k = 1
            while k < w:
                s = s + pltpu.roll(s, k, axis=0)
                k *= 2
            cnt = jnp.minimum(t + 1, w).astype(jnp.float32)
            pooled = s[POOL_HALO:, :] / cnt
            m = (pooled - a) * scale_ref[:, cols]
            act_ref[sb, :, cols] = (m * (z * jax.nn.sigmoid(z))).astype(jnp.bfloat16)

        y = jnp.dot(act_ref[sb], wout_ref[...], preferred_element_type=jnp.float32)
        o_ref[0, r0:r0 + sub, :] = x + _rms_norm(y, gpost_ref[layer:layer + 1, :])


def _build_bias_table(rb_ref, bias_ref):
    n_heads, rows, cols = bias_ref.shape
    lanes = REL_LANES
    n_rel = 2 * MAX_REL
    u = lax.broadcasted_iota(jnp.int32, (1, lanes), 1)
    in_table = (u > cols) & (u < n_rel)
    jj = lax.broadcasted_iota(jnp.int32, (rows, cols), 0)
    c = lax.broadcasted_iota(jnp.int32, (rows, cols), 1)
    kc, qc = jj // CHUNK, c // CHUNK
    in_band = (kc >= qc) & (kc <= qc + LEFT_CHUNKS)
    for h in range(n_heads):
        far = rb_ref[0, h:h + 1, n_rel:n_rel + 1]
        row = jnp.concatenate([rb_ref[0, h:h + 1, 0:n_rel], jnp.zeros((1, lanes - n_rel), jnp.float32)], axis=1)
        g = jnp.where(in_table, row, far)
        t = pltpu.roll(jnp.broadcast_to(g, (rows, lanes)), 0, 1, stride=1, stride_axis=0)
        bias_ref[h] = jnp.where(in_band, t[:, 0:cols] * LOG2_E, -jnp.inf)


def _att_layer_kernel(layer, x_ref, gpre_ref, gpost_ref, win_ref, rb_ref, wout_ref,
                      o_ref, wt_ref, k_ref, vt_ref, qt_ref, gate_ref, gated_ref, bias_ref, own_ref, s_ref, pt_ref):
    j = pl.program_id(0)
    n_seq, tq, _ = x_ref.shape
    width = wout_ref.shape[0]
    n_heads = width // HEAD_DIM
    n_pairs = n_heads // 2
    win = HIST + tq
    half = tq // 2
    span = bias_ref.shape[1]
    pair = 2 * HEAD_DIM
    bf16 = jnp.bfloat16

    @pl.when(j == 0)
    def _():
        _build_bias_table(rb_ref, bias_ref)
        for t, blk in enumerate(ATT_T_BLOCKS):
            for c0 in range(0, width, tq):
                w_blk = win_ref[:, blk * width + c0:blk * width + c0 + tq]
                wt_ref[t, c0:c0 + tq, :] = w_blk.astype(jnp.float32).T.astype(bf16)
        k_ref[...] = jnp.zeros(k_ref.shape, bf16)
        vt_ref[...] = jnp.zeros(vt_ref.shape, bf16)

    n_slots = k_ref.shape[1]
    slots = [lax.rem(j + 1 + i, n_slots) for i in range(n_slots)]

    row = lax.broadcasted_iota(jnp.int32, (win, pair), 0)
    lane = lax.broadcasted_iota(jnp.int32, (win, pair), 1)
    k_aux = jnp.where(lane == 0, jnp.where(row < HIST - j * tq, 1.0, 0.0),
                      jnp.where(lane == 1, 1.0, 0.0)).astype(bf16)
    r2 = lax.broadcasted_iota(jnp.int32, (pair, half), 0)
    zeros_q = jnp.zeros((HEAD_DIM, half), bf16)
    ones_v = jnp.ones((BF16_ROWS, win), bf16)
    n_tiles = win // half
    own = HIST // half
    n_stage = pt_ref.shape[0]

    for buf in range(n_stage):
        for e in range(2):
            pt_ref[buf, e, span:win, 0:half] = jnp.zeros((win - span, half), bf16)
            pt_ref[buf, e, 0:win - span, half:tq] = jnp.zeros((win - span, half), bf16)

    def penalty(shift=None):
        sub = 0.0 if shift is None else jnp.where(r2 == 1, -shift, 0.0)
        return jnp.where(r2 == 0, MASK_PENALTY, sub).astype(bf16)

    def score_dot(lhs, rhs):
        if rhs.shape[1] > tq:
            return jnp.dot(lhs, rhs, preferred_element_type=jnp.float32)
        h0 = lhs.shape[0] // 2
        return jnp.concatenate([jnp.dot(lhs[0:h0, :], rhs, preferred_element_type=jnp.float32),
                                jnp.dot(lhs[h0:, :], rhs, preferred_element_type=jnp.float32)], axis=0)

    def visible(r, skip_own=False):
        return [m for m in range(2)
                if 0 <= r - m < span // half and not (skip_own and r == own + m)]

    class Seq:
        def __init__(self, bi):
            self.bi = bi

        def stage(self, pr):
            return (self.bi * n_pairs + pr) % n_stage

        def project_in(self):
            bi = self.bi
            h = _rms_norm(x_ref[bi], gpre_ref[layer:layer + 1, :])
            hb = h.astype(bf16)
            hbt = h.T.astype(bf16)
            k = jnp.dot(hb, win_ref[:, width:2 * width], preferred_element_type=jnp.float32)
            k_ref[bi, slots[-1]] = k.astype(bf16)
            qt = jnp.dot(wt_ref[0], hbt, preferred_element_type=jnp.float32)
            qt_ref[bi] = (qt * (HEAD_DIM ** -0.5 * LOG2_E)).astype(bf16)
            vt = jnp.dot(wt_ref[1], hbt, preferred_element_type=jnp.float32)
            vt_ref[bi, slots[-1]] = vt.astype(bf16)
            zt = jnp.dot(wt_ref[2], hbt, preferred_element_type=jnp.float32)
            gate_ref[bi] = zt * jax.nn.sigmoid(zt)

        def rhs_tile(self, pr, m, shifts):
            p0 = pr * pair
            cols = slice(m * half, (m + 1) * half)
            q_a = qt_ref[self.bi, p0:p0 + HEAD_DIM, cols]
            q_b = qt_ref[self.bi, p0 + HEAD_DIM:p0 + pair, cols]
            pen = [penalty(None if shifts is None else shifts[e][m]) for e in range(2)]
            return jnp.concatenate([jnp.concatenate([q_a, zeros_q, pen[0]], axis=0),
                                    jnp.concatenate([zeros_q, q_b, pen[1]], axis=0)], axis=1)

        def lhs_tile(self, pr, r):
            s = slots[r // (tq // half)]
            r0 = (r % (tq // half)) * half
            return jnp.concatenate([k_ref[self.bi, s, r0:r0 + half, pr * pair:(pr + 1) * pair],
                                    k_aux[r * half:(r + 1) * half, :]], axis=1)

        def attend(self, pr):
            bi, buf = self.bi, self.stage(pr)
            gated_t, denoms = [], []
            for e in range(2):
                rows = slice((2 * pr + e) * HEAD_DIM, (2 * pr + e + 1) * HEAD_DIM)
                vt_win = jnp.concatenate([vt_ref[bi, s, rows, :] for s in slots], axis=1)
                v_ext = jnp.concatenate([vt_win, ones_v], axis=0)
                o_ext = jnp.dot(v_ext, pt_ref[buf, e], preferred_element_type=jnp.float32)
                denom = o_ext[HEAD_DIM:HEAD_DIM + SUBLANES, :]
                o_t = o_ext[0:HEAD_DIM, :] * jnp.concatenate([1.0 / denom] * (HEAD_DIM // SUBLANES), axis=0)
                gated_t.append(o_t * gate_ref[bi, rows, :])
                denoms.append(denom)
            gated_ref[bi, :, pr * pair:(pr + 1) * pair] = jnp.concatenate(gated_t, axis=0).T.astype(bf16)
            return denoms

        def project_out(self):
            for r0 in range(0, tq, half):
                y = jnp.dot(gated_ref[self.bi, r0:r0 + half, :], wout_ref[...],
                            preferred_element_type=jnp.float32)
                o_ref[self.bi, r0:r0 + half, :] = (x_ref[self.bi, r0:r0 + half, :]
                                                   + _rms_norm(y, gpost_ref[layer:layer + 1, :]))

        def own_scores(self, pr):
            buf = self.stage(pr)
            maxima = [[None, None], [None, None]]
            for m in range(2):
                st = score_dot(self.lhs_tile(pr, own + m), self.rhs_tile(pr, m, None))
                for e in range(2):
                    s = st[:, e * half:(e + 1) * half] + bias_ref[2 * pr + e, HIST:HIST + half, :]
                    own_ref[buf, e, m] = s
                    maxima[e][m] = jnp.max(s, axis=0, keepdims=True)
            return maxima

        def fast_probs(self, pr, maxima):
            buf = self.stage(pr)
            shifts = [[mx.astype(bf16).astype(jnp.float32) for mx in me] for me in maxima]
            rhs = jnp.concatenate([self.rhs_tile(pr, m, shifts) for m in range(2)], axis=1)
            for r in range(n_tiles):
                ms = visible(r, skip_own=True)
                if not ms:
                    continue
                st = score_dot(self.lhs_tile(pr, r), rhs[:, ms[0] * tq:(ms[-1] + 1) * tq])
                for mi, m in enumerate(ms):
                    for e in range(2):
                        kr = slice((r - m) * half, (r - m + 1) * half)
                        c0 = mi * tq + e * half
                        p = jnp.exp2(st[:, c0:c0 + half] + bias_ref[2 * pr + e, kr, :])
                        pt_ref[buf, e, r * half:(r + 1) * half, m * half:(m + 1) * half] = p.astype(bf16)
            for m in range(2):
                for e in range(2):
                    p = jnp.exp2(own_ref[buf, e, m] - shifts[e][m])
                    pt_ref[buf, e, (own + m) * half:(own + m + 1) * half,
                           m * half:(m + 1) * half] = p.astype(bf16)

        def heads(self):
            overflow = jnp.zeros((SUBLANES, tq), jnp.float32)
            maxima = {pr: self.own_scores(pr) for pr in range(min(2, n_pairs))}
            self.fast_probs(0, maxima.pop(0))
            for pr in range(n_pairs):
                if pr + 2 < n_pairs:
                    maxima[pr + 2] = self.own_scores(pr + 2)
                if pr + 1 < n_pairs:
                    self.fast_probs(pr + 1, maxima.pop(pr + 1))
                for denom in self.attend(pr):
                    overflow = jnp.maximum(overflow, jnp.where(denom < SOFTMAX_SUM_LIMIT, 0.0, 1.0))
            return overflow

        def exact_scores(self, pr):
            rhs = jnp.concatenate([self.rhs_tile(pr, m, None) for m in range(2)], axis=1)
            partial = [[None, None], [None, None]]
            for r in range(n_tiles):
                ms = visible(r)
                st = score_dot(self.lhs_tile(pr, r), rhs[:, ms[0] * tq:(ms[-1] + 1) * tq])
                for mi, m in enumerate(ms):
                    for e in range(2):
                        kr = slice((r - m) * half, (r - m + 1) * half)
                        c0 = mi * tq + e * half
                        s = st[:, c0:c0 + half] + bias_ref[2 * pr + e, kr, :]
                        s_ref[e, m, kr, :] = s
                        for i in range(0, half, SUBLANES):
                            tile = s[i:i + SUBLANES, :]
                            prev = partial[e][m]
                            partial[e][m] = tile if prev is None else jnp.maximum(prev, tile)
            return [[jnp.max(p, axis=0, keepdims=True) for p in pe] for pe in partial]

        def exact_heads(self):
            for pr in range(n_pairs):
                exact_maxima = self.exact_scores(pr)
                for e in range(2):
                    for m in range(2):
                        p = jnp.exp2(s_ref[e, m] - exact_maxima[e][m]).astype(bf16)
                        pt_ref[self.stage(pr), e, m * half:m * half + span, m * half:(m + 1) * half] = p
                self.attend(pr)
            self.project_out()

    seqs = [Seq(bi) for bi in range(n_seq)]
    flags = []
    for sq in seqs:
        sq.project_in()
        flags.append(jnp.max(sq.heads()) > 0.0)
        sq.project_out()
    for sq, flag in zip(seqs, flags):
        pl.when(flag)(sq.exact_heads)


def _resident(shape):
    zeros = (0,) * len(shape)
    return pl.BlockSpec(shape, lambda *_: zeros, pipeline_mode=pl.Buffered(1))


def _row_block(rows, d):
    return pl.BlockSpec((1, rows, d), lambda b, j: (b, j, 0))


_COMPILER_PARAMS = pltpu.CompilerParams(
    dimension_semantics=("arbitrary", "arbitrary"), vmem_limit_bytes=VMEM_LIMIT_BYTES)


def _pool_layer(x, layer, norm_pre, norm_post, w_in, w_group, scale, w_out):
    b, s, d = x.shape
    width = w_out.shape[0]
    tm = POOL_BLOCK_ROWS
    return pl.pallas_call(
        functools.partial(_pool_layer_kernel, layer),
        grid=(b, s // tm),
        in_specs=[_row_block(tm, d), _resident(norm_pre.shape), _resident(norm_post.shape),
                  _resident(w_in.shape), _resident(w_group.shape), _resident(scale.shape),
                  _resident(w_out.shape)],
        out_specs=_row_block(tm, d),
        out_shape=jax.ShapeDtypeStruct(x.shape, x.dtype),
        scratch_shapes=[pltpu.VMEM((d, width), jnp.bfloat16),
                        pltpu.VMEM((POOL_HALO, width), jnp.float32),
                        pltpu.VMEM((tm // POOL_SUB_ROWS, POOL_SUB_ROWS, width), jnp.bfloat16)],
        compiler_params=_COMPILER_PARAMS,
        name="pool_layer",
    )(x, norm_pre, norm_post, w_in, w_group, scale, w_out)


def _att_layer(x, layer, norm_pre, norm_post, w_in, rel_bias, w_out):
    b, s, d = x.shape
    width = w_out.shape[0]
    n_heads = width // HEAD_DIM
    tq = ATT_BLOCK_ROWS
    n_slots = HIST // tq + 1
    all_rows = pl.BlockSpec((b, tq, d), lambda j: (0, j, 0))
    return pl.pallas_call(
        functools.partial(_att_layer_kernel, layer),
        grid=(s // tq,),
        in_specs=[all_rows, _resident(norm_pre.shape), _resident(norm_post.shape),
                  _resident(w_in.shape), _resident(rel_bias.shape),
                  _resident(w_out.shape)],
        out_specs=all_rows,
        out_shape=jax.ShapeDtypeStruct(x.shape, x.dtype),
        scratch_shapes=[pltpu.VMEM((len(ATT_T_BLOCKS), width, d), jnp.bfloat16),
                        pltpu.VMEM((b, n_slots, tq, width), jnp.bfloat16),
                        pltpu.VMEM((b, n_slots, width, tq), jnp.bfloat16),
                        pltpu.VMEM((b, width, tq), jnp.bfloat16),
                        pltpu.VMEM((b, width, tq), jnp.float32),
                        pltpu.VMEM((b, tq, width), jnp.bfloat16),
                        pltpu.VMEM((n_heads, ATT_SPAN, LANES), jnp.float32),
                        pltpu.VMEM((ATT_STAGES, 2, 2, LANES, LANES), jnp.float32),
                        pltpu.VMEM((2, 2, ATT_SPAN, LANES), jnp.float32),
                        pltpu.VMEM((ATT_STAGES, 2, HIST + tq, tq), jnp.bfloat16)],
        compiler_params=pltpu.CompilerParams(dimension_semantics=("arbitrary",),
                                             vmem_limit_bytes=VMEM_LIMIT_BYTES),
        name="att_layer",
    )(x, norm_pre, norm_post, w_in, rel_bias, w_out)


def kernel(x, norm_pre, norm_post, pool_w_in, pool_w_group, pool_scale, pool_w_out,
           att_w_in, att_rel_bias, att_w_out):
    bf16 = jnp.bfloat16
    x = _pool_layer(x, 0, norm_pre, norm_post, pool_w_in[0].astype(bf16),
                    pool_w_group[0].astype(bf16), pool_scale[0:1], pool_w_out[0].astype(bf16))
    x = _att_layer(x, 1, norm_pre, norm_post, att_w_in[0].astype(bf16), att_rel_bias[0:1],
                   att_w_out[0].astype(bf16))
    return x
```

```python
import functools

import jax
import jax.numpy as jnp
from jax import lax
from jax.experimental import pallas as pl
from jax.experimental.pallas import tpu as pltpu

RMS_EPS = 1e-6
LOG2_E = 1.4426950408889634
SOFTMAX_SUM_LIMIT = 2.0 ** 100
POOL_WINDOWS = (2, 4, 8, 16)
POOL_HALO = 16
CHUNK = 64
LEFT_CHUNKS = 8
HEAD_DIM = 64
MAX_REL = 256
HIST = LEFT_CHUNKS * CHUNK

LANES = 128
SUBLANES = 8
BF16_ROWS = 16
V7X_VMEM_BYTES = 64 * 1024 * 1024
MASK_PENALTY = -1e30
POOL_BLOCK_ROWS = 512
POOL_SUB_ROWS = 512
ATT_BLOCK_ROWS = 256
ATT_T_BLOCKS = (0, 2, 3)
ATT_STAGES = 4
ATT_SPAN = HIST + LANES
REL_LANES = HIST + ATT_BLOCK_ROWS
VMEM_LIMIT_BYTES = V7X_VMEM_BYTES * 7 // 8


def _rms_norm(x, g):
    ms = jnp.mean(x * x, axis=-1, keepdims=True)
    return x * lax.rsqrt(ms + RMS_EPS) * g


def _pool_layer_kernel(layer, x_ref, gpre_ref, gpost_ref, win_ref, wg_ref, scale_ref, wout_ref,
                       o_ref, wmix_ref, carry_ref, act_ref):
    b = pl.program_id(0)
    j = pl.program_id(1)
    tm = x_ref.shape[1]
    n_sub, sub, width = act_ref.shape
    n_groups = len(POOL_WINDOWS)
    gw = width // n_groups

    @pl.when((b == 0) & (j == 0))
    def _():
        for gi in range(n_groups):
            cols = slice(gi * gw, (gi + 1) * gw)
            wmix_ref[:, cols] = jnp.dot(win_ref[:, cols], wg_ref[gi],
                                        preferred_element_type=jnp.float32).astype(jnp.bfloat16)

    @pl.when(j == 0)
    def _():
        carry_ref[...] = jnp.zeros_like(carry_ref)

    for sb in range(n_sub):
        r0 = sb * sub
        x = x_ref[0, r0:r0 + sub, :]
        hb = _rms_norm(x, gpre_ref[layer:layer + 1, :]).astype(jnp.bfloat16)

        t = j * tm + r0 + lax.broadcasted_iota(jnp.int32, (sub, 1), 0)

        def project(gi):
            a = jnp.dot(hb, wmix_ref[:, gi * gw:(gi + 1) * gw], preferred_element_type=jnp.float32)
            z = jnp.dot(hb, win_ref[:, width + gi * gw: width + (gi + 1) * gw],
                        preferred_element_type=jnp.float32)
            return a, z

        projected = project(0)
        for gi, w in enumerate(POOL_WINDOWS):
            cols = slice(gi * gw, (gi + 1) * gw)
            a, z = projected
            if gi + 1 < n_groups:
                projected = project(gi + 1)
            ext = jnp.concatenate([carry_ref[:, cols], a], axis=0)
            carry_ref[:, cols] = a[sub - POOL_HALO:, :]
            s = ext
            k = 1
            while k < w:
                s = s + pltpu.roll(s, k, axis=0)
                k *= 2
            cnt = jnp.minimum(t + 1, w).astype(jnp.float32)
            pooled = s[POOL_HALO:, :] / cnt
            m = (pooled - a) * scale_ref[:, cols]
            act_ref[sb, :, cols] = (m * (z * jax.nn.sigmoid(z))).astype(jnp.bfloat16)

        y = jnp.dot(act_ref[sb], wout_ref[...], preferred_element_type=jnp.float32)
        o_ref[0, r0:r0 + sub, :] = x + _rms_norm(y, gpost_ref[layer:layer + 1, :])


def _build_bias_table(rb_ref, bias_ref):
    n_heads, rows, cols = bias_ref.shape
    lanes = REL_LANES
    n_rel = 2 * MAX_REL
    u = lax.broadcasted_iota(jnp.int32, (1, lanes), 1)
    in_table = (u > cols) & (u < n_rel)
    jj = lax.broadcasted_iota(jnp.int32, (rows, cols), 0)
    c = lax.broadcasted_iota(jnp.int32, (rows, cols), 1)
    kc, qc = jj // CHUNK, c // CHUNK
    in_band = (kc >= qc) & (kc <= qc + LEFT_CHUNKS)
    for h in range(n_heads):
        far = rb_ref[0, h:h + 1, n_rel:n_rel + 1]
        row = jnp.concatenate([rb_ref[0, h:h + 1, 0:n_rel], jnp.zeros((1, lanes - n_rel), jnp.float32)], axis=1)
        g = jnp.where(in_table, row, far)
        t = pltpu.roll(jnp.broadcast_to(g, (rows, lanes)), 0, 1, stride=1, stride_axis=0)
        bias_ref[h] = jnp.where(in_band, t[:, 0:cols] * LOG2_E, -jnp.inf)


def _att_layer_kernel(layer, x_ref, gpre_ref, gpost_ref, win_ref, rb_ref, wout_ref,
                      o_ref, wt_ref, k_ref, vt_ref, qt_ref, gate_ref, gated_ref, bias_ref, own_ref, s_ref, pt_ref):
    b = pl.program_id(0)
    j = pl.program_id(1)
    tq = x_ref.shape[1]
    width = wout_ref.shape[0]
    n_heads = width // HEAD_DIM
    win = HIST + tq
    half = tq // 2
    span = bias_ref.shape[1]
    pair = 2 * HEAD_DIM
    bf16 = jnp.bfloat16

    @pl.when((b == 0) & (j == 0))
    def _():
        _build_bias_table(rb_ref, bias_ref)
        for t, blk in enumerate(ATT_T_BLOCKS):
            for c0 in range(0, width, tq):
                w_blk = win_ref[:, blk * width + c0:blk * width + c0 + tq]
                wt_ref[t, c0:c0 + tq, :] = w_blk.astype(jnp.float32).T.astype(bf16)

    n_slots = k_ref.shape[0]
    slots = [lax.rem(j + 1 + i, n_slots) for i in range(n_slots)]

    @pl.when(j == 0)
    def _():
        k_ref[...] = jnp.zeros(k_ref.shape, bf16)
        vt_ref[...] = jnp.zeros(vt_ref.shape, bf16)

    x = x_ref[0]
    h = _rms_norm(x, gpre_ref[layer:layer + 1, :])
    hb = h.astype(bf16)
    hbt = h.T.astype(bf16)
    k = jnp.dot(hb, win_ref[:, width:2 * width], preferred_element_type=jnp.float32)
    k_ref[slots[-1]] = k.astype(bf16)
    qt = jnp.dot(wt_ref[0], hbt, preferred_element_type=jnp.float32)
    qt_ref[...] = (qt * (HEAD_DIM ** -0.5 * LOG2_E)).astype(bf16)
    vt = jnp.dot(wt_ref[1], hbt, preferred_element_type=jnp.float32)
    vt_ref[slots[-1]] = vt.astype(bf16)
    zt = jnp.dot(wt_ref[2], hbt, preferred_element_type=jnp.float32)
    gate_ref[...] = zt * jax.nn.sigmoid(zt)

    row = lax.broadcasted_iota(jnp.int32, (win, pair), 0)
    lane = lax.broadcasted_iota(jnp.int32, (win, pair), 1)
    k_aux = jnp.where(lane == 0, jnp.where(row < HIST - j * tq, 1.0, 0.0),
                      jnp.where(lane == 1, 1.0, 0.0)).astype(bf16)
    r2 = lax.broadcasted_iota(jnp.int32, (pair, half), 0)
    zeros_q = jnp.zeros((HEAD_DIM, half), bf16)
    ones_v = jnp.ones((BF16_ROWS, win), bf16)
    n_tiles = win // half
    own = HIST // half

    def penalty(shift=None):
        sub = 0.0 if shift is None else jnp.where(r2 == 1, -shift, 0.0)
        return jnp.where(r2 == 0, MASK_PENALTY, sub).astype(bf16)

    def rhs_tile(pr, m, shifts):
        p0 = pr * pair
        cols = slice(m * half, (m + 1) * half)
        q_a = qt_ref[p0:p0 + HEAD_DIM, cols]
        q_b = qt_ref[p0 + HEAD_DIM:p0 + pair, cols]
        pen = [penalty(None if shifts is None else shifts[e][m]) for e in range(2)]
        return jnp.concatenate([jnp.concatenate([q_a, zeros_q, pen[0]], axis=0),
                                jnp.concatenate([zeros_q, q_b, pen[1]], axis=0)], axis=1)

    def lhs_tile(pr, r):
        s = slots[r // (tq // half)]
        r0 = (r % (tq // half)) * half
        return jnp.concatenate([k_ref[s, r0:r0 + half, pr * pair:(pr + 1) * pair],
                                k_aux[r * half:(r + 1) * half, :]], axis=1)

    def score_dot(lhs, rhs):
        if rhs.shape[1] > tq:
            return jnp.dot(lhs, rhs, preferred_element_type=jnp.float32)
        h0 = lhs.shape[0] // 2
        return jnp.concatenate([jnp.dot(lhs[0:h0, :], rhs, preferred_element_type=jnp.float32),
                                jnp.dot(lhs[h0:, :], rhs, preferred_element_type=jnp.float32)], axis=0)

    def visible(r, skip_own=False):
        return [m for m in range(2)
                if 0 <= r - m < span // half and not (skip_own and r == own + m)]

    n_stage = pt_ref.shape[0]
    for buf in range(n_stage):
        for e in range(2):
            pt_ref[buf, e, span:win, 0:half] = jnp.zeros((win - span, half), bf16)
            pt_ref[buf, e, 0:win - span, half:tq] = jnp.zeros((win - span, half), bf16)

    def attend(pr):
        buf = pr % n_stage
        gated_t, denoms = [], []
        for e in range(2):
            rows = slice((2 * pr + e) * HEAD_DIM, (2 * pr + e + 1) * HEAD_DIM)
            vt_win = jnp.concatenate([vt_ref[s, rows, :] for s in slots], axis=1)
            v_ext = jnp.concatenate([vt_win, ones_v], axis=0)
            o_ext = jnp.dot(v_ext, pt_ref[buf, e], preferred_element_type=jnp.float32)
            denom = o_ext[HEAD_DIM:HEAD_DIM + SUBLANES, :]
            o_t = o_ext[0:HEAD_DIM, :] * jnp.concatenate([1.0 / denom] * (HEAD_DIM // SUBLANES), axis=0)
            gated_t.append(o_t * gate_ref[rows, :])
            denoms.append(denom)
        gated_ref[:, pr * pair:(pr + 1) * pair] = jnp.concatenate(gated_t, axis=0).T.astype(bf16)
        return denoms

    def project_out():
        for r0 in range(0, tq, half):
            y = jnp.dot(gated_ref[r0:r0 + half, :], wout_ref[...], preferred_element_type=jnp.float32)
            o_ref[0, r0:r0 + half, :] = x[r0:r0 + half, :] + _rms_norm(y, gpost_ref[layer:layer + 1, :])

    def own_scores(pr):
        buf = pr % n_stage
        maxima = [[None, None], [None, None]]
        for m in range(2):
            st = score_dot(lhs_tile(pr, own + m), rhs_tile(pr, m, None))
            for e in range(2):
                s = st[:, e * half:(e + 1) * half] + bias_ref[2 * pr + e, HIST:HIST + half, :]
                own_ref[buf, e, m] = s
                maxima[e][m] = jnp.max(s, axis=0, keepdims=True)
        return maxima

    def fast_probs(pr, maxima):
        buf = pr % n_stage
        shifts = [[mx.astype(bf16).astype(jnp.float32) for mx in me] for me in maxima]
        rhs = jnp.concatenate([rhs_tile(pr, m, shifts) for m in range(2)], axis=1)
        for r in range(n_tiles):
            ms = visible(r, skip_own=True)
            if not ms:
                continue
            st = score_dot(lhs_tile(pr, r), rhs[:, ms[0] * tq:(ms[-1] + 1) * tq])
            for mi, m in enumerate(ms):
                for e in range(2):
                    kr = slice((r - m) * half, (r - m + 1) * half)
                    c0 = mi * tq + e * half
                    p = jnp.exp2(st[:, c0:c0 + half] + bias_ref[2 * pr + e, kr, :])
                    pt_ref[buf, e, r * half:(r + 1) * half, m * half:(m + 1) * half] = p.astype(bf16)
        for m in range(2):
            for e in range(2):
                p = jnp.exp2(own_ref[buf, e, m] - shifts[e][m])
                pt_ref[buf, e, (own + m) * half:(own + m + 1) * half,
                       m * half:(m + 1) * half] = p.astype(bf16)

    n_pairs = n_heads // 2
    overflow = jnp.zeros((SUBLANES, tq), jnp.float32)
    maxima = {pr: own_scores(pr) for pr in range(min(2, n_pairs))}
    fast_probs(0, maxima.pop(0))
    for pr in range(n_pairs):
        if pr + 2 < n_pairs:
            maxima[pr + 2] = own_scores(pr + 2)
        if pr + 1 < n_pairs:
            fast_probs(pr + 1, maxima.pop(pr + 1))
        for denom in attend(pr):
            overflow = jnp.maximum(overflow, jnp.where(denom < SOFTMAX_SUM_LIMIT, 0.0, 1.0))
    project_out()

    def exact_scores(pr):
        rhs = jnp.concatenate([rhs_tile(pr, m, None) for m in range(2)], axis=1)
        partial = [[None, None], [None, None]]
        for r in range(n_tiles):
            ms = visible(r)
            st = score_dot(lhs_tile(pr, r), rhs[:, ms[0] * tq:(ms[-1] + 1) * tq])
            for mi, m in enumerate(ms):
                for e in range(2):
                    kr = slice((r - m) * half, (r - m + 1) * half)
                    c0 = mi * tq + e * half
                    s = st[:, c0:c0 + half] + bias_ref[2 * pr + e, kr, :]
                    s_ref[e, m, kr, :] = s
                    for i in range(0, half, SUBLANES):
                        tile = s[i:i + SUBLANES, :]
                        prev = partial[e][m]
                        partial[e][m] = tile if prev is None else jnp.maximum(prev, tile)
        return [[jnp.max(p, axis=0, keepdims=True) for p in pe] for pe in partial]

    @pl.when(jnp.max(overflow) > 0.0)
    def _():
        for pr in range(n_pairs):
            exact_maxima = exact_scores(pr)
            for e in range(2):
                for m in range(2):
                    p = jnp.exp2(s_ref[e, m] - exact_maxima[e][m]).astype(bf16)
                    pt_ref[pr % n_stage, e, m * half:m * half + span, m * half:(m + 1) * half] = p
            attend(pr)
        project_out()


def _resident(shape):
    zeros = (0,) * len(shape)
    return pl.BlockSpec(shape, lambda b, j: zeros, pipeline_mode=pl.Buffered(1))


def _row_block(rows, d):
    return pl.BlockSpec((1, rows, d), lambda b, j: (b, j, 0))


_COMPILER_PARAMS = pltpu.CompilerParams(
    dimension_semantics=("arbitrary", "arbitrary"), vmem_limit_bytes=VMEM_LIMIT_BYTES)


def _pool_layer(x, layer, norm_pre, norm_post, w_in, w_group, scale, w_out):
    b, s, d = x.shape
    width = w_out.shape[0]
    tm = POOL_BLOCK_ROWS
    return pl.pallas_call(
        functools.partial(_pool_layer_kernel, layer),
        grid=(b, s // tm),
        in_specs=[_row_block(tm, d), _resident(norm_pre.shape), _resident(norm_post.shape),
                  _resident(w_in.shape), _resident(w_group.shape), _resident(scale.shape),
                  _resident(w_out.shape)],
        out_specs=_row_block(tm, d),
        out_shape=jax.ShapeDtypeStruct(x.shape, x.dtype),
        scratch_shapes=[pltpu.VMEM((d, width), jnp.bfloat16),
                        pltpu.VMEM((POOL_HALO, width), jnp.float32),
                        pltpu.VMEM((tm // POOL_SUB_ROWS, POOL_SUB_ROWS, width), jnp.bfloat16)],
        compiler_params=_COMPILER_PARAMS,
        name="pool_layer",
    )(x, norm_pre, norm_post, w_in, w_group, scale, w_out)


def _att_layer(x, layer, norm_pre, norm_post, w_in, rel_bias, w_out):
    b, s, d = x.shape
    width = w_out.shape[0]
    n_heads = width // HEAD_DIM
    tq = ATT_BLOCK_ROWS
    n_slots = HIST // tq + 1
    return pl.pallas_call(
        functools.partial(_att_layer_kernel, layer),
        grid=(b, s // tq),
        in_specs=[_row_block(tq, d), _resident(norm_pre.shape), _resident(norm_post.shape),
                  _resident(w_in.shape), _resident(rel_bias.shape),
                  _resident(w_out.shape)],
        out_specs=_row_block(tq, d),
        out_shape=jax.ShapeDtypeStruct(x.shape, x.dtype),
        scratch_shapes=[pltpu.VMEM((len(ATT_T_BLOCKS), width, d), jnp.bfloat16),
                        pltpu.VMEM((n_slots, tq, width), jnp.bfloat16),
                        pltpu.VMEM((n_slots, width, tq), jnp.bfloat16),
                        pltpu.VMEM((width, tq), jnp.bfloat16),
                        pltpu.VMEM((width, tq), jnp.float32),
                        pltpu.VMEM((tq, width), jnp.bfloat16),
                        pltpu.VMEM((n_heads, ATT_SPAN, LANES), jnp.float32),
                        pltpu.VMEM((ATT_STAGES, 2, 2, LANES, LANES), jnp.float32),
                        pltpu.VMEM((2, 2, ATT_SPAN, LANES), jnp.float32),
                        pltpu.VMEM((ATT_STAGES, 2, HIST + tq, tq), jnp.bfloat16)],
        compiler_params=_COMPILER_PARAMS,
        name="att_layer",
    )(x, norm_pre, norm_post, w_in, rel_bias, w_out)


def kernel(x, norm_pre, norm_post, pool_w_in, pool_w_group, pool_scale, pool_w_out,
           att_w_in, att_rel_bias, att_w_out):
    bf16 = jnp.bfloat16
    x = _pool_layer(x, 0, norm_pre, norm_post, pool_w_in[0].astype(bf16),
                    pool_w_group[0].astype(bf16), pool_scale[0:1], pool_w_out[0].astype(bf16))
    x = _att_layer(x, 1, norm_pre, norm_post, att_w_in[0].astype(bf16), att_rel_bias[0:1],
                   att_w_out[0].astype(bf16))
    return x
```

```python
import functools

import jax
import jax.numpy as jnp
from jax import lax
from jax.experimental import pallas as pl
from jax.experimental.pallas import tpu as pltpu

RMS_EPS = 1e-6
LOG2_E = 1.4426950408889634
SOFTMAX_SUM_LIMIT = 2.0 ** 100
POOL_WINDOWS = (2, 4, 8, 16)
POOL_HALO = 16
CHUNK = 64
LEFT_CHUNKS = 8
HEAD_DIM = 64
MAX_REL = 256
HIST = LEFT_CHUNKS * CHUNK

LANES = 128
SUBLANES = 8
BF16_ROWS = 16
V7X_VMEM_BYTES = 64 * 1024 * 1024
MASK_PENALTY = -1e30
POOL_BLOCK_ROWS = 512
POOL_SUB_ROWS = 256
ATT_BLOCK_ROWS = 256
ATT_T_BLOCKS = (0, 2, 3)
ATT_STAGES = 4
ATT_SPAN = HIST + LANES
REL_LANES = HIST + ATT_BLOCK_ROWS
VMEM_LIMIT_BYTES = V7X_VMEM_BYTES * 7 // 8


def _rms_norm(x, g):
    ms = jnp.mean(x * x, axis=-1, keepdims=True)
    return x * lax.rsqrt(ms + RMS_EPS) * g


def _pool_layer_kernel(layer, x_ref, gpre_ref, gpost_ref, win_ref, wg_ref, scale_ref, wout_ref,
                       o_ref, wmix_ref, carry_ref, act_ref):
    b = pl.program_id(0)
    j = pl.program_id(1)
    tm = x_ref.shape[1]
    n_sub, sub, width = act_ref.shape
    n_groups = len(POOL_WINDOWS)
    gw = width // n_groups

    @pl.when((b == 0) & (j == 0))
    def _():
        for gi in range(n_groups):
            cols = slice(gi * gw, (gi + 1) * gw)
            wmix_ref[:, cols] = jnp.dot(win_ref[:, cols], wg_ref[gi],
                                        preferred_element_type=jnp.float32).astype(jnp.bfloat16)

    @pl.when(j == 0)
    def _():
        carry_ref[...] = jnp.zeros_like(carry_ref)

    for sb in range(n_sub):
        r0 = sb * sub
        x = x_ref[0, r0:r0 + sub, :]
        hb = _rms_norm(x, gpre_ref[layer:layer + 1, :]).astype(jnp.bfloat16)

        t = j * tm + r0 + lax.broadcasted_iota(jnp.int32, (sub, 1), 0)

        def project(gi):
            a = jnp.dot(hb, wmix_ref[:, gi * gw:(gi + 1) * gw], preferred_element_type=jnp.float32)
            z = jnp.dot(hb, win_ref[:, width + gi * gw: width + (gi + 1) * gw],
                        preferred_element_type=jnp.float32)
            return a, z

        projected = project(0)
        for gi, w in enumerate(POOL_WINDOWS):
            cols = slice(gi * gw, (gi + 1) * gw)
            a, z = projected
            if gi + 1 < n_groups:
                projected = project(gi + 1)
            ext = jnp.concatenate([carry_ref[:, cols], a], axis=0)
            carry_ref[:, cols] = a[sub - POOL_HALO:, :]
            s = ext
            k = 1
            while k < w:
                s = s + pltpu.roll(s, k, axis=0)
                k *= 2
            cnt = jnp.minimum(t + 1, w).astype(jnp.float32)
            pooled = s[POOL_HALO:, :] / cnt
            m = (pooled - a) * scale_ref[:, cols]
            act_ref[sb, :, cols] = (m * (z * jax.nn.sigmoid(z))).astype(jnp.bfloat16)

        y = jnp.dot(act_ref[sb], wout_ref[...], preferred_element_type=jnp.float32)
        o_ref[0, r0:r0 + sub, :] = x + _rms_norm(y, gpost_ref[layer:layer + 1, :])


def _build_bias_table(rb_ref, bias_ref):
    n_heads, rows, cols = bias_ref.shape
    lanes = REL_LANES
    n_rel = 2 * MAX_REL
    u = lax.broadcasted_iota(jnp.int32, (1, lanes), 1)
    in_table = (u > cols) & (u < n_rel)
    jj = lax.broadcasted_iota(jnp.int32, (rows, cols), 0)
    c = lax.broadcasted_iota(jnp.int32, (rows, cols), 1)
    kc, qc = jj // CHUNK, c // CHUNK
    in_band = (kc >= qc) & (kc <= qc + LEFT_CHUNKS)
    for h in range(n_heads):
        far = rb_ref[0, h:h + 1, n_rel:n_rel + 1]
        row = jnp.concatenate([rb_ref[0, h:h + 1, 0:n_rel], jnp.zeros((1, lanes - n_rel), jnp.float32)], axis=1)
        g = jnp.where(in_table, row, far)
        t = pltpu.roll(jnp.broadcast_to(g, (rows, lanes)), 0, 1, stride=1, stride_axis=0)
        bias_ref[h] = jnp.where(in_band, t[:, 0:cols] * LOG2_E, -jnp.inf)


def _att_layer_kernel(layer, x_ref, gpre_ref, gpost_ref, win_ref, rb_ref, wout_ref,
                      o_ref, wt_ref, k_ref, vt_ref, qt_ref, gate_ref, gated_ref, bias_ref, own_ref, s_ref, pt_ref):
    b = pl.program_id(0)
    j = pl.program_id(1)
    tq = x_ref.shape[1]
    width = wout_ref.shape[0]
    n_heads = width // HEAD_DIM
    win = HIST + tq
    half = tq // 2
    span = bias_ref.shape[1]
    pair = 2 * HEAD_DIM
    bf16 = jnp.bfloat16

    @pl.when((b == 0) & (j == 0))
    def _():
        _build_bias_table(rb_ref, bias_ref)
        for t, blk in enumerate(ATT_T_BLOCKS):
            for c0 in range(0, width, tq):
                w_blk = win_ref[:, blk * width + c0:blk * width + c0 + tq]
                wt_ref[t, c0:c0 + tq, :] = w_blk.astype(jnp.float32).T.astype(bf16)

    n_slots = k_ref.shape[0]
    slots = [lax.rem(j + 1 + i, n_slots) for i in range(n_slots)]

    @pl.when(j == 0)
    def _():
        k_ref[...] = jnp.zeros(k_ref.shape, bf16)
        vt_ref[...] = jnp.zeros(vt_ref.shape, bf16)

    x = x_ref[0]
    h = _rms_norm(x, gpre_ref[layer:layer + 1, :])
    hb = h.astype(bf16)
    hbt = h.T.astype(bf16)
    k = jnp.dot(hb, win_ref[:, width:2 * width], preferred_element_type=jnp.float32)
    k_ref[slots[-1]] = k.astype(bf16)
    qt = jnp.dot(wt_ref[0], hbt, preferred_element_type=jnp.float32)
    qt_ref[...] = (qt * (HEAD_DIM ** -0.5 * LOG2_E)).astype(bf16)
    vt = jnp.dot(wt_ref[1], hbt, preferred_element_type=jnp.float32)
    vt_ref[slots[-1]] = vt.astype(bf16)
    zt = jnp.dot(wt_ref[2], hbt, preferred_element_type=jnp.float32)
    gate_ref[...] = zt * jax.nn.sigmoid(zt)

    row = lax.broadcasted_iota(jnp.int32, (win, pair), 0)
    lane = lax.broadcasted_iota(jnp.int32, (win, pair), 1)
    k_aux = jnp.where(lane == 0, jnp.where(row < HIST - j * tq, 1.0, 0.0),
                      jnp.where(lane == 1, 1.0, 0.0)).astype(bf16)
    r2 = lax.broadcasted_iota(jnp.int32, (pair, half), 0)
    zeros_q = jnp.zeros((HEAD_DIM, half), bf16)
    ones_v = jnp.ones((BF16_ROWS, win), bf16)
    n_tiles = win // half
    own = HIST // half

    def penalty(shift=None):
        sub = 0.0 if shift is None else jnp.where(r2 == 1, -shift, 0.0)
        return jnp.where(r2 == 0, MASK_PENALTY, sub).astype(bf16)

    def rhs_tile(pr, m, shifts):
        p0 = pr * pair
        cols = slice(m * half, (m + 1) * half)
        q_a = qt_ref[p0:p0 + HEAD_DIM, cols]
        q_b = qt_ref[p0 + HEAD_DIM:p0 + pair, cols]
        pen = [penalty(None if shifts is None else shifts[e][m]) for e in range(2)]
        return jnp.concatenate([jnp.concatenate([q_a, zeros_q, pen[0]], axis=0),
                                jnp.concatenate([zeros_q, q_b, pen[1]], axis=0)], axis=1)

    def lhs_tile(pr, r):
        s = slots[r // (tq // half)]
        r0 = (r % (tq // half)) * half
        return jnp.concatenate([k_ref[s, r0:r0 + half, pr * pair:(pr + 1) * pair],
                                k_aux[r * half:(r + 1) * half, :]], axis=1)

    def score_dot(lhs, rhs):
        if rhs.shape[1] > tq:
            return jnp.dot(lhs, rhs, preferred_element_type=jnp.float32)
        h0 = lhs.shape[0] // 2
        return jnp.concatenate([jnp.dot(lhs[0:h0, :], rhs, preferred_element_type=jnp.float32),
                                jnp.dot(lhs[h0:, :], rhs, preferred_element_type=jnp.float32)], axis=0)

    def visible(r, skip_own=False):
        return [m for m in range(2)
                if 0 <= r - m < span // half and not (skip_own and r == own + m)]

    n_stage = pt_ref.shape[0]
    for buf in range(n_stage):
        for e in range(2):
            pt_ref[buf, e, span:win, 0:half] = jnp.zeros((win - span, half), bf16)
            pt_ref[buf, e, 0:win - span, half:tq] = jnp.zeros((win - span, half), bf16)

    def attend(pr):
        buf = pr % n_stage
        gated_t, denoms = [], []
        for e in range(2):
            rows = slice((2 * pr + e) * HEAD_DIM, (2 * pr + e + 1) * HEAD_DIM)
            vt_win = jnp.concatenate([vt_ref[s, rows, :] for s in slots], axis=1)
            v_ext = jnp.concatenate([vt_win, ones_v], axis=0)
            o_ext = jnp.dot(v_ext, pt_ref[buf, e], preferred_element_type=jnp.float32)
            denom = o_ext[HEAD_DIM:HEAD_DIM + SUBLANES, :]
            o_t = o_ext[0:HEAD_DIM, :] * jnp.concatenate([1.0 / denom] * (HEAD_DIM // SUBLANES), axis=0)
            gated_t.append(o_t * gate_ref[rows, :])
            denoms.append(denom)
        gated_ref[:, pr * pair:(pr + 1) * pair] = jnp.concatenate(gated_t, axis=0).T.astype(bf16)
        return denoms

    def project_out():
        for r0 in range(0, tq, half):
            y = jnp.dot(gated_ref[r0:r0 + half, :], wout_ref[...], preferred_element_type=jnp.float32)
            o_ref[0, r0:r0 + half, :] = x[r0:r0 + half, :] + _rms_norm(y, gpost_ref[layer:layer + 1, :])

    def own_scores(pr):
        buf = pr % n_stage
        maxima = [[None, None], [None, None]]
        for m in range(2):
            st = score_dot(lhs_tile(pr, own + m), rhs_tile(pr, m, None))
            for e in range(2):
                s = st[:, e * half:(e + 1) * half] + bias_ref[2 * pr + e, HIST:HIST + half, :]
                own_ref[buf, e, m] = s
                maxima[e][m] = jnp.max(s, axis=0, keepdims=True)
        return maxima

    def fast_probs(pr, maxima):
        buf = pr % n_stage
        shifts = [[mx.astype(bf16).astype(jnp.float32) for mx in me] for me in maxima]
        rhs = jnp.concatenate([rhs_tile(pr, m, shifts) for m in range(2)], axis=1)
        groups = [[0], [1, 2, 3], [4]]
        for rows in groups:
            ms = visible(rows[0], skip_own=True)
            lhs = jnp.concatenate([lhs_tile(pr, r) for r in rows], axis=0)
            st_all = score_dot(lhs, rhs[:, ms[0] * tq:(ms[-1] + 1) * tq])
            for ri, r in enumerate(rows):
                st = st_all[ri * half:(ri + 1) * half, :]
                for mi, m in enumerate(ms):
                    for e in range(2):
                        kr = slice((r - m) * half, (r - m + 1) * half)
                        c0 = mi * tq + e * half
                        p = jnp.exp2(st[:, c0:c0 + half] + bias_ref[2 * pr + e, kr, :])
                        pt_ref[buf, e, r * half:(r + 1) * half, m * half:(m + 1) * half] = p.astype(bf16)
        for m in range(2):
            for e in range(2):
                p = jnp.exp2(own_ref[buf, e, m] - shifts[e][m])
                pt_ref[buf, e, (own + m) * half:(own + m + 1) * half,
                       m * half:(m + 1) * half] = p.astype(bf16)

    n_pairs = n_heads // 2
    overflow = jnp.zeros((SUBLANES, tq), jnp.float32)
    maxima = {pr: own_scores(pr) for pr in range(min(2, n_pairs))}
    fast_probs(0, maxima.pop(0))
    for pr in range(n_pairs):
        if pr + 2 < n_pairs:
            maxima[pr + 2] = own_scores(pr + 2)
        if pr + 1 < n_pairs:
            fast_probs(pr + 1, maxima.pop(pr + 1))
        for denom in attend(pr):
            overflow = jnp.maximum(overflow, jnp.where(denom < SOFTMAX_SUM_LIMIT, 0.0, 1.0))
    project_out()

    def exact_scores(pr):
        rhs = jnp.concatenate([rhs_tile(pr, m, None) for m in range(2)], axis=1)
        partial = [[None, None], [None, None]]
        for r in range(n_tiles):
            ms = visible(r)
            st = score_dot(lhs_tile(pr, r), rhs[:, ms[0] * tq:(ms[-1] + 1) * tq])
            for mi, m in enumerate(ms):
                for e in range(2):
                    kr = slice((r - m) * half, (r - m + 1) * half)
                    c0 = mi * tq + e * half
                    s = st[:, c0:c0 + half] + bias_ref[2 * pr + e, kr, :]
                    s_ref[e, m, kr, :] = s
                    for i in range(0, half, SUBLANES):
                        tile = s[i:i + SUBLANES, :]
                        prev = partial[e][m]
                        partial[e][m] = tile if prev is None else jnp.maximum(prev, tile)
        return [[jnp.max(p, axis=0, keepdims=True) for p in pe] for pe in partial]

    @pl.when(jnp.max(overflow) > 0.0)
    def _():
        for pr in range(n_pairs):
            exact_maxima = exact_scores(pr)
            for e in range(2):
                for m in range(2):
                    p = jnp.exp2(s_ref[e, m] - exact_maxima[e][m]).astype(bf16)
                    pt_ref[pr % n_stage, e, m * half:m * half + span, m * half:(m + 1) * half] = p
            attend(pr)
        project_out()


def _resident(shape):
    zeros = (0,) * len(shape)
    return pl.BlockSpec(shape, lambda b, j: zeros, pipeline_mode=pl.Buffered(1))


def _row_block(rows, d):
    return pl.BlockSpec((1, rows, d), lambda b, j: (b, j, 0))


_COMPILER_PARAMS = pltpu.CompilerParams(
    dimension_semantics=("arbitrary", "arbitrary"), vmem_limit_bytes=VMEM_LIMIT_BYTES)


def _pool_layer(x, layer, norm_pre, norm_post, w_in, w_group, scale, w_out):
    b, s, d = x.shape
    width = w_out.shape[0]
    tm = POOL_BLOCK_ROWS
    return pl.pallas_call(
        functools.partial(_pool_layer_kernel, layer),
        grid=(b, s // tm),
        in_specs=[_row_block(tm, d), _resident(norm_pre.shape), _resident(norm_post.shape),
                  _resident(w_in.shape), _resident(w_group.shape), _resident(scale.shape),
                  _resident(w_out.shape)],
        out_specs=_row_block(tm, d),
        out_shape=jax.ShapeDtypeStruct(x.shape, x.dtype),
        scratch_shapes=[pltpu.VMEM((d, width), jnp.bfloat16),
                        pltpu.VMEM((POOL_HALO, width), jnp.float32),
                        pltpu.VMEM((tm // POOL_SUB_ROWS, POOL_SUB_ROWS, width), jnp.bfloat16)],
        compiler_params=_COMPILER_PARAMS,
        name="pool_layer",
    )(x, norm_pre, norm_post, w_in, w_group, scale, w_out)


def _att_layer(x, layer, norm_pre, norm_post, w_in, rel_bias, w_out):
    b, s, d = x.shape
    width = w_out.shape[0]
    n_heads = width // HEAD_DIM
    tq = ATT_BLOCK_ROWS
    n_slots = HIST // tq + 1
    return pl.pallas_call(
        functools.partial(_att_layer_kernel, layer),
        grid=(b, s // tq),
        in_specs=[_row_block(tq, d), _resident(norm_pre.shape), _resident(norm_post.shape),
                  _resident(w_in.shape), _resident(rel_bias.shape),
                  _resident(w_out.shape)],
        out_specs=_row_block(tq, d),
        out_shape=jax.ShapeDtypeStruct(x.shape, x.dtype),
        scratch_shapes=[pltpu.VMEM((len(ATT_T_BLOCKS), width, d), jnp.bfloat16),
                        pltpu.VMEM((n_slots, tq, width), jnp.bfloat16),
                        pltpu.VMEM((n_slots, width, tq), jnp.bfloat16),
                        pltpu.VMEM((width, tq), jnp.bfloat16),
                        pltpu.VMEM((width, tq), jnp.float32),
                        pltpu.VMEM((tq, width), jnp.bfloat16),
                        pltpu.VMEM((n_heads, ATT_SPAN, LANES), jnp.float32),
                        pltpu.VMEM((ATT_STAGES, 2, 2, LANES, LANES), jnp.float32),
                        pltpu.VMEM((2, 2, ATT_SPAN, LANES), jnp.float32),
                        pltpu.VMEM((ATT_STAGES, 2, HIST + tq, tq), jnp.bfloat16)],
        compiler_params=_COMPILER_PARAMS,
        name="att_layer",
    )(x, norm_pre, norm_post, w_in, rel_bias, w_out)


def kernel(x, norm_pre, norm_post, pool_w_in, pool_w_group, pool_scale, pool_w_out,
           att_w_in, att_rel_bias, att_w_out):
    bf16 = jnp.bfloat16
    x = _pool_layer(x, 0, norm_pre, norm_post, pool_w_in[0].astype(bf16),
                    pool_w_group[0].astype(bf16), pool_scale[0:1], pool_w_out[0].astype(bf16))
    x = _att_layer(x, 1, norm_pre, norm_post, att_w_in[0].astype(bf16), att_rel_bias[0:1],
                   att_w_out[0].astype(bf16))
    return x
```

```python
import functools

import jax
import jax.numpy as jnp
from jax import lax
from jax.experimental import pallas as pl
from jax.experimental.pallas import tpu as pltpu

RMS_EPS = 1e-6
LOG2_E = 1.4426950408889634
SOFTMAX_SUM_LIMIT = 2.0 ** 100
POOL_WINDOWS = (2, 4, 8, 16)
POOL_HALO = 16
CHUNK = 64
LEFT_CHUNKS = 8
HEAD_DIM = 64
MAX_REL = 256
HIST = LEFT_CHUNKS * CHUNK

LANES = 128
SUBLANES = 8
BF16_ROWS = 16
V7X_VMEM_BYTES = 64 * 1024 * 1024
MASK_PENALTY = -1e30
WEIGHT_CHUNK_ROWS = 256
POOL_BLOCK_ROWS = 512
POOL_SUB_ROWS = 256
ATT_BLOCK_ROWS = 256
ATT_T_BLOCKS = (0, 2, 3)
ATT_STAGES = 4
ATT_SPAN = HIST + LANES
REL_LANES = HIST + ATT_BLOCK_ROWS
VMEM_LIMIT_BYTES = V7X_VMEM_BYTES * 7 // 8


def _rms_norm(x, g):
    ms = jnp.mean(x * x, axis=-1, keepdims=True)
    return x * lax.rsqrt(ms + RMS_EPS) * g


def _load_weights_as_bf16(jobs, stage_ref, sem_ref):
    rows_per = stage_ref.shape[1]
    chunks = [(src, dst, r0) for src, dst in jobs for r0 in range(0, src.shape[0], rows_per)]

    def copy(i):
        src, _, r0 = chunks[i]
        slot = i % 2
        return pltpu.make_async_copy(src.at[pl.ds(r0, rows_per), :],
                                     stage_ref.at[slot, :, pl.ds(0, src.shape[1])], sem_ref.at[slot])

    copy(0).start()
    for i, (src, dst, r0) in enumerate(chunks):
        if i + 1 < len(chunks):
            copy(i + 1).start()
        copy(i).wait()
        dst[r0:r0 + rows_per, :] = stage_ref[i % 2, :, 0:src.shape[1]].astype(jnp.bfloat16)


def _pool_layer_kernel(layer, x_ref, gpre_ref, gpost_ref, win_hbm, wg_hbm, scale_ref, wout_hbm,
                       o_ref, win_ref, wg_ref, wout_ref, stage_ref, sem_ref, wmix_ref, carry_ref, act_ref):
    b = pl.program_id(0)
    j = pl.program_id(1)
    tm = x_ref.shape[1]
    n_sub, sub, width = act_ref.shape
    n_groups = len(POOL_WINDOWS)
    gw = width // n_groups

    @pl.when((b == 0) & (j == 0))
    def _():
        _load_weights_as_bf16([(win_hbm.at[0], win_ref), (wout_hbm.at[0], wout_ref)]
                              + [(wg_hbm.at[0, gi], wg_ref.at[gi]) for gi in range(n_groups)],
                              stage_ref, sem_ref)
        for gi in range(n_groups):
            cols = slice(gi * gw, (gi + 1) * gw)
            wmix_ref[:, cols] = jnp.dot(win_ref[:, cols], wg_ref[gi],
                                        preferred_element_type=jnp.float32).astype(jnp.bfloat16)

    @pl.when(j == 0)
    def _():
        carry_ref[...] = jnp.zeros_like(carry_ref)

    for sb in range(n_sub):
        r0 = sb * sub
        x = x_ref[0, r0:r0 + sub, :]
        hb = _rms_norm(x, gpre_ref[layer:layer + 1, :]).astype(jnp.bfloat16)

        t = j * tm + r0 + lax.broadcasted_iota(jnp.int32, (sub, 1), 0)

        def project(gi):
            a = jnp.dot(hb, wmix_ref[:, gi * gw:(gi + 1) * gw], preferred_element_type=jnp.float32)
            z = jnp.dot(hb, win_ref[:, width + gi * gw: width + (gi + 1) * gw],
                        preferred_element_type=jnp.float32)
            return a, z

        projected = project(0)
        for gi, w in enumerate(POOL_WINDOWS):
            cols = slice(gi * gw, (gi + 1) * gw)
            a, z = projected
            if gi + 1 < n_groups:
                projected = project(gi + 1)
            ext = jnp.concatenate([carry_ref[:, cols], a], axis=0)
            carry_ref[:, cols] = a[sub - POOL_HALO:, :]
            s = ext
            k = 1
            while k < w:
                s = s + pltpu.roll(s, k, axis=0)
                k *= 2
            cnt = jnp.minimum(t + 1, w).astype(jnp.float32)
            pooled = s[POOL_HALO:, :] / cnt
            m = (pooled - a) * scale_ref[:, cols]
            act_ref[sb, :, cols] = (m * (z * jax.nn.sigmoid(z))).astype(jnp.bfloat16)

        y = jnp.dot(act_ref[sb], wout_ref[...], preferred_element_type=jnp.float32)
        o_ref[0, r0:r0 + sub, :] = x + _rms_norm(y, gpost_ref[layer:layer + 1, :])


def _build_bias_table(rb_ref, bias_ref):
    n_heads, rows, cols = bias_ref.shape
    lanes = REL_LANES
    n_rel = 2 * MAX_REL
    u = lax.broadcasted_iota(jnp.int32, (1, lanes), 1)
    in_table = (u > cols) & (u < n_rel)
    jj = lax.broadcasted_iota(jnp.int32, (rows, cols), 0)
    c = lax.broadcasted_iota(jnp.int32, (rows, cols), 1)
    kc, qc = jj // CHUNK, c // CHUNK
    in_band = (kc >= qc) & (kc <= qc + LEFT_CHUNKS)
    for h in range(n_heads):
        far = rb_ref[0, h:h + 1, n_rel:n_rel + 1]
        row = jnp.concatenate([rb_ref[0, h:h + 1, 0:n_rel], jnp.zeros((1, lanes - n_rel), jnp.float32)], axis=1)
        g = jnp.where(in_table, row, far)
        t = pltpu.roll(jnp.broadcast_to(g, (rows, lanes)), 0, 1, stride=1, stride_axis=0)
        bias_ref[h] = jnp.where(in_band, t[:, 0:cols] * LOG2_E, -jnp.inf)


def _att_layer_kernel(layer, x_ref, gpre_ref, gpost_ref, win_hbm, rb_ref, wout_hbm,
                      o_ref, win_ref, wout_ref, stage_ref, sem_ref, wt_ref, k_ref, vt_ref, qt_ref, gate_ref, gated_ref, bias_ref, own_ref, s_ref, pt_ref):
    b = pl.program_id(0)
    j = pl.program_id(1)
    tq = x_ref.shape[1]
    width = wout_ref.shape[0]
    n_heads = width // HEAD_DIM
    win = HIST + tq
    half = tq // 2
    span = bias_ref.shape[1]
    pair = 2 * HEAD_DIM
    bf16 = jnp.bfloat16

    @pl.when((b == 0) & (j == 0))
    def _():
        _load_weights_as_bf16([(win_hbm.at[0], win_ref), (wout_hbm.at[0], wout_ref)], stage_ref, sem_ref)
        _build_bias_table(rb_ref, bias_ref)
        for t, blk in enumerate(ATT_T_BLOCKS):
            for c0 in range(0, width, tq):
                w_blk = win_ref[:, blk * width + c0:blk * width + c0 + tq]
                wt_ref[t, c0:c0 + tq, :] = w_blk.astype(jnp.float32).T.astype(bf16)

    n_slots = k_ref.shape[0]
    slots = [lax.rem(j + 1 + i, n_slots) for i in range(n_slots)]

    @pl.when(j == 0)
    def _():
        k_ref[...] = jnp.zeros(k_ref.shape, bf16)
        vt_ref[...] = jnp.zeros(vt_ref.shape, bf16)

    x = x_ref[0]
    h = _rms_norm(x, gpre_ref[layer:layer + 1, :])
    hb = h.astype(bf16)
    hbt = h.T.astype(bf16)
    k = jnp.dot(hb, win_ref[:, width:2 * width], preferred_element_type=jnp.float32)
    k_ref[slots[-1]] = k.astype(bf16)
    qt = jnp.dot(wt_ref[0], hbt, preferred_element_type=jnp.float32)
    qt_ref[...] = (qt * (HEAD_DIM ** -0.5 * LOG2_E)).astype(bf16)
    vt = jnp.dot(wt_ref[1], hbt, preferred_element_type=jnp.float32)
    vt_ref[slots[-1]] = vt.astype(bf16)
    zt = jnp.dot(wt_ref[2], hbt, preferred_element_type=jnp.float32)
    gate_ref[...] = zt * jax.nn.sigmoid(zt)

    row = lax.broadcasted_iota(jnp.int32, (win, pair), 0)
    lane = lax.broadcasted_iota(jnp.int32, (win, pair), 1)
    k_aux = jnp.where(lane == 0, jnp.where(row < HIST - j * tq, 1.0, 0.0),
                      jnp.where(lane == 1, 1.0, 0.0)).astype(bf16)
    r2 = lax.broadcasted_iota(jnp.int32, (pair, half), 0)
    zeros_q = jnp.zeros((HEAD_DIM, half), bf16)
    ones_v = jnp.ones((BF16_ROWS, win), bf16)
    n_tiles = win // half
    own = HIST // half

    def penalty(shift=None):
        sub = 0.0 if shift is None else jnp.where(r2 == 1, -shift, 0.0)
        return jnp.where(r2 == 0, MASK_PENALTY, sub).astype(bf16)

    def rhs_tile(pr, m, shifts):
        p0 = pr * pair
        cols = slice(m * half, (m + 1) * half)
        q_a = qt_ref[p0:p0 + HEAD_DIM, cols]
        q_b = qt_ref[p0 + HEAD_DIM:p0 + pair, cols]
        pen = [penalty(None if shifts is None else shifts[e][m]) for e in range(2)]
        return jnp.concatenate([jnp.concatenate([q_a, zeros_q, pen[0]], axis=0),
                                jnp.concatenate([zeros_q, q_b, pen[1]], axis=0)], axis=1)

    def lhs_tile(pr, r):
        s = slots[r // (tq // half)]
        r0 = (r % (tq // half)) * half
        return jnp.concatenate([k_ref[s, r0:r0 + half, pr * pair:(pr + 1) * pair],
                                k_aux[r * half:(r + 1) * half, :]], axis=1)

    def score_dot(lhs, rhs):
        if rhs.shape[1] > tq:
            return jnp.dot(lhs, rhs, preferred_element_type=jnp.float32)
        h0 = lhs.shape[0] // 2
        return jnp.concatenate([jnp.dot(lhs[0:h0, :], rhs, preferred_element_type=jnp.float32),
                                jnp.dot(lhs[h0:, :], rhs, preferred_element_type=jnp.float32)], axis=0)

    def visible(r, skip_own=False):
        return [m for m in range(2)
                if 0 <= r - m < span // half and not (skip_own and r == own + m)]

    n_stage = pt_ref.shape[0]
    for buf in range(n_stage):
        for e in range(2):
            pt_ref[buf, e, span:win, 0:half] = jnp.zeros((win - span, half), bf16)
            pt_ref[buf, e, 0:win - span, half:tq] = jnp.zeros((win - span, half), bf16)

    def attend(pr):
        buf = pr % n_stage
        gated_t, denoms = [], []
        for e in range(2):
            rows = slice((2 * pr + e) * HEAD_DIM, (2 * pr + e + 1) * HEAD_DIM)
            vt_win = jnp.concatenate([vt_ref[s, rows, :] for s in slots], axis=1)
            v_ext = jnp.concatenate([vt_win, ones_v], axis=0)
            o_ext = jnp.dot(v_ext, pt_ref[buf, e], preferred_element_type=jnp.float32)
            denom = o_ext[HEAD_DIM:HEAD_DIM + SUBLANES, :]
            o_t = o_ext[0:HEAD_DIM, :] * jnp.concatenate([1.0 / denom] * (HEAD_DIM // SUBLANES), axis=0)
            gated_t.append(o_t * gate_ref[rows, :])
            denoms.append(denom)
        gated_ref[:, pr * pair:(pr + 1) * pair] = jnp.concatenate(gated_t, axis=0).T.astype(bf16)
        return denoms

    def project_out():
        for r0 in range(0, tq, half):
            y = jnp.dot(gated_ref[r0:r0 + half, :], wout_ref[...], preferred_element_type=jnp.float32)
            o_ref[0, r0:r0 + half, :] = x[r0:r0 + half, :] + _rms_norm(y, gpost_ref[layer:layer + 1, :])

    def own_scores(pr):
        buf = pr % n_stage
        maxima = [[None, None], [None, None]]
        for m in range(2):
            st = score_dot(lhs_tile(pr, own + m), rhs_tile(pr, m, None))
            for e in range(2):
                s = st[:, e * half:(e + 1) * half] + bias_ref[2 * pr + e, HIST:HIST + half, :]
                own_ref[buf, e, m] = s
                maxima[e][m] = jnp.max(s, axis=0, keepdims=True)
        return maxima

    def fast_probs(pr, maxima):
        buf = pr % n_stage
        shifts = [[mx.astype(bf16).astype(jnp.float32) for mx in me] for me in maxima]
        rhs = jnp.concatenate([rhs_tile(pr, m, shifts) for m in range(2)], axis=1)
        groups = [[0], [1, 2, 3], [4]]
        for rows in groups:
            ms = visible(rows[0], skip_own=True)
            lhs = jnp.concatenate([lhs_tile(pr, r) for r in rows], axis=0)
            st_all = score_dot(lhs, rhs[:, ms[0] * tq:(ms[-1] + 1) * tq])
            for ri, r in enumerate(rows):
                st = st_all[ri * half:(ri + 1) * half, :]
                for mi, m in enumerate(ms):
                    for e in range(2):
                        kr = slice((r - m) * half, (r - m + 1) * half)
                        c0 = mi * tq + e * half
                        p = jnp.exp2(st[:, c0:c0 + half] + bias_ref[2 * pr + e, kr, :])
                        pt_ref[buf, e, r * half:(r + 1) * half, m * half:(m + 1) * half] = p.astype(bf16)
        for m in range(2):
            for e in range(2):
                p = jnp.exp2(own_ref[buf, e, m] - shifts[e][m])
                pt_ref[buf, e, (own + m) * half:(own + m + 1) * half,
                       m * half:(m + 1) * half] = p.astype(bf16)

    n_pairs = n_heads // 2
    overflow = jnp.zeros((SUBLANES, tq), jnp.float32)
    maxima = {pr: own_scores(pr) for pr in range(min(2, n_pairs))}
    fast_probs(0, maxima.pop(0))
    for pr in range(n_pairs):
        if pr + 2 < n_pairs:
            maxima[pr + 2] = own_scores(pr + 2)
        if pr + 1 < n_pairs:
            fast_probs(pr + 1, maxima.pop(pr + 1))
        for denom in attend(pr):
            overflow = jnp.maximum(overflow, jnp.where(denom < SOFTMAX_SUM_LIMIT, 0.0, 1.0))
    project_out()

    def exact_scores(pr):
        rhs = jnp.concatenate([rhs_tile(pr, m, None) for m in range(2)], axis=1)
        partial = [[None, None], [None, None]]
        for r in range(n_tiles):
            ms = visible(r)
            st = score_dot(lhs_tile(pr, r), rhs[:, ms[0] * tq:(ms[-1] + 1) * tq])
            for mi, m in enumerate(ms):
                for e in range(2):
                    kr = slice((r - m) * half, (r - m + 1) * half)
                    c0 = mi * tq + e * half
                    s = st[:, c0:c0 + half] + bias_ref[2 * pr + e, kr, :]
                    s_ref[e, m, kr, :] = s
                    for i in range(0, half, SUBLANES):
                        tile = s[i:i + SUBLANES, :]
                        prev = partial[e][m]
                        partial[e][m] = tile if prev is None else jnp.maximum(prev, tile)
        return [[jnp.max(p, axis=0, keepdims=True) for p in pe] for pe in partial]

    @pl.when(jnp.max(overflow) > 0.0)
    def _():
        for pr in range(n_pairs):
            exact_maxima = exact_scores(pr)
            for e in range(2):
                for m in range(2):
                    p = jnp.exp2(s_ref[e, m] - exact_maxima[e][m]).astype(bf16)
                    pt_ref[pr % n_stage, e, m * half:m * half + span, m * half:(m + 1) * half] = p
            attend(pr)
        project_out()


def _resident(shape):
    zeros = (0,) * len(shape)
    return pl.BlockSpec(shape, lambda b, j: zeros, pipeline_mode=pl.Buffered(1))


def _row_block(rows, d):
    return pl.BlockSpec((1, rows, d), lambda b, j: (b, j, 0))


_IN_HBM = pl.BlockSpec(memory_space=pl.ANY)


_COMPILER_PARAMS = pltpu.CompilerParams(
    dimension_semantics=("arbitrary", "arbitrary"), vmem_limit_bytes=VMEM_LIMIT_BYTES)


def _pool_layer(x, layer, norm_pre, norm_post, w_in, w_group, scale, w_out):
    b, s, d = x.shape
    width = w_out.shape[1]
    tm = POOL_BLOCK_ROWS
    return pl.pallas_call(
        functools.partial(_pool_layer_kernel, layer),
        grid=(b, s // tm),
        in_specs=[_row_block(tm, d), _resident(norm_pre.shape), _resident(norm_post.shape),
                  _IN_HBM, _IN_HBM, _resident(scale.shape), _IN_HBM],
        out_specs=_row_block(tm, d),
        out_shape=jax.ShapeDtypeStruct(x.shape, x.dtype),
        scratch_shapes=[pltpu.VMEM(w_in.shape[1:], jnp.bfloat16),
                        pltpu.VMEM(w_group.shape[1:], jnp.bfloat16),
                        pltpu.VMEM(w_out.shape[1:], jnp.bfloat16),
                        pltpu.VMEM((2, WEIGHT_CHUNK_ROWS, w_in.shape[2]), jnp.float32),
                        pltpu.SemaphoreType.DMA((2,)),
                        pltpu.VMEM((d, width), jnp.bfloat16),
                        pltpu.VMEM((POOL_HALO, width), jnp.float32),
                        pltpu.VMEM((tm // POOL_SUB_ROWS, POOL_SUB_ROWS, width), jnp.bfloat16)],
        compiler_params=_COMPILER_PARAMS,
        name="pool_layer",
    )(x, norm_pre, norm_post, w_in, w_group, scale, w_out)


def _att_layer(x, layer, norm_pre, norm_post, w_in, rel_bias, w_out):
    b, s, d = x.shape
    width = w_out.shape[1]
    n_heads = width // HEAD_DIM
    tq = ATT_BLOCK_ROWS
    n_slots = HIST // tq + 1
    return pl.pallas_call(
        functools.partial(_att_layer_kernel, layer),
        grid=(b, s // tq),
        in_specs=[_row_block(tq, d), _resident(norm_pre.shape), _resident(norm_post.shape),
                  _IN_HBM, _resident(rel_bias.shape), _IN_HBM],
        out_specs=_row_block(tq, d),
        out_shape=jax.ShapeDtypeStruct(x.shape, x.dtype),
        scratch_shapes=[pltpu.VMEM(w_in.shape[1:], jnp.bfloat16),
                        pltpu.VMEM(w_out.shape[1:], jnp.bfloat16),
                        pltpu.VMEM((2, WEIGHT_CHUNK_ROWS, w_in.shape[2]), jnp.float32),
                        pltpu.SemaphoreType.DMA((2,)),
                        pltpu.VMEM((len(ATT_T_BLOCKS), width, d), jnp.bfloat16),
                        pltpu.VMEM((n_slots, tq, width), jnp.bfloat16),
                        pltpu.VMEM((n_slots, width, tq), jnp.bfloat16),
                        pltpu.VMEM((width, tq), jnp.bfloat16),
                        pltpu.VMEM((width, tq), jnp.float32),
                        pltpu.VMEM((tq, width), jnp.bfloat16),
                        pltpu.VMEM((n_heads, ATT_SPAN, LANES), jnp.float32),
                        pltpu.VMEM((ATT_STAGES, 2, 2, LANES, LANES), jnp.float32),
                        pltpu.VMEM((2, 2, ATT_SPAN, LANES), jnp.float32),
                        pltpu.VMEM((ATT_STAGES, 2, HIST + tq, tq), jnp.bfloat16)],
        compiler_params=_COMPILER_PARAMS,
        name="att_layer",
    )(x, norm_pre, norm_post, w_in, rel_bias, w_out)


def kernel(x, norm_pre, norm_post, pool_w_in, pool_w_group, pool_scale, pool_w_out,
           att_w_in, att_rel_bias, att_w_out):
    x = _pool_layer(x, 0, norm_pre, norm_post, pool_w_in, pool_w_group, pool_scale, pool_w_out)
    x = _att_layer(x, 1, norm_pre, norm_post, att_w_in, att_rel_bias, att_w_out)
    return x
```

```python
import functools

import jax
import jax.numpy as jnp
from jax import lax
from jax.experimental import pallas as pl
from jax.experimental.pallas import tpu as pltpu

RMS_EPS = 1e-6
LOG2_E = 1.4426950408889634
SOFTMAX_SUM_LIMIT = 2.0 ** 100
POOL_WINDOWS = (2, 4, 8, 16)
POOL_HALO = 16
CHUNK = 64
LEFT_CHUNKS = 8
HEAD_DIM = 64
MAX_REL = 256
HIST = LEFT_CHUNKS * CHUNK

LANES = 128
SUBLANES = 8
BF16_ROWS = 16
V7X_VMEM_BYTES = 64 * 1024 * 1024
MASK_PENALTY = -1e30
WEIGHT_CHUNK_ROWS = 128
WEIGHT_STAGE_SLOTS = 4
POOL_BLOCK_ROWS = 512
POOL_SUB_ROWS = 256
ATT_BLOCK_ROWS = 256
ATT_T_BLOCKS = (0, 2, 3)
ATT_STAGES = 4
ATT_SPAN = HIST + LANES
REL_LANES = HIST + ATT_BLOCK_ROWS
VMEM_LIMIT_BYTES = V7X_VMEM_BYTES * 7 // 8


def _rms_norm(x, g):
    ms = jnp.mean(x * x, axis=-1, keepdims=True)
    return x * lax.rsqrt(ms + RMS_EPS) * g


def _load_weights_as_bf16(jobs, stage_ref, sem_ref):
    n_slots, rows_per, _ = stage_ref.shape
    depth = n_slots - 1
    chunks = [(src, dst, r0) for src, dst in jobs for r0 in range(0, src.shape[0], rows_per)]

    def copy(i):
        src, _, r0 = chunks[i]
        slot = i % n_slots
        return pltpu.make_async_copy(src.at[pl.ds(r0, rows_per), :],
                                     stage_ref.at[slot, :, pl.ds(0, src.shape[1])], sem_ref.at[slot])

    for i in range(min(depth, len(chunks))):
        copy(i).start()
    for i, (src, dst, r0) in enumerate(chunks):
        if i + depth < len(chunks):
            copy(i + depth).start()
        copy(i).wait()
        dst[r0:r0 + rows_per, :] = stage_ref[i % n_slots, :, 0:src.shape[1]].astype(jnp.bfloat16)


def _pool_layer_kernel(layer, x_ref, gpre_ref, gpost_ref, win_hbm, wg_hbm, scale_ref, wout_hbm,
                       o_ref, win_ref, wg_ref, wout_ref, stage_ref, sem_ref, wmix_ref, carry_ref, act_ref):
    b = pl.program_id(0)
    j = pl.program_id(1)
    tm = x_ref.shape[1]
    n_sub, sub, width = act_ref.shape
    n_groups = len(POOL_WINDOWS)
    gw = width // n_groups

    @pl.when((b == 0) & (j == 0))
    def _():
        _load_weights_as_bf16([(win_hbm.at[0], win_ref), (wout_hbm.at[0], wout_ref)]
                              + [(wg_hbm.at[0, gi], wg_ref.at[gi]) for gi in range(n_groups)],
                              stage_ref, sem_ref)
        for gi in range(n_groups):
            cols = slice(gi * gw, (gi + 1) * gw)
            wmix_ref[:, cols] = jnp.dot(win_ref[:, cols], wg_ref[gi],
                                        preferred_element_type=jnp.float32).astype(jnp.bfloat16)

    @pl.when(j == 0)
    def _():
        carry_ref[...] = jnp.zeros_like(carry_ref)

    for sb in range(n_sub):
        r0 = sb * sub
        x = x_ref[0, r0:r0 + sub, :]
        hb = _rms_norm(x, gpre_ref[layer:layer + 1, :]).astype(jnp.bfloat16)

        t = j * tm + r0 + lax.broadcasted_iota(jnp.int32, (sub, 1), 0)

        def project(gi):
            a = jnp.dot(hb, wmix_ref[:, gi * gw:(gi + 1) * gw], preferred_element_type=jnp.float32)
            z = jnp.dot(hb, win_ref[:, width + gi * gw: width + (gi + 1) * gw],
                        preferred_element_type=jnp.float32)
            return a, z

        projected = project(0)
        for gi, w in enumerate(POOL_WINDOWS):
            cols = slice(gi * gw, (gi + 1) * gw)
            a, z = projected
            if gi + 1 < n_groups:
                projected = project(gi + 1)
            ext = jnp.concatenate([carry_ref[:, cols], a], axis=0)
            carry_ref[:, cols] = a[sub - POOL_HALO:, :]
            s = ext
            k = 1
            while k < w:
                s = s + pltpu.roll(s, k, axis=0)
                k *= 2
            cnt = jnp.minimum(t + 1, w).astype(jnp.float32)
            pooled = s[POOL_HALO:, :] / cnt
            m = (pooled - a) * scale_ref[:, cols]
            act_ref[sb, :, cols] = (m * (z * jax.nn.sigmoid(z))).astype(jnp.bfloat16)

        y = jnp.dot(act_ref[sb], wout_ref[...], preferred_element_type=jnp.float32)
        o_ref[0, r0:r0 + sub, :] = x + _rms_norm(y, gpost_ref[layer:layer + 1, :])


def _build_bias_table(rb_ref, bias_ref):
    n_heads, rows, cols = bias_ref.shape
    lanes = REL_LANES
    n_rel = 2 * MAX_REL
    u = lax.broadcasted_iota(jnp.int32, (1, lanes), 1)
    in_table = (u > cols) & (u < n_rel)
    jj = lax.broadcasted_iota(jnp.int32, (rows, cols), 0)
    c = lax.broadcasted_iota(jnp.int32, (rows, cols), 1)
    kc, qc = jj // CHUNK, c // CHUNK
    in_band = (kc >= qc) & (kc <= qc + LEFT_CHUNKS)
    for h in range(n_heads):
        far = rb_ref[0, h:h + 1, n_rel:n_rel + 1]
        row = jnp.concatenate([rb_ref[0, h:h + 1, 0:n_rel], jnp.zeros((1, lanes - n_rel), jnp.float32)], axis=1)
        g = jnp.where(in_table, row, far)
        t = pltpu.roll(jnp.broadcast_to(g, (rows, lanes)), 0, 1, stride=1, stride_axis=0)
        bias_ref[h] = jnp.where(in_band, t[:, 0:cols] * LOG2_E, -jnp.inf)


def _att_layer_kernel(layer, x_ref, gpre_ref, gpost_ref, win_hbm, rb_ref, wout_hbm,
                      o_ref, win_ref, wout_ref, stage_ref, sem_ref, wt_ref, k_ref, vt_ref, qt_ref, gate_ref, gated_ref, bias_ref, own_ref, s_ref, pt_ref):
    b = pl.program_id(0)
    j = pl.program_id(1)
    tq = x_ref.shape[1]
    width = wout_ref.shape[0]
    n_heads = width // HEAD_DIM
    win = HIST + tq
    half = tq // 2
    span = bias_ref.shape[1]
    pair = 2 * HEAD_DIM
    bf16 = jnp.bfloat16

    @pl.when((b == 0) & (j == 0))
    def _():
        _load_weights_as_bf16([(win_hbm.at[0], win_ref), (wout_hbm.at[0], wout_ref)], stage_ref, sem_ref)
        _build_bias_table(rb_ref, bias_ref)
        for t, blk in enumerate(ATT_T_BLOCKS):
            for c0 in range(0, width, tq):
                w_blk = win_ref[:, blk * width + c0:blk * width + c0 + tq]
                wt_ref[t, c0:c0 + tq, :] = w_blk.astype(jnp.float32).T.astype(bf16)

    n_slots = k_ref.shape[0]
    slots = [lax.rem(j + 1 + i, n_slots) for i in range(n_slots)]

    @pl.when(j == 0)
    def _():
        k_ref[...] = jnp.zeros(k_ref.shape, bf16)
        vt_ref[...] = jnp.zeros(vt_ref.shape, bf16)

    x = x_ref[0]
    h = _rms_norm(x, gpre_ref[layer:layer + 1, :])
    hb = h.astype(bf16)
    hbt = h.T.astype(bf16)
    k = jnp.dot(hb, win_ref[:, width:2 * width], preferred_element_type=jnp.float32)
    k_ref[slots[-1]] = k.astype(bf16)
    qt = jnp.dot(wt_ref[0], hbt, preferred_element_type=jnp.float32)
    qt_ref[...] = (qt * (HEAD_DIM ** -0.5 * LOG2_E)).astype(bf16)
    vt = jnp.dot(wt_ref[1], hbt, preferred_element_type=jnp.float32)
    vt_ref[slots[-1]] = vt.astype(bf16)
    zt = jnp.dot(wt_ref[2], hbt, preferred_element_type=jnp.float32)
    gate_ref[...] = zt * jax.nn.sigmoid(zt)

    row = lax.broadcasted_iota(jnp.int32, (win, pair), 0)
    lane = lax.broadcasted_iota(jnp.int32, (win, pair), 1)
    k_aux = jnp.where(lane == 0, jnp.where(row < HIST - j * tq, 1.0, 0.0),
                      jnp.where(lane == 1, 1.0, 0.0)).astype(bf16)
    r2 = lax.broadcasted_iota(jnp.int32, (pair, half), 0)
    zeros_q = jnp.zeros((HEAD_DIM, half), bf16)
    ones_v = jnp.ones((BF16_ROWS, win), bf16)
    n_tiles = win // half
    own = HIST // half

    def penalty(shift=None):
        sub = 0.0 if shift is None else jnp.where(r2 == 1, -shift, 0.0)
        return jnp.where(r2 == 0, MASK_PENALTY, sub).astype(bf16)

    def rhs_tile(pr, m, shifts):
        p0 = pr * pair
        cols = slice(m * half, (m + 1) * half)
        q_a = qt_ref[p0:p0 + HEAD_DIM, cols]
        q_b = qt_ref[p0 + HEAD_DIM:p0 + pair, cols]
        pen = [penalty(None if shifts is None else shifts[e][m]) for e in range(2)]
        return jnp.concatenate([jnp.concatenate([q_a, zeros_q, pen[0]], axis=0),
                                jnp.concatenate([zeros_q, q_b, pen[1]], axis=0)], axis=1)

    def lhs_tile(pr, r):
        s = slots[r // (tq // half)]
        r0 = (r % (tq // half)) * half
        return jnp.concatenate([k_ref[s, r0:r0 + half, pr * pair:(pr + 1) * pair],
                                k_aux[r * half:(r + 1) * half, :]], axis=1)

    def score_dot(lhs, rhs):
        if rhs.shape[1] > tq:
            return jnp.dot(lhs, rhs, preferred_element_type=jnp.float32)
        h0 = lhs.shape[0] // 2
        return jnp.concatenate([jnp.dot(lhs[0:h0, :], rhs, preferred_element_type=jnp.float32),
                                jnp.dot(lhs[h0:, :], rhs, preferred_element_type=jnp.float32)], axis=0)

    def visible(r, skip_own=False):
        return [m for m in range(2)
                if 0 <= r - m < span // half and not (skip_own and r == own + m)]

    n_stage = pt_ref.shape[0]
    for buf in range(n_stage):
        for e in range(2):
            pt_ref[buf, e, span:win, 0:half] = jnp.zeros((win - span, half), bf16)
            pt_ref[buf, e, 0:win - span, half:tq] = jnp.zeros((win - span, half), bf16)

    def attend(pr):
        buf = pr % n_stage
        gated_t, denoms = [], []
        for e in range(2):
            rows = slice((2 * pr + e) * HEAD_DIM, (2 * pr + e + 1) * HEAD_DIM)
            vt_win = jnp.concatenate([vt_ref[s, rows, :] for s in slots], axis=1)
            v_ext = jnp.concatenate([vt_win, ones_v], axis=0)
            o_ext = jnp.dot(v_ext, pt_ref[buf, e], preferred_element_type=jnp.float32)
            denom = o_ext[HEAD_DIM:HEAD_DIM + SUBLANES, :]
            o_t = o_ext[0:HEAD_DIM, :] * jnp.concatenate([1.0 / denom] * (HEAD_DIM // SUBLANES), axis=0)
            gated_t.append(o_t * gate_ref[rows, :])
            denoms.append(denom)
        gated_ref[:, pr * pair:(pr + 1) * pair] = jnp.concatenate(gated_t, axis=0).T.astype(bf16)
        return denoms

    def project_out():
        for r0 in range(0, tq, half):
            y = jnp.dot(gated_ref[r0:r0 + half, :], wout_ref[...], preferred_element_type=jnp.float32)
            o_ref[0, r0:r0 + half, :] = x[r0:r0 + half, :] + _rms_norm(y, gpost_ref[layer:layer + 1, :])

    def own_scores(pr):
        buf = pr % n_stage
        maxima = [[None, None], [None, None]]
        for m in range(2):
            st = score_dot(lhs_tile(pr, own + m), rhs_tile(pr, m, None))
            for e in range(2):
                s = st[:, e * half:(e + 1) * half] + bias_ref[2 * pr + e, HIST:HIST + half, :]
                own_ref[buf, e, m] = s
                maxima[e][m] = jnp.max(s, axis=0, keepdims=True)
        return maxima

    def fast_probs(pr, maxima):
        buf = pr % n_stage
        shifts = [[mx.astype(bf16).astype(jnp.float32) for mx in me] for me in maxima]
        rhs = jnp.concatenate([rhs_tile(pr, m, shifts) for m in range(2)], axis=1)
        groups = [[0], [1, 2, 3], [4]]
        for rows in groups:
            ms = visible(rows[0], skip_own=True)
            lhs = jnp.concatenate([lhs_tile(pr, r) for r in rows], axis=0)
            st_all = score_dot(lhs, rhs[:, ms[0] * tq:(ms[-1] + 1) * tq])
            for ri, r in enumerate(rows):
                st = st_all[ri * half:(ri + 1) * half, :]
                for mi, m in enumerate(ms):
                    for e in range(2):
                        kr = slice((r - m) * half, (r - m + 1) * half)
                        c0 = mi * tq + e * half
                        p = jnp.exp2(st[:, c0:c0 + half] + bias_ref[2 * pr + e, kr, :])
                        pt_ref[buf, e, r * half:(r + 1) * half, m * half:(m + 1) * half] = p.astype(bf16)
        for m in range(2):
            for e in range(2):
                p = jnp.exp2(own_ref[buf, e, m] - shifts[e][m])
                pt_ref[buf, e, (own + m) * half:(own + m + 1) * half,
                       m * half:(m + 1) * half] = p.astype(bf16)

    n_pairs = n_heads // 2
    overflow = jnp.zeros((SUBLANES, tq), jnp.float32)
    maxima = {pr: own_scores(pr) for pr in range(min(2, n_pairs))}
    fast_probs(0, maxima.pop(0))
    for pr in range(n_pairs):
        if pr + 2 < n_pairs:
            maxima[pr + 2] = own_scores(pr + 2)
        if pr + 1 < n_pairs:
            fast_probs(pr + 1, maxima.pop(pr + 1))
        for denom in attend(pr):
            overflow = jnp.maximum(overflow, jnp.where(denom < SOFTMAX_SUM_LIMIT, 0.0, 1.0))
    project_out()

    def exact_scores(pr):
        rhs = jnp.concatenate([rhs_tile(pr, m, None) for m in range(2)], axis=1)
        partial = [[None, None], [None, None]]
        for r in range(n_tiles):
            ms = visible(r)
            st = score_dot(lhs_tile(pr, r), rhs[:, ms[0] * tq:(ms[-1] + 1) * tq])
            for mi, m in enumerate(ms):
                for e in range(2):
                    kr = slice((r - m) * half, (r - m + 1) * half)
                    c0 = mi * tq + e * half
                    s = st[:, c0:c0 + half] + bias_ref[2 * pr + e, kr, :]
                    s_ref[e, m, kr, :] = s
                    for i in range(0, half, SUBLANES):
                        tile = s[i:i + SUBLANES, :]
                        prev = partial[e][m]
                        partial[e][m] = tile if prev is None else jnp.maximum(prev, tile)
        return [[jnp.max(p, axis=0, keepdims=True) for p in pe] for pe in partial]

    @pl.when(jnp.max(overflow) > 0.0)
    def _():
        for pr in range(n_pairs):
            exact_maxima = exact_scores(pr)
            for e in range(2):
                for m in range(2):
                    p = jnp.exp2(s_ref[e, m] - exact_maxima[e][m]).astype(bf16)
                    pt_ref[pr % n_stage, e, m * half:m * half + span, m * half:(m + 1) * half] = p
            attend(pr)
        project_out()


def _resident(shape):
    zeros = (0,) * len(shape)
    return pl.BlockSpec(shape, lambda b, j: zeros, pipeline_mode=pl.Buffered(1))


def _row_block(rows, d):
    return pl.BlockSpec((1, rows, d), lambda b, j: (b, j, 0))


_IN_HBM = pl.BlockSpec(memory_space=pl.ANY)


_COMPILER_PARAMS = pltpu.CompilerParams(
    dimension_semantics=("arbitrary", "arbitrary"), vmem_limit_bytes=VMEM_LIMIT_BYTES)


def _pool_layer(x, layer, norm_pre, norm_post, w_in, w_group, scale, w_out):
    b, s, d = x.shape
    width = w_out.shape[1]
    tm = POOL_BLOCK_ROWS
    return pl.pallas_call(
        functools.partial(_pool_layer_kernel, layer),
        grid=(b, s // tm),
        in_specs=[_row_block(tm, d), _resident(norm_pre.shape), _resident(norm_post.shape),
                  _IN_HBM, _IN_HBM, _resident(scale.shape), _IN_HBM],
        out_specs=_row_block(tm, d),
        out_shape=jax.ShapeDtypeStruct(x.shape, x.dtype),
        scratch_shapes=[pltpu.VMEM(w_in.shape[1:], jnp.bfloat16),
                        pltpu.VMEM(w_group.shape[1:], jnp.bfloat16),
                        pltpu.VMEM(w_out.shape[1:], jnp.bfloat16),
                        pltpu.VMEM((WEIGHT_STAGE_SLOTS, WEIGHT_CHUNK_ROWS, w_in.shape[2]), jnp.float32),
                        pltpu.SemaphoreType.DMA((WEIGHT_STAGE_SLOTS,)),
                        pltpu.VMEM((d, width), jnp.bfloat16),
                        pltpu.VMEM((POOL_HALO, width), jnp.float32),
                        pltpu.VMEM((tm // POOL_SUB_ROWS, POOL_SUB_ROWS, width), jnp.bfloat16)],
        compiler_params=_COMPILER_PARAMS,
        name="pool_layer",
    )(x, norm_pre, norm_post, w_in, w_group, scale, w_out)


def _att_layer(x, layer, norm_pre, norm_post, w_in, rel_bias, w_out):
    b, s, d = x.shape
    width = w_out.shape[1]
    n_heads = width // HEAD_DIM
    tq = ATT_BLOCK_ROWS
    n_slots = HIST // tq + 1
    return pl.pallas_call(
        functools.partial(_att_layer_kernel, layer),
        grid=(b, s // tq),
        in_specs=[_row_block(tq, d), _resident(norm_pre.shape), _resident(norm_post.shape),
                  _IN_HBM, _resident(rel_bias.shape), _IN_HBM],
        out_specs=_row_block(tq, d),
        out_shape=jax.ShapeDtypeStruct(x.shape, x.dtype),
        scratch_shapes=[pltpu.VMEM(w_in.shape[1:], jnp.bfloat16),
                        pltpu.VMEM(w_out.shape[1:], jnp.bfloat16),
                        pltpu.VMEM((WEIGHT_STAGE_SLOTS, WEIGHT_CHUNK_ROWS, w_in.shape[2]), jnp.float32),
                        pltpu.SemaphoreType.DMA((WEIGHT_STAGE_SLOTS,)),
                        pltpu.VMEM((len(ATT_T_BLOCKS), width, d), jnp.bfloat16),
                        pltpu.VMEM((n_slots, tq, width), jnp.bfloat16),
                        pltpu.VMEM((n_slots, width, tq), jnp.bfloat16),
                        pltpu.VMEM((width, tq), jnp.bfloat16),
                        pltpu.VMEM((width, tq), jnp.float32),
                        pltpu.VMEM((tq, width), jnp.bfloat16),
                        pltpu.VMEM((n_heads, ATT_SPAN, LANES), jnp.float32),
                        pltpu.VMEM((ATT_STAGES, 2, 2, LANES, LANES), jnp.float32),
                        pltpu.VMEM((2, 2, ATT_SPAN, LANES), jnp.float32),
                        pltpu.VMEM((ATT_STAGES, 2, HIST + tq, tq), jnp.bfloat16)],
        compiler_params=_COMPILER_PARAMS,
        name="att_layer",
    )(x, norm_pre, norm_post, w_in, rel_bias, w_out)


def kernel(x, norm_pre, norm_post, pool_w_in, pool_w_group, pool_scale, pool_w_out,
           att_w_in, att_rel_bias, att_w_out):
    x = _pool_layer(x, 0, norm_pre, norm_post, pool_w_in, pool_w_group, pool_scale, pool_w_out)
    x = _att_layer(x, 1, norm_pre, norm_post, att_w_in, att_rel_bias, att_w_out)
    return x
```

```python
import functools

import jax
import jax.numpy as jnp
from jax import lax
from jax.experimental import pallas as pl
from jax.experimental.pallas import tpu as pltpu

RMS_EPS = 1e-6
LOG2_E = 1.4426950408889634
SOFTMAX_SUM_LIMIT = 2.0 ** 100
POOL_WINDOWS = (2, 4, 8, 16)
POOL_HALO = 16
CHUNK = 64
LEFT_CHUNKS = 8
HEAD_DIM = 64
MAX_REL = 256
HIST = LEFT_CHUNKS * CHUNK

LANES = 128
SUBLANES = 8
BF16_ROWS = 16
V7X_VMEM_BYTES = 64 * 1024 * 1024
MASK_PENALTY = -1e30
WEIGHT_TILE_COLS = 256
WEIGHT_STAGE_SLOTS = 4
POOL_BLOCK_ROWS = 512
POOL_SUB_ROWS = 256
ATT_BLOCK_ROWS = 256
ATT_T_BLOCKS = (0, 2, 3)
ATT_STAGES = 4
ATT_SPAN = HIST + LANES
REL_LANES = HIST + ATT_BLOCK_ROWS
VMEM_LIMIT_BYTES = V7X_VMEM_BYTES * 7 // 8


def _rms_norm(x, g):
    ms = jnp.mean(x * x, axis=-1, keepdims=True)
    return x * lax.rsqrt(ms + RMS_EPS) * g


def _stream_weights(chunks, stage_ref, sem_ref):
    n_slots = stage_ref.shape[0]
    depth = n_slots - 1

    def copy(i):
        src = chunks[i][0]
        rows, cols = src.shape
        slot = i % n_slots
        return pltpu.make_async_copy(src, stage_ref.at[slot, pl.ds(0, rows), pl.ds(0, cols)], sem_ref.at[slot])

    for i in range(min(depth, len(chunks))):
        copy(i).start()
    for i, (src, consume) in enumerate(chunks):
        if i + depth < len(chunks):
            copy(i + depth).start()
        copy(i).wait()
        rows, cols = src.shape
        consume(stage_ref[i % n_slots, 0:rows, 0:cols])


def _pool_layer_kernel(layer, x_ref, gpre_ref, gpost_ref, win_hbm, wg_hbm, scale_ref, wout_hbm,
                       o_ref, wa_ref, wz_ref, wg_ref, wout_ref, stage_ref, sem_ref, wmix_ref, carry_ref, act_ref):
    b = pl.program_id(0)
    j = pl.program_id(1)
    tm = x_ref.shape[1]
    n_sub, sub, width = act_ref.shape
    n_groups = len(POOL_WINDOWS)
    gw = width // n_groups

    @pl.when((b == 0) & (j == 0))
    def _():
        cw = stage_ref.shape[2]

        def cast_into(dst_ref, rows, c0):
            def consume(tile):
                dst_ref[rows, c0:c0 + cw] = tile.astype(jnp.bfloat16)
            return consume

        def mix_group(gi, last_consume):
            def consume(tile):
                last_consume(tile)
                cols = slice(gi * gw, (gi + 1) * gw)
                wmix_ref[:, cols] = jnp.dot(wa_ref[:, cols], wg_ref[gi],
                                            preferred_element_type=jnp.float32).astype(jnp.bfloat16)
            return consume

        d_in, d_grp = wa_ref.shape[0], wg_ref.shape[1]
        chunks = []
        for gi in range(n_groups):
            for c0 in range(0, gw, cw):
                chunks.append((wg_hbm.at[0, gi, :, pl.ds(c0, cw)], cast_into(wg_ref.at[gi], slice(0, d_grp), c0)))
            for c0 in range(gi * gw, (gi + 1) * gw, cw):
                chunks.append((win_hbm.at[0, :, pl.ds(c0, cw)], cast_into(wa_ref, slice(0, d_in), c0)))
            chunks[-1] = (chunks[-1][0], mix_group(gi, chunks[-1][1]))
        for c0 in range(0, width, cw):
            chunks.append((win_hbm.at[0, :, pl.ds(width + c0, cw)], cast_into(wz_ref, slice(0, d_in), c0)))
        for r0 in range(0, width, d_in):
            for c0 in range(0, wout_ref.shape[1], cw):
                chunks.append((wout_hbm.at[0, pl.ds(r0, d_in), pl.ds(c0, cw)],
                               cast_into(wout_ref, slice(r0, r0 + d_in), c0)))
        _stream_weights(chunks, stage_ref, sem_ref)

    @pl.when(j == 0)
    def _():
        carry_ref[...] = jnp.zeros_like(carry_ref)

    for sb in range(n_sub):
        r0 = sb * sub
        x = x_ref[0, r0:r0 + sub, :]
        hb = _rms_norm(x, gpre_ref[layer:layer + 1, :]).astype(jnp.bfloat16)

        t = j * tm + r0 + lax.broadcasted_iota(jnp.int32, (sub, 1), 0)

        def project(gi):
            a = jnp.dot(hb, wmix_ref[:, gi * gw:(gi + 1) * gw], preferred_element_type=jnp.float32)
            z = jnp.dot(hb, wz_ref[:, gi * gw:(gi + 1) * gw],
                        preferred_element_type=jnp.float32)
            return a, z

        projected = project(0)
        for gi, w in enumerate(POOL_WINDOWS):
            cols = slice(gi * gw, (gi + 1) * gw)
            a, z = projected
            if gi + 1 < n_groups:
                projected = project(gi + 1)
            ext = jnp.concatenate([carry_ref[:, cols], a], axis=0)
            carry_ref[:, cols] = a[sub - POOL_HALO:, :]
            s = ext
            k = 1
            while k < w:
                s = s + pltpu.roll(s, k, axis=0)
                k *= 2
            cnt = jnp.minimum(t + 1, w).astype(jnp.float32)
            pooled = s[POOL_HALO:, :] / cnt
            m = (pooled - a) * scale_ref[:, cols]
            act_ref[sb, :, cols] = (m * (z * jax.nn.sigmoid(z))).astype(jnp.bfloat16)

        y = jnp.dot(act_ref[sb], wout_ref[...], preferred_element_type=jnp.float32)
        o_ref[0, r0:r0 + sub, :] = x + _rms_norm(y, gpost_ref[layer:layer + 1, :])


def _build_bias_table(rb_ref, bias_ref, heads):
    n_heads, rows, cols = bias_ref.shape
    lanes = REL_LANES
    n_rel = 2 * MAX_REL
    u = lax.broadcasted_iota(jnp.int32, (1, lanes), 1)
    in_table = (u > cols) & (u < n_rel)
    jj = lax.broadcasted_iota(jnp.int32, (rows, cols), 0)
    c = lax.broadcasted_iota(jnp.int32, (rows, cols), 1)
    kc, qc = jj // CHUNK, c // CHUNK
    in_band = (kc >= qc) & (kc <= qc + LEFT_CHUNKS)
    for h in heads:
        far = rb_ref[0, h:h + 1, n_rel:n_rel + 1]
        row = jnp.concatenate([rb_ref[0, h:h + 1, 0:n_rel], jnp.zeros((1, lanes - n_rel), jnp.float32)], axis=1)
        g = jnp.where(in_table, row, far)
        t = pltpu.roll(jnp.broadcast_to(g, (rows, lanes)), 0, 1, stride=1, stride_axis=0)
        bias_ref[h] = jnp.where(in_band, t[:, 0:cols] * LOG2_E, -jnp.inf)


def _att_layer_kernel(layer, x_ref, gpre_ref, gpost_ref, win_hbm, rb_ref, wout_hbm,
                      o_ref, wk_ref, wout_ref, stage_ref, sem_ref, wt_ref, k_ref, vt_ref, qt_ref, gate_ref, gated_ref, bias_ref, own_ref, s_ref, pt_ref):
    b = pl.program_id(0)
    j = pl.program_id(1)
    tq = x_ref.shape[1]
    width = wout_ref.shape[0]
    n_heads = width // HEAD_DIM
    win = HIST + tq
    half = tq // 2
    span = bias_ref.shape[1]
    pair = 2 * HEAD_DIM
    bf16 = jnp.bfloat16

    @pl.when((b == 0) & (j == 0))
    def _():
        cw = stage_ref.shape[2]

        def cast_into(dst_ref, c0):
            def consume(tile):
                dst_ref[:, c0:c0 + cw] = tile.astype(bf16)
            return consume

        def transpose_into(t, c0, heads):
            def consume(tile):
                wt_ref[t, c0:c0 + cw, :] = tile.T.astype(bf16)
                _build_bias_table(rb_ref, bias_ref, heads)
            return consume

        t_tiles = [(t, blk, c0) for t, blk in enumerate(ATT_T_BLOCKS) for c0 in range(0, width, cw)]
        per_tile = -(-n_heads // len(t_tiles))
        chunks = []
        for i, (t, blk, c0) in enumerate(t_tiles):
            heads = range(min(i * per_tile, n_heads), min((i + 1) * per_tile, n_heads))
            chunks.append((win_hbm.at[0, :, pl.ds(blk * width + c0, cw)], transpose_into(t, c0, heads)))
        for c0 in range(0, width, cw):
            chunks.append((win_hbm.at[0, :, pl.ds(width + c0, cw)], cast_into(wk_ref, c0)))
        for c0 in range(0, wout_ref.shape[1], cw):
            chunks.append((wout_hbm.at[0, :, pl.ds(c0, cw)], cast_into(wout_ref, c0)))
        _stream_weights(chunks, stage_ref, sem_ref)

    n_slots = k_ref.shape[0]
    slots = [lax.rem(j + 1 + i, n_slots) for i in range(n_slots)]

    @pl.when(j == 0)
    def _():
        k_ref[...] = jnp.zeros(k_ref.shape, bf16)
        vt_ref[...] = jnp.zeros(vt_ref.shape, bf16)

    x = x_ref[0]
    h = _rms_norm(x, gpre_ref[layer:layer + 1, :])
    hb = h.astype(bf16)
    hbt = h.T.astype(bf16)
    k = jnp.dot(hb, wk_ref[...], preferred_element_type=jnp.float32)
    k_ref[slots[-1]] = k.astype(bf16)
    qt = jnp.dot(wt_ref[0], hbt, preferred_element_type=jnp.float32)
    qt_ref[...] = (qt * (HEAD_DIM ** -0.5 * LOG2_E)).astype(bf16)
    vt = jnp.dot(wt_ref[1], hbt, preferred_element_type=jnp.float32)
    vt_ref[slots[-1]] = vt.astype(bf16)
    zt = jnp.dot(wt_ref[2], hbt, preferred_element_type=jnp.float32)
    gate_ref[...] = zt * jax.nn.sigmoid(zt)

    row = lax.broadcasted_iota(jnp.int32, (win, pair), 0)
    lane = lax.broadcasted_iota(jnp.int32, (win, pair), 1)
    k_aux = jnp.where(lane == 0, jnp.where(row < HIST - j * tq, 1.0, 0.0),
                      jnp.where(lane == 1, 1.0, 0.0)).astype(bf16)
    r2 = lax.broadcasted_iota(jnp.int32, (pair, half), 0)
    zeros_q = jnp.zeros((HEAD_DIM, half), bf16)
    ones_v = jnp.ones((BF16_ROWS, win), bf16)
    n_tiles = win // half
    own = HIST // half

    def penalty(shift=None):
        sub = 0.0 if shift is None else jnp.where(r2 == 1, -shift, 0.0)
        return jnp.where(r2 == 0, MASK_PENALTY, sub).astype(bf16)

    def rhs_tile(pr, m, shifts):
        p0 = pr * pair
        cols = slice(m * half, (m + 1) * half)
        q_a = qt_ref[p0:p0 + HEAD_DIM, cols]
        q_b = qt_ref[p0 + HEAD_DIM:p0 + pair, cols]
        pen = [penalty(None if shifts is None else shifts[e][m]) for e in range(2)]
        return jnp.concatenate([jnp.concatenate([q_a, zeros_q, pen[0]], axis=0),
                                jnp.concatenate([zeros_q, q_b, pen[1]], axis=0)], axis=1)

    def lhs_tile(pr, r):
        s = slots[r // (tq // half)]
        r0 = (r % (tq // half)) * half
        return jnp.concatenate([k_ref[s, r0:r0 + half, pr * pair:(pr + 1) * pair],
                                k_aux[r * half:(r + 1) * half, :]], axis=1)

    def score_dot(lhs, rhs):
        if rhs.shape[1] > tq:
            return jnp.dot(lhs, rhs, preferred_element_type=jnp.float32)
        h0 = lhs.shape[0] // 2
        return jnp.concatenate([jnp.dot(lhs[0:h0, :], rhs, preferred_element_type=jnp.float32),
                                jnp.dot(lhs[h0:, :], rhs, preferred_element_type=jnp.float32)], axis=0)

    def visible(r, skip_own=False):
        return [m for m in range(2)
                if 0 <= r - m < span // half and not (skip_own and r == own + m)]

    n_stage = pt_ref.shape[0]
    for buf in range(n_stage):
        for e in range(2):
            pt_ref[buf, e, span:win, 0:half] = jnp.zeros((win - span, half), bf16)
            pt_ref[buf, e, 0:win - span, half:tq] = jnp.zeros((win - span, half), bf16)

    def attend(pr):
        buf = pr % n_stage
        gated_t, denoms = [], []
        for e in range(2):
            rows = slice((2 * pr + e) * HEAD_DIM, (2 * pr + e + 1) * HEAD_DIM)
            vt_win = jnp.concatenate([vt_ref[s, rows, :] for s in slots], axis=1)
            v_ext = jnp.concatenate([vt_win, ones_v], axis=0)
            o_ext = jnp.dot(v_ext, pt_ref[buf, e], preferred_element_type=jnp.float32)
            denom = o_ext[HEAD_DIM:HEAD_DIM + SUBLANES, :]
            o_t = o_ext[0:HEAD_DIM, :] * jnp.concatenate([1.0 / denom] * (HEAD_DIM // SUBLANES), axis=0)
            gated_t.append(o_t * gate_ref[rows, :])
            denoms.append(denom)
        gated_ref[:, pr * pair:(pr + 1) * pair] = jnp.concatenate(gated_t, axis=0).T.astype(bf16)
        return denoms

    def project_out():
        for r0 in range(0, tq, half):
            y = jnp.dot(gated_ref[r0:r0 + half, :], wout_ref[...], preferred_element_type=jnp.float32)
            o_ref[0, r0:r0 + half, :] = x[r0:r0 + half, :] + _rms_norm(y, gpost_ref[layer:layer + 1, :])

    def own_scores(pr):
        buf = pr % n_stage
        maxima = [[None, None], [None, None]]
        for m in range(2):
            st = score_dot(lhs_tile(pr, own + m), rhs_tile(pr, m, None))
            for e in range(2):
                s = st[:, e * half:(e + 1) * half] + bias_ref[2 * pr + e, HIST:HIST + half, :]
                own_ref[buf, e, m] = s
                maxima[e][m] = jnp.max(s, axis=0, keepdims=True)
        return maxima

    def fast_probs(pr, maxima):
        buf = pr % n_stage
        shifts = [[mx.astype(bf16).astype(jnp.float32) for mx in me] for me in maxima]
        rhs = jnp.concatenate([rhs_tile(pr, m, shifts) for m in range(2)], axis=1)
        groups = [[0], [1, 2, 3], [4]]
        for rows in groups:
            ms = visible(rows[0], skip_own=True)
            lhs = jnp.concatenate([lhs_tile(pr, r) for r in rows], axis=0)
            st_all = score_dot(lhs, rhs[:, ms[0] * tq:(ms[-1] + 1) * tq])
            for ri, r in enumerate(rows):
                st = st_all[ri * half:(ri + 1) * half, :]
                for mi, m in enumerate(ms):
                    for e in range(2):
                        kr = slice((r - m) * half, (r - m + 1) * half)
                        c0 = mi * tq + e * half
                        p = jnp.exp2(st[:, c0:c0 + half] + bias_ref[2 * pr + e, kr, :])
                        pt_ref[buf, e, r * half:(r + 1) * half, m * half:(m + 1) * half] = p.astype(bf16)
        for m in range(2):
            for e in range(2):
                p = jnp.exp2(own_ref[buf, e, m] - shifts[e][m])
                pt_ref[buf, e, (own + m) * half:(own + m + 1) * half,
                       m * half:(m + 1) * half] = p.astype(bf16)

    n_pairs = n_heads // 2
    overflow = jnp.zeros((SUBLANES, tq), jnp.float32)
    maxima = {pr: own_scores(pr) for pr in range(min(2, n_pairs))}
    fast_probs(0, maxima.pop(0))
    for pr in range(n_pairs):
        if pr + 2 < n_pairs:
            maxima[pr + 2] = own_scores(pr + 2)
        if pr + 1 < n_pairs:
            fast_probs(pr + 1, maxima.pop(pr + 1))
        for denom in attend(pr):
            overflow = jnp.maximum(overflow, jnp.where(denom < SOFTMAX_SUM_LIMIT, 0.0, 1.0))
    project_out()

    def exact_scores(pr):
        rhs = jnp.concatenate([rhs_tile(pr, m, None) for m in range(2)], axis=1)
        partial = [[None, None], [None, None]]
        for r in range(n_tiles):
            ms = visible(r)
            st = score_dot(lhs_tile(pr, r), rhs[:, ms[0] * tq:(ms[-1] + 1) * tq])
            for mi, m in enumerate(ms):
                for e in range(2):
                    kr = slice((r - m) * half, (r - m + 1) * half)
                    c0 = mi * tq + e * half
                    s = st[:, c0:c0 + half] + bias_ref[2 * pr + e, kr, :]
                    s_ref[e, m, kr, :] = s
                    for i in range(0, half, SUBLANES):
                        tile = s[i:i + SUBLANES, :]
                        prev = partial[e][m]
                        partial[e][m] = tile if prev is None else jnp.maximum(prev, tile)
        return [[jnp.max(p, axis=0, keepdims=True) for p in pe] for pe in partial]

    @pl.when(jnp.max(overflow) > 0.0)
    def _():
        for pr in range(n_pairs):
            exact_maxima = exact_scores(pr)
            for e in range(2):
                for m in range(2):
                    p = jnp.exp2(s_ref[e, m] - exact_maxima[e][m]).astype(bf16)
                    pt_ref[pr % n_stage, e, m * half:m * half + span, m * half:(m + 1) * half] = p
            attend(pr)
        project_out()


def _resident(shape):
    zeros = (0,) * len(shape)
    return pl.BlockSpec(shape, lambda b, j: zeros, pipeline_mode=pl.Buffered(1))


def _row_block(rows, d):
    return pl.BlockSpec((1, rows, d), lambda b, j: (b, j, 0))


_IN_HBM = pl.BlockSpec(memory_space=pl.ANY)


_COMPILER_PARAMS = pltpu.CompilerParams(
    dimension_semantics=("arbitrary", "arbitrary"), vmem_limit_bytes=VMEM_LIMIT_BYTES)


def _pool_layer(x, layer, norm_pre, norm_post, w_in, w_group, scale, w_out):
    b, s, d = x.shape
    width = w_out.shape[1]
    tm = POOL_BLOCK_ROWS
    return pl.pallas_call(
        functools.partial(_pool_layer_kernel, layer),
        grid=(b, s // tm),
        in_specs=[_row_block(tm, d), _resident(norm_pre.shape), _resident(norm_post.shape),
                  _IN_HBM, _IN_HBM, _resident(scale.shape), _IN_HBM],
        out_specs=_row_block(tm, d),
        out_shape=jax.ShapeDtypeStruct(x.shape, x.dtype),
        scratch_shapes=[pltpu.VMEM((d, width), jnp.bfloat16),
                        pltpu.VMEM((d, width), jnp.bfloat16),
                        pltpu.VMEM(w_group.shape[1:], jnp.bfloat16),
                        pltpu.VMEM(w_out.shape[1:], jnp.bfloat16),
                        pltpu.VMEM((WEIGHT_STAGE_SLOTS, d, WEIGHT_TILE_COLS), jnp.float32),
                        pltpu.SemaphoreType.DMA((WEIGHT_STAGE_SLOTS,)),
                        pltpu.VMEM((d, width), jnp.bfloat16),
                        pltpu.VMEM((POOL_HALO, width), jnp.float32),
                        pltpu.VMEM((tm // POOL_SUB_ROWS, POOL_SUB_ROWS, width), jnp.bfloat16)],
        compiler_params=_COMPILER_PARAMS,
        name="pool_layer",
    )(x, norm_pre, norm_post, w_in, w_group, scale, w_out)


def _att_layer(x, layer, norm_pre, norm_post, w_in, rel_bias, w_out):
    b, s, d = x.shape
    width = w_out.shape[1]
    n_heads = width // HEAD_DIM
    tq = ATT_BLOCK_ROWS
    n_slots = HIST // tq + 1
    return pl.pallas_call(
        functools.partial(_att_layer_kernel, layer),
        grid=(b, s // tq),
        in_specs=[_row_block(tq, d), _resident(norm_pre.shape), _resident(norm_post.shape),
                  _IN_HBM, _resident(rel_bias.shape), _IN_HBM],
        out_specs=_row_block(tq, d),
        out_shape=jax.ShapeDtypeStruct(x.shape, x.dtype),
        scratch_shapes=[pltpu.VMEM((d, width), jnp.bfloat16),
                        pltpu.VMEM(w_out.shape[1:], jnp.bfloat16),
                        pltpu.VMEM((WEIGHT_STAGE_SLOTS, d, WEIGHT_TILE_COLS), jnp.float32),
                        pltpu.SemaphoreType.DMA((WEIGHT_STAGE_SLOTS,)),
                        pltpu.VMEM((len(ATT_T_BLOCKS), width, d), jnp.bfloat16),
                        pltpu.VMEM((n_slots, tq, width), jnp.bfloat16),
                        pltpu.VMEM((n_slots, width, tq), jnp.bfloat16),
                        pltpu.VMEM((width, tq), jnp.bfloat16),
                        pltpu.VMEM((width, tq), jnp.float32),
                        pltpu.VMEM((tq, width), jnp.bfloat16),
                        pltpu.VMEM((n_heads, ATT_SPAN, LANES), jnp.float32),
                        pltpu.VMEM((ATT_STAGES, 2, 2, LANES, LANES), jnp.float32),
                        pltpu.VMEM((2, 2, ATT_SPAN, LANES), jnp.float32),
                        pltpu.VMEM((ATT_STAGES, 2, HIST + tq, tq), jnp.bfloat16)],
        compiler_params=_COMPILER_PARAMS,
        name="att_layer",
    )(x, norm_pre, norm_post, w_in, rel_bias, w_out)


def kernel(x, norm_pre, norm_post, pool_w_in, pool_w_group, pool_scale, pool_w_out,
           att_w_in, att_rel_bias, att_w_out):
    x = _pool_layer(x, 0, norm_pre, norm_post, pool_w_in, pool_w_group, pool_scale, pool_w_out)
    x = _att_layer(x, 1, norm_pre, norm_post, att_w_in, att_rel_bias, att_w_out)
    return x
```

```python
import functools

import jax
import jax.numpy as jnp
from jax import lax
from jax.experimental import pallas as pl
from jax.experimental.pallas import tpu as pltpu

RMS_EPS = 1e-6
LOG2_E = 1.4426950408889634
SOFTMAX_SUM_LIMIT = 2.0 ** 100
POOL_WINDOWS = (2, 4, 8, 16)
POOL_HALO = 16
CHUNK = 64
LEFT_CHUNKS = 8
HEAD_DIM = 64
MAX_REL = 256
HIST = LEFT_CHUNKS * CHUNK

LANES = 128
SUBLANES = 8
BF16_ROWS = 16
V7X_VMEM_BYTES = 64 * 1024 * 1024
MASK_PENALTY = -1e30
WEIGHT_TILE_COLS = 256
WEIGHT_STAGE_SLOTS = 4
POOL_STAGE_SLOTS = 8
POOL_BLOCK_ROWS = 512
POOL_SUB_ROWS = 256
ATT_BLOCK_ROWS = 256
ATT_T_BLOCKS = (0, 2, 3)
ATT_STAGES = 4
ATT_SPAN = HIST + LANES
REL_LANES = HIST + ATT_BLOCK_ROWS
VMEM_LIMIT_BYTES = V7X_VMEM_BYTES * 7 // 8


def _rms_norm(x, g):
    ms = jnp.mean(x * x, axis=-1, keepdims=True)
    return x * lax.rsqrt(ms + RMS_EPS) * g


class _WeightStream:
    def __init__(self, chunks, stage_ref, sem_ref):
        self.chunks, self.stage_ref, self.sem_ref = chunks, stage_ref, sem_ref
        self.n_slots = stage_ref.shape[0]
        self.depth = self.n_slots - 1
        self.taken = 0
        for i in range(min(self.depth, len(chunks))):
            self._copy(i).start()

    def _copy(self, i):
        src = self.chunks[i][0]
        rows, cols = src.shape
        slot = i % self.n_slots
        return pltpu.make_async_copy(src, self.stage_ref.at[slot, pl.ds(0, rows), pl.ds(0, cols)],
                                     self.sem_ref.at[slot])

    def take(self, n):
        for i in range(self.taken, self.taken + n):
            src, consume = self.chunks[i]
            if i + self.depth < len(self.chunks):
                self._copy(i + self.depth).start()
            self._copy(i).wait()
            rows, cols = src.shape
            consume(self.stage_ref[i % self.n_slots, 0:rows, 0:cols])
        self.taken += n


def _pool_layer_kernel(layer, x_ref, gpre_ref, gpost_ref, win_hbm, wg_hbm, scale_ref, wout_hbm,
                       o_ref, wa_ref, wz_ref, wg_ref, wout_ref, stage_ref, sem_ref, wmix_ref, carry_ref, act_ref):
    b = pl.program_id(0)
    j = pl.program_id(1)
    tm = x_ref.shape[1]
    n_sub, sub, width = act_ref.shape
    n_groups = len(POOL_WINDOWS)
    gw = width // n_groups

    first = (b == 0) & (j == 0)

    @pl.when(j == 0)
    def _():
        carry_ref[...] = jnp.zeros_like(carry_ref)

    def normed(sb):
        r0 = sb * sub
        hb = _rms_norm(x_ref[0, r0:r0 + sub, :], gpre_ref[layer:layer + 1, :]).astype(jnp.bfloat16)
        t = j * tm + r0 + lax.broadcasted_iota(jnp.int32, (sub, 1), 0)
        return hb, t

    def project(hb, gi):
        a = jnp.dot(hb, wmix_ref[:, gi * gw:(gi + 1) * gw], preferred_element_type=jnp.float32)
        z = jnp.dot(hb, wz_ref[:, gi * gw:(gi + 1) * gw], preferred_element_type=jnp.float32)
        return a, z

    def mix(sb, gi, a, z, t):
        w = POOL_WINDOWS[gi]
        cols = slice(gi * gw, (gi + 1) * gw)
        ext = jnp.concatenate([carry_ref[:, cols], a], axis=0)
        carry_ref[:, cols] = a[sub - POOL_HALO:, :]
        s = ext
        k = 1
        while k < w:
            s = s + pltpu.roll(s, k, axis=0)
            k *= 2
        cnt = jnp.minimum(t + 1, w).astype(jnp.float32)
        pooled = s[POOL_HALO:, :] / cnt
        m = (pooled - a) * scale_ref[:, cols]
        act_ref[sb, :, cols] = (m * (z * jax.nn.sigmoid(z))).astype(jnp.bfloat16)

    def finish(sb):
        r0 = sb * sub
        y = jnp.dot(act_ref[sb], wout_ref[...], preferred_element_type=jnp.float32)
        o_ref[0, r0:r0 + sub, :] = x_ref[0, r0:r0 + sub, :] + _rms_norm(y, gpost_ref[layer:layer + 1, :])

    @pl.when(first)
    def _():
        cw = stage_ref.shape[2]

        def cast_into(dst_ref, rows, c0):
            def consume(tile):
                dst_ref[rows, c0:c0 + cw] = tile.astype(jnp.bfloat16)
            return consume

        def mix_group(gi, last_consume):
            def consume(tile):
                last_consume(tile)
                cols = slice(gi * gw, (gi + 1) * gw)
                wmix_ref[:, cols] = jnp.dot(wa_ref[:, cols], wg_ref[gi],
                                            preferred_element_type=jnp.float32).astype(jnp.bfloat16)
            return consume

        d_in, d_grp = wa_ref.shape[0], wg_ref.shape[1]
        group_chunks = []
        for gi in range(n_groups):
            chunks = []
            for c0 in range(0, gw, cw):
                chunks.append((wg_hbm.at[0, gi, :, pl.ds(c0, cw)], cast_into(wg_ref.at[gi], slice(0, d_grp), c0)))
            for c0 in range(gi * gw, (gi + 1) * gw, cw):
                chunks.append((win_hbm.at[0, :, pl.ds(c0, cw)], cast_into(wa_ref, slice(0, d_in), c0)))
            chunks[-1] = (chunks[-1][0], mix_group(gi, chunks[-1][1]))
            for c0 in range(gi * gw, (gi + 1) * gw, cw):
                chunks.append((win_hbm.at[0, :, pl.ds(width + c0, cw)], cast_into(wz_ref, slice(0, d_in), c0)))
            group_chunks.append(chunks)
        out_chunks = []
        for r0 in range(0, width, d_in):
            for c0 in range(0, wout_ref.shape[1], cw):
                out_chunks.append((wout_hbm.at[0, pl.ds(r0, d_in), pl.ds(c0, cw)],
                                   cast_into(wout_ref, slice(r0, r0 + d_in), c0)))
        stream = _WeightStream(sum(group_chunks, []) + out_chunks, stage_ref, sem_ref)

        blocks = [normed(sb) for sb in range(n_sub)]
        for gi in range(n_groups):
            stream.take(len(group_chunks[gi]))
            for sb, (hb, t) in enumerate(blocks):
                a, z = project(hb, gi)
                mix(sb, gi, a, z, t)
        stream.take(len(out_chunks))
        for sb in range(n_sub):
            finish(sb)

    @pl.when(jnp.logical_not(first))
    def _():
        for sb in range(n_sub):
            hb, t = normed(sb)
            projected = project(hb, 0)
            for gi in range(n_groups):
                a, z = projected
                if gi + 1 < n_groups:
                    projected = project(hb, gi + 1)
                mix(sb, gi, a, z, t)
            finish(sb)


def _build_bias_table(rb_ref, bias_ref, heads):
    n_heads, rows, cols = bias_ref.shape
    lanes = REL_LANES
    n_rel = 2 * MAX_REL
    u = lax.broadcasted_iota(jnp.int32, (1, lanes), 1)
    in_table = (u > cols) & (u < n_rel)
    jj = lax.broadcasted_iota(jnp.int32, (rows, cols), 0)
    c = lax.broadcasted_iota(jnp.int32, (rows, cols), 1)
    kc, qc = jj // CHUNK, c // CHUNK
    in_band = (kc >= qc) & (kc <= qc + LEFT_CHUNKS)
    for h in heads:
        far = rb_ref[0, h:h + 1, n_rel:n_rel + 1]
        row = jnp.concatenate([rb_ref[0, h:h + 1, 0:n_rel], jnp.zeros((1, lanes - n_rel), jnp.float32)], axis=1)
        g = jnp.where(in_table, row, far)
        t = pltpu.roll(jnp.broadcast_to(g, (rows, lanes)), 0, 1, stride=1, stride_axis=0)
        bias_ref[h] = jnp.where(in_band, t[:, 0:cols] * LOG2_E, -jnp.inf)


def _att_layer_kernel(layer, x_ref, gpre_ref, gpost_ref, win_hbm, rb_ref, wout_hbm,
                      o_ref, wk_ref, wout_ref, stage_ref, sem_ref, wt_ref, k_ref, vt_ref, qt_ref, gate_ref, gated_ref, bias_ref, own_ref, s_ref, pt_ref):
    b = pl.program_id(0)
    j = pl.program_id(1)
    tq = x_ref.shape[1]
    width = wout_ref.shape[0]
    n_heads = width // HEAD_DIM
    win = HIST + tq
    half = tq // 2
    span = bias_ref.shape[1]
    pair = 2 * HEAD_DIM
    bf16 = jnp.bfloat16

    @pl.when((b == 0) & (j == 0))
    def _():
        cw = stage_ref.shape[2]

        def cast_into(dst_ref, c0):
            def consume(tile):
                dst_ref[:, c0:c0 + cw] = tile.astype(bf16)
            return consume

        def transpose_into(t, c0, heads):
            def consume(tile):
                wt_ref[t, c0:c0 + cw, :] = tile.T.astype(bf16)
                _build_bias_table(rb_ref, bias_ref, heads)
            return consume

        t_tiles = [(t, blk, c0) for t, blk in enumerate(ATT_T_BLOCKS) for c0 in range(0, width, cw)]
        per_tile = -(-n_heads // len(t_tiles))
        chunks = []
        for i, (t, blk, c0) in enumerate(t_tiles):
            heads = range(min(i * per_tile, n_heads), min((i + 1) * per_tile, n_heads))
            chunks.append((win_hbm.at[0, :, pl.ds(blk * width + c0, cw)], transpose_into(t, c0, heads)))
        for c0 in range(0, width, cw):
            chunks.append((win_hbm.at[0, :, pl.ds(width + c0, cw)], cast_into(wk_ref, c0)))
        for c0 in range(0, wout_ref.shape[1], cw):
            chunks.append((wout_hbm.at[0, :, pl.ds(c0, cw)], cast_into(wout_ref, c0)))
        _WeightStream(chunks, stage_ref, sem_ref).take(len(chunks))

    n_slots = k_ref.shape[0]
    slots = [lax.rem(j + 1 + i, n_slots) for i in range(n_slots)]

    @pl.when(j == 0)
    def _():
        k_ref[...] = jnp.zeros(k_ref.shape, bf16)
        vt_ref[...] = jnp.zeros(vt_ref.shape, bf16)

    x = x_ref[0]
    h = _rms_norm(x, gpre_ref[layer:layer + 1, :])
    hb = h.astype(bf16)
    hbt = h.T.astype(bf16)
    k = jnp.dot(hb, wk_ref[...], preferred_element_type=jnp.float32)
    k_ref[slots[-1]] = k.astype(bf16)
    qt = jnp.dot(wt_ref[0], hbt, preferred_element_type=jnp.float32)
    qt_ref[...] = (qt * (HEAD_DIM ** -0.5 * LOG2_E)).astype(bf16)
    vt = jnp.dot(wt_ref[1], hbt, preferred_element_type=jnp.float32)
    vt_ref[slots[-1]] = vt.astype(bf16)
    zt = jnp.dot(wt_ref[2], hbt, preferred_element_type=jnp.float32)
    gate_ref[...] = zt * jax.nn.sigmoid(zt)

    row = lax.broadcasted_iota(jnp.int32, (win, pair), 0)
    lane = lax.broadcasted_iota(jnp.int32, (win, pair), 1)
    k_aux = jnp.where(lane == 0, jnp.where(row < HIST - j * tq, 1.0, 0.0),
                      jnp.where(lane == 1, 1.0, 0.0)).astype(bf16)
    r2 = lax.broadcasted_iota(jnp.int32, (pair, half), 0)
    zeros_q = jnp.zeros((HEAD_DIM, half), bf16)
    ones_v = jnp.ones((BF16_ROWS, win), bf16)
    n_tiles = win // half
    own = HIST // half

    def penalty(shift=None):
        sub = 0.0 if shift is None else jnp.where(r2 == 1, -shift, 0.0)
        return jnp.where(r2 == 0, MASK_PENALTY, sub).astype(bf16)

    def rhs_tile(pr, m, shifts):
        p0 = pr * pair
        cols = slice(m * half, (m + 1) * half)
        q_a = qt_ref[p0:p0 + HEAD_DIM, cols]
        q_b = qt_ref[p0 + HEAD_DIM:p0 + pair, cols]
        pen = [penalty(None if shifts is None else shifts[e][m]) for e in range(2)]
        return jnp.concatenate([jnp.concatenate([q_a, zeros_q, pen[0]], axis=0),
                                jnp.concatenate([zeros_q, q_b, pen[1]], axis=0)], axis=1)

    def lhs_tile(pr, r):
        s = slots[r // (tq // half)]
        r0 = (r % (tq // half)) * half
        return jnp.concatenate([k_ref[s, r0:r0 + half, pr * pair:(pr + 1) * pair],
                                k_aux[r * half:(r + 1) * half, :]], axis=1)

    def score_dot(lhs, rhs):
        if rhs.shape[1] > tq:
            return jnp.dot(lhs, rhs, preferred_element_type=jnp.float32)
        h0 = lhs.shape[0] // 2
        return jnp.concatenate([jnp.dot(lhs[0:h0, :], rhs, preferred_element_type=jnp.float32),
                                jnp.dot(lhs[h0:, :], rhs, preferred_element_type=jnp.float32)], axis=0)

    def visible(r, skip_own=False):
        return [m for m in range(2)
                if 0 <= r - m < span // half and not (skip_own and r == own + m)]

    n_stage = pt_ref.shape[0]
    for buf in range(n_stage):
        for e in range(2):
            pt_ref[buf, e, span:win, 0:half] = jnp.zeros((win - span, half), bf16)
            pt_ref[buf, e, 0:win - span, half:tq] = jnp.zeros((win - span, half), bf16)

    def attend(pr):
        buf = pr % n_stage
        gated_t, denoms = [], []
        for e in range(2):
            rows = slice((2 * pr + e) * HEAD_DIM, (2 * pr + e + 1) * HEAD_DIM)
            vt_win = jnp.concatenate([vt_ref[s, rows, :] for s in slots], axis=1)
            v_ext = jnp.concatenate([vt_win, ones_v], axis=0)
            o_ext = jnp.dot(v_ext, pt_ref[buf, e], preferred_element_type=jnp.float32)
            denom = o_ext[HEAD_DIM:HEAD_DIM + SUBLANES, :]
            o_t = o_ext[0:HEAD_DIM, :] * jnp.concatenate([1.0 / denom] * (HEAD_DIM // SUBLANES), axis=0)
            gated_t.append(o_t * gate_ref[rows, :])
            denoms.append(denom)
        gated_ref[:, pr * pair:(pr + 1) * pair] = jnp.concatenate(gated_t, axis=0).T.astype(bf16)
        return denoms

    def project_out():
        for r0 in range(0, tq, half):
            y = jnp.dot(gated_ref[r0:r0 + half, :], wout_ref[...], preferred_element_type=jnp.float32)
            o_ref[0, r0:r0 + half, :] = x[r0:r0 + half, :] + _rms_norm(y, gpost_ref[layer:layer + 1, :])

    def own_scores(pr):
        buf = pr % n_stage
        maxima = [[None, None], [None, None]]
        for m in range(2):
            st = score_dot(lhs_tile(pr, own + m), rhs_tile(pr, m, None))
            for e in range(2):
                s = st[:, e * half:(e + 1) * half] + bias_ref[2 * pr + e, HIST:HIST + half, :]
                own_ref[buf, e, m] = s
                maxima[e][m] = jnp.max(s, axis=0, keepdims=True)
        return maxima

    def fast_probs(pr, maxima):
        buf = pr % n_stage
        shifts = [[mx.astype(bf16).astype(jnp.float32) for mx in me] for me in maxima]
        rhs = jnp.concatenate([rhs_tile(pr, m, shifts) for m in range(2)], axis=1)
        groups = [[0], [1, 2, 3], [4]]
        for rows in groups:
            ms = visible(rows[0], skip_own=True)
            lhs = jnp.concatenate([lhs_tile(pr, r) for r in rows], axis=0)
            st_all = score_dot(lhs, rhs[:, ms[0] * tq:(ms[-1] + 1) * tq])
            for ri, r in enumerate(rows):
                st = st_all[ri * half:(ri + 1) * half, :]
                for mi, m in enumerate(ms):
                    for e in range(2):
                        kr = slice((r - m) * half, (r - m + 1) * half)
                        c0 = mi * tq + e * half
                        p = jnp.exp2(st[:, c0:c0 + half] + bias_ref[2 * pr + e, kr, :])
                        pt_ref[buf, e, r * half:(r + 1) * half, m * half:(m + 1) * half] = p.astype(bf16)
        for m in range(2):
            for e in range(2):
                p = jnp.exp2(own_ref[buf, e, m] - shifts[e][m])
                pt_ref[buf, e, (own + m) * half:(own + m + 1) * half,
                       m * half:(m + 1) * half] = p.astype(bf16)

    n_pairs = n_heads // 2
    overflow = jnp.zeros((SUBLANES, tq), jnp.float32)
    maxima = {pr: own_scores(pr) for pr in range(min(2, n_pairs))}
    fast_probs(0, maxima.pop(0))
    for pr in range(n_pairs):
        if pr + 2 < n_pairs:
            maxima[pr + 2] = own_scores(pr + 2)
        if pr + 1 < n_pairs:
            fast_probs(pr + 1, maxima.pop(pr + 1))
        for denom in attend(pr):
            overflow = jnp.maximum(overflow, jnp.where(denom < SOFTMAX_SUM_LIMIT, 0.0, 1.0))
    project_out()

    def exact_scores(pr):
        rhs = jnp.concatenate([rhs_tile(pr, m, None) for m in range(2)], axis=1)
        partial = [[None, None], [None, None]]
        for r in range(n_tiles):
            ms = visible(r)
            st = score_dot(lhs_tile(pr, r), rhs[:, ms[0] * tq:(ms[-1] + 1) * tq])
            for mi, m in enumerate(ms):
                for e in range(2):
                    kr = slice((r - m) * half, (r - m + 1) * half)
                    c0 = mi * tq + e * half
                    s = st[:, c0:c0 + half] + bias_ref[2 * pr + e, kr, :]
                    s_ref[e, m, kr, :] = s
                    for i in range(0, half, SUBLANES):
                        tile = s[i:i + SUBLANES, :]
                        prev = partial[e][m]
                        partial[e][m] = tile if prev is None else jnp.maximum(prev, tile)
        return [[jnp.max(p, axis=0, keepdims=True) for p in pe] for pe in partial]

    @pl.when(jnp.max(overflow) > 0.0)
    def _():
        for pr in range(n_pairs):
            exact_maxima = exact_scores(pr)
            for e in range(2):
                for m in range(2):
                    p = jnp.exp2(s_ref[e, m] - exact_maxima[e][m]).astype(bf16)
                    pt_ref[pr % n_stage, e, m * half:m * half + span, m * half:(m + 1) * half] = p
            attend(pr)
        project_out()


def _resident(shape):
    zeros = (0,) * len(shape)
    return pl.BlockSpec(shape, lambda b, j: zeros, pipeline_mode=pl.Buffered(1))


def _row_block(rows, d):
    return pl.BlockSpec((1, rows, d), lambda b, j: (b, j, 0))


_IN_HBM = pl.BlockSpec(memory_space=pl.ANY)


_COMPILER_PARAMS = pltpu.CompilerParams(
    dimension_semantics=("arbitrary", "arbitrary"), vmem_limit_bytes=VMEM_LIMIT_BYTES)


def _pool_layer(x, layer, norm_pre, norm_post, w_in, w_group, scale, w_out):
    b, s, d = x.shape
    width = w_out.shape[1]
    tm = POOL_BLOCK_ROWS
    return pl.pallas_call(
        functools.partial(_pool_layer_kernel, layer),
        grid=(b, s // tm),
        in_specs=[_row_block(tm, d), _resident(norm_pre.shape), _resident(norm_post.shape),
                  _IN_HBM, _IN_HBM, _resident(scale.shape), _IN_HBM],
        out_specs=_row_block(tm, d),
        out_shape=jax.ShapeDtypeStruct(x.shape, x.dtype),
        scratch_shapes=[pltpu.VMEM((d, width), jnp.bfloat16),
                        pltpu.VMEM((d, width), jnp.bfloat16),
                        pltpu.VMEM(w_group.shape[1:], jnp.bfloat16),
                        pltpu.VMEM(w_out.shape[1:], jnp.bfloat16),
                        pltpu.VMEM((POOL_STAGE_SLOTS, d, WEIGHT_TILE_COLS), jnp.float32),
                        pltpu.SemaphoreType.DMA((POOL_STAGE_SLOTS,)),
                        pltpu.VMEM((d, width), jnp.bfloat16),
                        pltpu.VMEM((POOL_HALO, width), jnp.float32),
                        pltpu.VMEM((tm // POOL_SUB_ROWS, POOL_SUB_ROWS, width), jnp.bfloat16)],
        compiler_params=_COMPILER_PARAMS,
        name="pool_layer",
    )(x, norm_pre, norm_post, w_in, w_group, scale, w_out)


def _att_layer(x, layer, norm_pre, norm_post, w_in, rel_bias, w_out):
    b, s, d = x.shape
    width = w_out.shape[1]
    n_heads = width // HEAD_DIM
    tq = ATT_BLOCK_ROWS
    n_slots = HIST // tq + 1
    return pl.pallas_call(
        functools.partial(_att_layer_kernel, layer),
        grid=(b, s // tq),
        in_specs=[_row_block(tq, d), _resident(norm_pre.shape), _resident(norm_post.shape),
                  _IN_HBM, _resident(rel_bias.shape), _IN_HBM],
        out_specs=_row_block(tq, d),
        out_shape=jax.ShapeDtypeStruct(x.shape, x.dtype),
        scratch_shapes=[pltpu.VMEM((d, width), jnp.bfloat16),
                        pltpu.VMEM(w_out.shape[1:], jnp.bfloat16),
                        pltpu.VMEM((WEIGHT_STAGE_SLOTS, d, WEIGHT_TILE_COLS), jnp.float32),
                        pltpu.SemaphoreType.DMA((WEIGHT_STAGE_SLOTS,)),
                        pltpu.VMEM((len(ATT_T_BLOCKS), width, d), jnp.bfloat16),
                        pltpu.VMEM((n_slots, tq, width), jnp.bfloat16),
                        pltpu.VMEM((n_slots, width, tq), jnp.bfloat16),
                        pltpu.VMEM((width, tq), jnp.bfloat16),
                        pltpu.VMEM((width, tq), jnp.float32),
                        pltpu.VMEM((tq, width), jnp.bfloat16),
                        pltpu.VMEM((n_heads, ATT_SPAN, LANES), jnp.float32),
                        pltpu.VMEM((ATT_STAGES, 2, 2, LANES, LANES), jnp.float32),
                        pltpu.VMEM((2, 2, ATT_SPAN, LANES), jnp.float32),
                        pltpu.VMEM((ATT_STAGES, 2, HIST + tq, tq), jnp.bfloat16)],
        compiler_params=_COMPILER_PARAMS,
        name="att_layer",
    )(x, norm_pre, norm_post, w_in, rel_bias, w_out)


def kernel(x, norm_pre, norm_post, pool_w_in, pool_w_group, pool_scale, pool_w_out,
           att_w_in, att_rel_bias, att_w_out):
    x = _pool_layer(x, 0, norm_pre, norm_post, pool_w_in, pool_w_group, pool_scale, pool_w_out)
    x = _att_layer(x, 1, norm_pre, norm_post, att_w_in, att_rel_bias, att_w_out)
    return x
```

```python
import functools

import jax
import jax.numpy as jnp
from jax import lax
from jax.experimental import pallas as pl
from jax.experimental.pallas import tpu as pltpu

RMS_EPS = 1e-6
LOG2_E = 1.4426950408889634
SOFTMAX_SUM_LIMIT = 2.0 ** 100
POOL_WINDOWS = (2, 4, 8, 16)
POOL_HALO = 16
CHUNK = 64
LEFT_CHUNKS = 8
HEAD_DIM = 64
MAX_REL = 256
HIST = LEFT_CHUNKS * CHUNK

LANES = 128
SUBLANES = 8
BF16_ROWS = 16
V7X_VMEM_BYTES = 64 * 1024 * 1024
MASK_PENALTY = -1e30
WEIGHT_TILE_COLS = 256
WEIGHT_STAGE_SLOTS = 4
POOL_BLOCK_ROWS = 256
POOL_SUB_ROWS = 256
ATT_BLOCK_ROWS = 256
ATT_T_BLOCKS = (0, 2, 3)
ATT_STAGES = 4
ATT_SPAN = HIST + LANES
REL_LANES = HIST + ATT_BLOCK_ROWS
VMEM_LIMIT_BYTES = V7X_VMEM_BYTES * 7 // 8


def _rms_norm(x, g):
    ms = jnp.mean(x * x, axis=-1, keepdims=True)
    return x * lax.rsqrt(ms + RMS_EPS) * g


class _WeightStream:
    def __init__(self, chunks, stage_ref, sem_ref):
        self.chunks, self.stage_ref, self.sem_ref = chunks, stage_ref, sem_ref
        self.n_slots = stage_ref.shape[0]
        self.depth = self.n_slots - 1
        self.taken = 0
        for i in range(min(self.depth, len(chunks))):
            self._copy(i).start()

    def _copy(self, i):
        src = self.chunks[i][0]
        rows, cols = src.shape
        slot = i % self.n_slots
        return pltpu.make_async_copy(src, self.stage_ref.at[slot, pl.ds(0, rows), pl.ds(0, cols)],
                                     self.sem_ref.at[slot])

    def take(self, n):
        for i in range(self.taken, self.taken + n):
            src, consume = self.chunks[i]
            if i + self.depth < len(self.chunks):
                self._copy(i + self.depth).start()
            self._copy(i).wait()
            rows, cols = src.shape
            consume(self.stage_ref[i % self.n_slots, 0:rows, 0:cols])
        self.taken += n


def _pool_layer_kernel(layer, x_ref, gpre_ref, gpost_ref, win_hbm, wg_hbm, scale_ref, wout_hbm,
                       o_ref, wa_ref, wz_ref, wg_ref, wout_ref, stage_ref, sem_ref, wmix_ref, carry_ref, act_ref):
    b = pl.program_id(0)
    j = pl.program_id(1)
    tm = x_ref.shape[1]
    n_sub, sub, width = act_ref.shape
    n_groups = len(POOL_WINDOWS)
    gw = width // n_groups

    first = (b == 0) & (j == 0)

    @pl.when(j == 0)
    def _():
        carry_ref[...] = jnp.zeros_like(carry_ref)

    def normed(sb):
        r0 = sb * sub
        hb = _rms_norm(x_ref[0, r0:r0 + sub, :], gpre_ref[layer:layer + 1, :]).astype(jnp.bfloat16)
        t = j * tm + r0 + lax.broadcasted_iota(jnp.int32, (sub, 1), 0)
        return hb, t

    def project(hb, gi):
        a = jnp.dot(hb, wmix_ref[:, gi * gw:(gi + 1) * gw], preferred_element_type=jnp.float32)
        z = jnp.dot(hb, wz_ref[:, gi * gw:(gi + 1) * gw], preferred_element_type=jnp.float32)
        return a, z

    def mix(sb, gi, a, z, t):
        w = POOL_WINDOWS[gi]
        cols = slice(gi * gw, (gi + 1) * gw)
        ext = jnp.concatenate([carry_ref[:, cols], a], axis=0)
        carry_ref[:, cols] = a[sub - POOL_HALO:, :]
        s = ext
        k = 1
        while k < w:
            s = s + pltpu.roll(s, k, axis=0)
            k *= 2
        cnt = jnp.minimum(t + 1, w).astype(jnp.float32)
        pooled = s[POOL_HALO:, :] / cnt
        m = (pooled - a) * scale_ref[:, cols]
        act_ref[sb, :, cols] = (m * (z * jax.nn.sigmoid(z))).astype(jnp.bfloat16)

    def finish(sb):
        r0 = sb * sub
        y = jnp.dot(act_ref[sb], wout_ref[...], preferred_element_type=jnp.float32)
        o_ref[0, r0:r0 + sub, :] = x_ref[0, r0:r0 + sub, :] + _rms_norm(y, gpost_ref[layer:layer + 1, :])

    @pl.when(first)
    def _():
        cw = stage_ref.shape[2]

        def cast_into(dst_ref, rows, c0):
            def consume(tile):
                dst_ref[rows, c0:c0 + cw] = tile.astype(jnp.bfloat16)
            return consume

        def mix_group(gi, last_consume):
            def consume(tile):
                last_consume(tile)
                cols = slice(gi * gw, (gi + 1) * gw)
                wmix_ref[:, cols] = jnp.dot(wa_ref[:, cols], wg_ref[gi],
                                            preferred_element_type=jnp.float32).astype(jnp.bfloat16)
            return consume

        d_in, d_grp = wa_ref.shape[0], wg_ref.shape[1]
        group_chunks = []
        for gi in range(n_groups):
            chunks = []
            for c0 in range(0, gw, cw):
                chunks.append((wg_hbm.at[0, gi, :, pl.ds(c0, cw)], cast_into(wg_ref.at[gi], slice(0, d_grp), c0)))
            for c0 in range(gi * gw, (gi + 1) * gw, cw):
                chunks.append((win_hbm.at[0, :, pl.ds(c0, cw)], cast_into(wa_ref, slice(0, d_in), c0)))
            chunks[-1] = (chunks[-1][0], mix_group(gi, chunks[-1][1]))
            for c0 in range(gi * gw, (gi + 1) * gw, cw):
                chunks.append((win_hbm.at[0, :, pl.ds(width + c0, cw)], cast_into(wz_ref, slice(0, d_in), c0)))
            group_chunks.append(chunks)
        out_chunks = []
        for r0 in range(0, width, d_in):
            for c0 in range(0, wout_ref.shape[1], cw):
                out_chunks.append((wout_hbm.at[0, pl.ds(r0, d_in), pl.ds(c0, cw)],
                                   cast_into(wout_ref, slice(r0, r0 + d_in), c0)))
        chunks = sum(group_chunks, []) + out_chunks
        _WeightStream(chunks, stage_ref, sem_ref).take(len(chunks))

    for sb in range(n_sub):
        hb, t = normed(sb)
        projected = project(hb, 0)
        for gi in range(n_groups):
            a, z = projected
            if gi + 1 < n_groups:
                projected = project(hb, gi + 1)
            mix(sb, gi, a, z, t)
        finish(sb)


def _build_bias_table(rb_ref, bias_ref, heads):
    n_heads, rows, cols = bias_ref.shape
    lanes = REL_LANES
    n_rel = 2 * MAX_REL
    u = lax.broadcasted_iota(jnp.int32, (1, lanes), 1)
    in_table = (u > cols) & (u < n_rel)
    jj = lax.broadcasted_iota(jnp.int32, (rows, cols), 0)
    c = lax.broadcasted_iota(jnp.int32, (rows, cols), 1)
    kc, qc = jj // CHUNK, c // CHUNK
    in_band = (kc >= qc) & (kc <= qc + LEFT_CHUNKS)
    for h in heads:
        far = rb_ref[0, h:h + 1, n_rel:n_rel + 1]
        row = jnp.concatenate([rb_ref[0, h:h + 1, 0:n_rel], jnp.zeros((1, lanes - n_rel), jnp.float32)], axis=1)
        g = jnp.where(in_table, row, far)
        t = pltpu.roll(jnp.broadcast_to(g, (rows, lanes)), 0, 1, stride=1, stride_axis=0)
        bias_ref[h] = jnp.where(in_band, t[:, 0:cols] * LOG2_E, -jnp.inf)


def _att_layer_kernel(layer, x_ref, gpre_ref, gpost_ref, win_hbm, rb_ref, wout_hbm,
                      o_ref, wk_ref, wout_ref, stage_ref, sem_ref, wt_ref, k_ref, vt_ref, qt_ref, gate_ref, gated_ref, bias_ref, own_ref, s_ref, pt_ref):
    b = pl.program_id(0)
    j = pl.program_id(1)
    tq = x_ref.shape[1]
    width = wout_ref.shape[0]
    n_heads = width // HEAD_DIM
    win = HIST + tq
    half = tq // 2
    span = bias_ref.shape[1]
    pair = 2 * HEAD_DIM
    bf16 = jnp.bfloat16

    @pl.when((b == 0) & (j == 0))
    def _():
        cw = stage_ref.shape[2]

        def cast_into(dst_ref, c0):
            def consume(tile):
                dst_ref[:, c0:c0 + cw] = tile.astype(bf16)
            return consume

        def transpose_into(t, c0, heads):
            def consume(tile):
                wt_ref[t, c0:c0 + cw, :] = tile.T.astype(bf16)
                _build_bias_table(rb_ref, bias_ref, heads)
            return consume

        t_tiles = [(t, blk, c0) for t, blk in enumerate(ATT_T_BLOCKS) for c0 in range(0, width, cw)]
        per_tile = -(-n_heads // len(t_tiles))
        chunks = []
        for i, (t, blk, c0) in enumerate(t_tiles):
            heads = range(min(i * per_tile, n_heads), min((i + 1) * per_tile, n_heads))
            chunks.append((win_hbm.at[0, :, pl.ds(blk * width + c0, cw)], transpose_into(t, c0, heads)))
        for c0 in range(0, width, cw):
            chunks.append((win_hbm.at[0, :, pl.ds(width + c0, cw)], cast_into(wk_ref, c0)))
        for c0 in range(0, wout_ref.shape[1], cw):
            chunks.append((wout_hbm.at[0, :, pl.ds(c0, cw)], cast_into(wout_ref, c0)))
        _WeightStream(chunks, stage_ref, sem_ref).take(len(chunks))

    n_slots = k_ref.shape[0]
    slots = [lax.rem(j + 1 + i, n_slots) for i in range(n_slots)]

    @pl.when(j == 0)
    def _():
        k_ref[...] = jnp.zeros(k_ref.shape, bf16)
        vt_ref[...] = jnp.zeros(vt_ref.shape, bf16)

    x = x_ref[0]
    h = _rms_norm(x, gpre_ref[layer:layer + 1, :])
    hb = h.astype(bf16)
    hbt = h.T.astype(bf16)
    k = jnp.dot(hb, wk_ref[...], preferred_element_type=jnp.float32)
    k_ref[slots[-1]] = k.astype(bf16)
    qt = jnp.dot(wt_ref[0], hbt, preferred_element_type=jnp.float32)
    qt_ref[...] = (qt * (HEAD_DIM ** -0.5 * LOG2_E)).astype(bf16)
    vt = jnp.dot(wt_ref[1], hbt, preferred_element_type=jnp.float32)
    vt_ref[slots[-1]] = vt.astype(bf16)
    zt = jnp.dot(wt_ref[2], hbt, preferred_element_type=jnp.float32)
    gate_ref[...] = zt * jax.nn.sigmoid(zt)

    row = lax.broadcasted_iota(jnp.int32, (win, pair), 0)
    lane = lax.broadcasted_iota(jnp.int32, (win, pair), 1)
    k_aux = jnp.where(lane == 0, jnp.where(row < HIST - j * tq, 1.0, 0.0),
                      jnp.where(lane == 1, 1.0, 0.0)).astype(bf16)
    r2 = lax.broadcasted_iota(jnp.int32, (pair, half), 0)
    zeros_q = jnp.zeros((HEAD_DIM, half), bf16)
    ones_v = jnp.ones((BF16_ROWS, win), bf16)
    n_tiles = win // half
    own = HIST // half

    def penalty(shift=None):
        sub = 0.0 if shift is None else jnp.where(r2 == 1, -shift, 0.0)
        return jnp.where(r2 == 0, MASK_PENALTY, sub).astype(bf16)

    def rhs_tile(pr, m, shifts):
        p0 = pr * pair
        cols = slice(m * half, (m + 1) * half)
        q_a = qt_ref[p0:p0 + HEAD_DIM, cols]
        q_b = qt_ref[p0 + HEAD_DIM:p0 + pair, cols]
        pen = [penalty(None if shifts is None else shifts[e][m]) for e in range(2)]
        return jnp.concatenate([jnp.concatenate([q_a, zeros_q, pen[0]], axis=0),
                                jnp.concatenate([zeros_q, q_b, pen[1]], axis=0)], axis=1)

    def lhs_tile(pr, r):
        s = slots[r // (tq // half)]
        r0 = (r % (tq // half)) * half
        return jnp.concatenate([k_ref[s, r0:r0 + half, pr * pair:(pr + 1) * pair],
                                k_aux[r * half:(r + 1) * half, :]], axis=1)

    def score_dot(lhs, rhs):
        if rhs.shape[1] > tq:
            return jnp.dot(lhs, rhs, preferred_element_type=jnp.float32)
        h0 = lhs.shape[0] // 2
        return jnp.concatenate([jnp.dot(lhs[0:h0, :], rhs, preferred_element_type=jnp.float32),
                                jnp.dot(lhs[h0:, :], rhs, preferred_element_type=jnp.float32)], axis=0)

    def visible(r, skip_own=False):
        return [m for m in range(2)
                if 0 <= r - m < span // half and not (skip_own and r == own + m)]

    n_stage = pt_ref.shape[0]
    for buf in range(n_stage):
        for e in range(2):
            pt_ref[buf, e, span:win, 0:half] = jnp.zeros((win - span, half), bf16)
            pt_ref[buf, e, 0:win - span, half:tq] = jnp.zeros((win - span, half), bf16)

    def attend(pr):
        buf = pr % n_stage
        gated_t, denoms = [], []
        for e in range(2):
            rows = slice((2 * pr + e) * HEAD_DIM, (2 * pr + e + 1) * HEAD_DIM)
            vt_win = jnp.concatenate([vt_ref[s, rows, :] for s in slots], axis=1)
            v_ext = jnp.concatenate([vt_win, ones_v], axis=0)
            o_ext = jnp.dot(v_ext, pt_ref[buf, e], preferred_element_type=jnp.float32)
            denom = o_ext[HEAD_DIM:HEAD_DIM + SUBLANES, :]
            o_t = o_ext[0:HEAD_DIM, :] * jnp.concatenate([1.0 / denom] * (HEAD_DIM // SUBLANES), axis=0)
            gated_t.append(o_t * gate_ref[rows, :])
            denoms.append(denom)
        gated_ref[:, pr * pair:(pr + 1) * pair] = jnp.concatenate(gated_t, axis=0).T.astype(bf16)
        return denoms

    def project_out():
        for r0 in range(0, tq, half):
            y = jnp.dot(gated_ref[r0:r0 + half, :], wout_ref[...], preferred_element_type=jnp.float32)
            o_ref[0, r0:r0 + half, :] = x[r0:r0 + half, :] + _rms_norm(y, gpost_ref[layer:layer + 1, :])

    def own_scores(pr):
        buf = pr % n_stage
        maxima = [[None, None], [None, None]]
        for m in range(2):
            st = score_dot(lhs_tile(pr, own + m), rhs_tile(pr, m, None))
            for e in range(2):
                s = st[:, e * half:(e + 1) * half] + bias_ref[2 * pr + e, HIST:HIST + half, :]
                own_ref[buf, e, m] = s
                maxima[e][m] = jnp.max(s, axis=0, keepdims=True)
        return maxima

    def fast_probs(pr, maxima):
        buf = pr % n_stage
        shifts = [[mx.astype(bf16).astype(jnp.float32) for mx in me] for me in maxima]
        rhs = jnp.concatenate([rhs_tile(pr, m, shifts) for m in range(2)], axis=1)
        groups = [[0], [1, 2, 3], [4]]
        for rows in groups:
            ms = visible(rows[0], skip_own=True)
            lhs = jnp.concatenate([lhs_tile(pr, r) for r in rows], axis=0)
            st_all = score_dot(lhs, rhs[:, ms[0] * tq:(ms[-1] + 1) * tq])
            for ri, r in enumerate(rows):
                st = st_all[ri * half:(ri + 1) * half, :]
                for mi, m in enumerate(ms):
                    for e in range(2):
                        kr = slice((r - m) * half, (r - m + 1) * half)
                        c0 = mi * tq + e * half
                        p = jnp.exp2(st[:, c0:c0 + half] + bias_ref[2 * pr + e, kr, :])
                        pt_ref[buf, e, r * half:(r + 1) * half, m * half:(m + 1) * half] = p.astype(bf16)
        for m in range(2):
            for e in range(2):
                p = jnp.exp2(own_ref[buf, e, m] - shifts[e][m])
                pt_ref[buf, e, (own + m) * half:(own + m + 1) * half,
                       m * half:(m + 1) * half] = p.astype(bf16)

    n_pairs = n_heads // 2
    overflow = jnp.zeros((SUBLANES, tq), jnp.float32)
    maxima = {pr: own_scores(pr) for pr in range(min(2, n_pairs))}
    fast_probs(0, maxima.pop(0))
    for pr in range(n_pairs):
        if pr + 2 < n_pairs:
            maxima[pr + 2] = own_scores(pr + 2)
        if pr + 1 < n_pairs:
            fast_probs(pr + 1, maxima.pop(pr + 1))
        for denom in attend(pr):
            overflow = jnp.maximum(overflow, jnp.where(denom < SOFTMAX_SUM_LIMIT, 0.0, 1.0))
    project_out()

    def exact_scores(pr):
        rhs = jnp.concatenate([rhs_tile(pr, m, None) for m in range(2)], axis=1)
        partial = [[None, None], [None, None]]
        for r in range(n_tiles):
            ms = visible(r)
            st = score_dot(lhs_tile(pr, r), rhs[:, ms[0] * tq:(ms[-1] + 1) * tq])
            for mi, m in enumerate(ms):
                for e in range(2):
                    kr = slice((r - m) * half, (r - m + 1) * half)
                    c0 = mi * tq + e * half
                    s = st[:, c0:c0 + half] + bias_ref[2 * pr + e, kr, :]
                    s_ref[e, m, kr, :] = s
                    for i in range(0, half, SUBLANES):
                        tile = s[i:i + SUBLANES, :]
                        prev = partial[e][m]
                        partial[e][m] = tile if prev is None else jnp.maximum(prev, tile)
        return [[jnp.max(p, axis=0, keepdims=True) for p in pe] for pe in partial]

    @pl.when(jnp.max(overflow) > 0.0)
    def _():
        for pr in range(n_pairs):
            exact_maxima = exact_scores(pr)
            for e in range(2):
                for m in range(2):
                    p = jnp.exp2(s_ref[e, m] - exact_maxima[e][m]).astype(bf16)
                    pt_ref[pr % n_stage, e, m * half:m * half + span, m * half:(m + 1) * half] = p
            attend(pr)
        project_out()


def _resident(shape):
    zeros = (0,) * len(shape)
    return pl.BlockSpec(shape, lambda b, j: zeros, pipeline_mode=pl.Buffered(1))


def _row_block(rows, d):
    return pl.BlockSpec((1, rows, d), lambda b, j: (b, j, 0))


_IN_HBM = pl.BlockSpec(memory_space=pl.ANY)


_COMPILER_PARAMS = pltpu.CompilerParams(
    dimension_semantics=("arbitrary", "arbitrary"), vmem_limit_bytes=VMEM_LIMIT_BYTES)


def _pool_layer(x, layer, norm_pre, norm_post, w_in, w_group, scale, w_out):
    b, s, d = x.shape
    width = w_out.shape[1]
    tm = POOL_BLOCK_ROWS
    return pl.pallas_call(
        functools.partial(_pool_layer_kernel, layer),
        grid=(b, s // tm),
        in_specs=[_row_block(tm, d), _resident(norm_pre.shape), _resident(norm_post.shape),
                  _IN_HBM, _IN_HBM, _resident(scale.shape), _IN_HBM],
        out_specs=_row_block(tm, d),
        out_shape=jax.ShapeDtypeStruct(x.shape, x.dtype),
        scratch_shapes=[pltpu.VMEM((d, width), jnp.bfloat16),
                        pltpu.VMEM((d, width), jnp.bfloat16),
                        pltpu.VMEM(w_group.shape[1:], jnp.bfloat16),
                        pltpu.VMEM(w_out.shape[1:], jnp.bfloat16),
                        pltpu.VMEM((WEIGHT_STAGE_SLOTS, d, WEIGHT_TILE_COLS), jnp.float32),
                        pltpu.SemaphoreType.DMA((WEIGHT_STAGE_SLOTS,)),
                        pltpu.VMEM((d, width), jnp.bfloat16),
                        pltpu.VMEM((POOL_HALO, width), jnp.float32),
                        pltpu.VMEM((tm // POOL_SUB_ROWS, POOL_SUB_ROWS, width), jnp.bfloat16)],
        compiler_params=_COMPILER_PARAMS,
        name="pool_layer",
    )(x, norm_pre, norm_post, w_in, w_group, scale, w_out)


def _att_layer(x, layer, norm_pre, norm_post, w_in, rel_bias, w_out):
    b, s, d = x.shape
    width = w_out.shape[1]
    n_heads = width // HEAD_DIM
    tq = ATT_BLOCK_ROWS
    n_slots = HIST // tq + 1
    return pl.pallas_call(
        functools.partial(_att_layer_kernel, layer),
        grid=(b, s // tq),
        in_specs=[_row_block(tq, d), _resident(norm_pre.shape), _resident(norm_post.shape),
                  _IN_HBM, _resident(rel_bias.shape), _IN_HBM],
        out_specs=_row_block(tq, d),
        out_shape=jax.ShapeDtypeStruct(x.shape, x.dtype),
        scratch_shapes=[pltpu.VMEM((d, width), jnp.bfloat16),
                        pltpu.VMEM(w_out.shape[1:], jnp.bfloat16),
                        pltpu.VMEM((WEIGHT_STAGE_SLOTS, d, WEIGHT_TILE_COLS), jnp.float32),
                        pltpu.SemaphoreType.DMA((WEIGHT_STAGE_SLOTS,)),
                        pltpu.VMEM((len(ATT_T_BLOCKS), width, d), jnp.bfloat16),
                        pltpu.VMEM((n_slots, tq, width), jnp.bfloat16),
                        pltpu.VMEM((n_slots, width, tq), jnp.bfloat16),
                        pltpu.VMEM((width, tq), jnp.bfloat16),
                        pltpu.VMEM((width, tq), jnp.float32),
                        pltpu.VMEM((tq, width), jnp.bfloat16),
                        pltpu.VMEM((n_heads, ATT_SPAN, LANES), jnp.float32),
                        pltpu.VMEM((ATT_STAGES, 2, 2, LANES, LANES), jnp.float32),
                        pltpu.VMEM((2, 2, ATT_SPAN, LANES), jnp.float32),
                        pltpu.VMEM((ATT_STAGES, 2, HIST + tq, tq), jnp.bfloat16)],
        compiler_params=_COMPILER_PARAMS,
        name="att_layer",
    )(x, norm_pre, norm_post, w_in, rel_bias, w_out)


def kernel(x, norm_pre, norm_post, pool_w_in, pool_w_group, pool_scale, pool_w_out,
           att_w_in, att_rel_bias, att_w_out):
    x = _pool_layer(x, 0, norm_pre, norm_post, pool_w_in, pool_w_group, pool_scale, pool_w_out)
    x = _att_layer(x, 1, norm_pre, norm_post, att_w_in, att_rel_bias, att_w_out)
    return x
```

```python
import functools

import jax
import jax.numpy as jnp
from jax import lax
from jax.experimental import pallas as pl
from jax.experimental.pallas import tpu as pltpu

RMS_EPS = 1e-6
LOG2_E = 1.4426950408889634
SOFTMAX_SUM_LIMIT = 2.0 ** 100
POOL_WINDOWS = (2, 4, 8, 16)
POOL_HALO = 16
CHUNK = 64
LEFT_CHUNKS = 8
HEAD_DIM = 64
MAX_REL = 256
HIST = LEFT_CHUNKS * CHUNK

LANES = 128
SUBLANES = 8
BF16_ROWS = 16
V7X_VMEM_BYTES = 64 * 1024 * 1024
MASK_PENALTY = -1e30
WEIGHT_TILE_COLS = 256
WEIGHT_STAGE_SLOTS = 6
POOL_BLOCK_ROWS = 512
POOL_SUB_ROWS = 256
ATT_BLOCK_ROWS = 256
ATT_T_BLOCKS = (0, 2, 3)
ATT_STAGES = 4
ATT_SPAN = HIST + LANES
REL_LANES = HIST + ATT_BLOCK_ROWS
VMEM_LIMIT_BYTES = V7X_VMEM_BYTES * 7 // 8


def _rms_norm(x, g):
    ms = jnp.mean(x * x, axis=-1, keepdims=True)
    return x * lax.rsqrt(ms + RMS_EPS) * g


class _WeightStream:
    def __init__(self, chunks, stage_ref, sem_ref):
        self.chunks, self.stage_ref, self.sem_ref = chunks, stage_ref, sem_ref
        self.n_slots = stage_ref.shape[0]
        self.depth = self.n_slots - 1
        self.taken = 0
        for i in range(min(self.depth, len(chunks))):
            self._copy(i).start(priority=i % 2)

    def _copy(self, i):
        src = self.chunks[i][0]
        rows, cols = src.shape
        slot = i % self.n_slots
        return pltpu.make_async_copy(src, self.stage_ref.at[slot, pl.ds(0, rows), pl.ds(0, cols)],
                                     self.sem_ref.at[slot])

    def take(self, n):
        for i in range(self.taken, self.taken + n):
            src, consume = self.chunks[i]
            if i + self.depth < len(self.chunks):
                self._copy(i + self.depth).start(priority=(i + self.depth) % 2)
            self._copy(i).wait()
            rows, cols = src.shape
            consume(self.stage_ref[i % self.n_slots, 0:rows, 0:cols])
        self.taken += n


def _pool_layer_kernel(layer, x_ref, gpre_ref, gpost_ref, win_hbm, wg_hbm, scale_ref, wout_hbm,
                       o_ref, wa_ref, wz_ref, wg_ref, wout_ref, stage_ref, sem_ref, wmix_ref, carry_ref, act_ref):
    b = pl.program_id(0)
    j = pl.program_id(1)
    tm = x_ref.shape[1]
    n_sub, sub, width = act_ref.shape
    n_groups = len(POOL_WINDOWS)
    gw = width // n_groups

    first = (b == 0) & (j == 0)

    @pl.when(j == 0)
    def _():
        carry_ref[...] = jnp.zeros_like(carry_ref)

    def normed(sb):
        r0 = sb * sub
        hb = _rms_norm(x_ref[0, r0:r0 + sub, :], gpre_ref[layer:layer + 1, :]).astype(jnp.bfloat16)
        t = j * tm + r0 + lax.broadcasted_iota(jnp.int32, (sub, 1), 0)
        return hb, t

    def project(hb, gi):
        a = jnp.dot(hb, wmix_ref[:, gi * gw:(gi + 1) * gw], preferred_element_type=jnp.float32)
        z = jnp.dot(hb, wz_ref[:, gi * gw:(gi + 1) * gw], preferred_element_type=jnp.float32)
        return a, z

    def mix(sb, gi, a, z, t):
        w = POOL_WINDOWS[gi]
        cols = slice(gi * gw, (gi + 1) * gw)
        ext = jnp.concatenate([carry_ref[:, cols], a], axis=0)
        carry_ref[:, cols] = a[sub - POOL_HALO:, :]
        s = ext
        k = 1
        while k < w:
            s = s + pltpu.roll(s, k, axis=0)
            k *= 2
        cnt = jnp.minimum(t + 1, w).astype(jnp.float32)
        pooled = s[POOL_HALO:, :] / cnt
        m = (pooled - a) * scale_ref[:, cols]
        act_ref[sb, :, cols] = (m * (z * jax.nn.sigmoid(z))).astype(jnp.bfloat16)

    def finish(sb):
        r0 = sb * sub
        y = jnp.dot(act_ref[sb], wout_ref[...], preferred_element_type=jnp.float32)
        o_ref[0, r0:r0 + sub, :] = x_ref[0, r0:r0 + sub, :] + _rms_norm(y, gpost_ref[layer:layer + 1, :])

    @pl.when(first)
    def _():
        cw = stage_ref.shape[2]

        def cast_into(dst_ref, rows, c0):
            def consume(tile):
                dst_ref[rows, c0:c0 + cw] = tile.astype(jnp.bfloat16)
            return consume

        def mix_group(gi, last_consume):
            def consume(tile):
                last_consume(tile)
                cols = slice(gi * gw, (gi + 1) * gw)
                wmix_ref[:, cols] = jnp.dot(wa_ref[:, cols], wg_ref[gi],
                                            preferred_element_type=jnp.float32).astype(jnp.bfloat16)
            return consume

        d_in, d_grp = wa_ref.shape[0], wg_ref.shape[1]
        group_chunks = []
        for gi in range(n_groups):
            chunks = []
            for c0 in range(0, gw, cw):
                chunks.append((wg_hbm.at[0, gi, :, pl.ds(c0, cw)], cast_into(wg_ref.at[gi], slice(0, d_grp), c0)))
            for c0 in range(gi * gw, (gi + 1) * gw, cw):
                chunks.append((win_hbm.at[0, :, pl.ds(c0, cw)], cast_into(wa_ref, slice(0, d_in), c0)))
            chunks[-1] = (chunks[-1][0], mix_group(gi, chunks[-1][1]))
            for c0 in range(gi * gw, (gi + 1) * gw, cw):
                chunks.append((win_hbm.at[0, :, pl.ds(width + c0, cw)], cast_into(wz_ref, slice(0, d_in), c0)))
            group_chunks.append(chunks)
        out_chunks = []
        for r0 in range(0, width, d_in):
            for c0 in range(0, wout_ref.shape[1], cw):
                out_chunks.append((wout_hbm.at[0, pl.ds(r0, d_in), pl.ds(c0, cw)],
                                   cast_into(wout_ref, slice(r0, r0 + d_in), c0)))
        chunks = sum(group_chunks, []) + out_chunks
        _WeightStream(chunks, stage_ref, sem_ref).take(len(chunks))

    for sb in range(n_sub):
        hb, t = normed(sb)
        projected = project(hb, 0)
        for gi in range(n_groups):
            a, z = projected
            if gi + 1 < n_groups:
                projected = project(hb, gi + 1)
            mix(sb, gi, a, z, t)
        finish(sb)


def _build_bias_table(rb_ref, bias_ref, heads):
    n_heads, rows, cols = bias_ref.shape
    lanes = REL_LANES
    n_rel = 2 * MAX_REL
    u = lax.broadcasted_iota(jnp.int32, (1, lanes), 1)
    in_table = (u > cols) & (u < n_rel)
    jj = lax.broadcasted_iota(jnp.int32, (rows, cols), 0)
    c = lax.broadcasted_iota(jnp.int32, (rows, cols), 1)
    kc, qc = jj // CHUNK, c // CHUNK
    in_band = (kc >= qc) & (kc <= qc + LEFT_CHUNKS)
    for h in heads:
        far = rb_ref[0, h:h + 1, n_rel:n_rel + 1]
        row = jnp.concatenate([rb_ref[0, h:h + 1, 0:n_rel], jnp.zeros((1, lanes - n_rel), jnp.float32)], axis=1)
        g = jnp.where(in_table, row, far)
        t = pltpu.roll(jnp.broadcast_to(g, (rows, lanes)), 0, 1, stride=1, stride_axis=0)
        bias_ref[h] = jnp.where(in_band, t[:, 0:cols] * LOG2_E, -jnp.inf)


def _att_layer_kernel(layer, x_ref, gpre_ref, gpost_ref, win_hbm, rb_ref, wout_hbm,
                      o_ref, wk_ref, wout_ref, stage_ref, sem_ref, wt_ref, k_ref, vt_ref, qt_ref, gate_ref, gated_ref, bias_ref, own_ref, s_ref, pt_ref):
    b = pl.program_id(0)
    j = pl.program_id(1)
    tq = x_ref.shape[1]
    width = wout_ref.shape[0]
    n_heads = width // HEAD_DIM
    win = HIST + tq
    half = tq // 2
    span = bias_ref.shape[1]
    pair = 2 * HEAD_DIM
    bf16 = jnp.bfloat16

    @pl.when((b == 0) & (j == 0))
    def _():
        cw = stage_ref.shape[2]

        def cast_into(dst_ref, c0):
            def consume(tile):
                dst_ref[:, c0:c0 + cw] = tile.astype(bf16)
            return consume

        def transpose_into(t, c0, heads):
            def consume(tile):
                wt_ref[t, c0:c0 + cw, :] = tile.T.astype(bf16)
                _build_bias_table(rb_ref, bias_ref, heads)
            return consume

        t_tiles = [(t, blk, c0) for t, blk in enumerate(ATT_T_BLOCKS) for c0 in range(0, width, cw)]
        per_tile = -(-n_heads // len(t_tiles))
        chunks = []
        for i, (t, blk, c0) in enumerate(t_tiles):
            heads = range(min(i * per_tile, n_heads), min((i + 1) * per_tile, n_heads))
            chunks.append((win_hbm.at[0, :, pl.ds(blk * width + c0, cw)], transpose_into(t, c0, heads)))
        for c0 in range(0, width, cw):
            chunks.append((win_hbm.at[0, :, pl.ds(width + c0, cw)], cast_into(wk_ref, c0)))
        for c0 in range(0, wout_ref.shape[1], cw):
            chunks.append((wout_hbm.at[0, :, pl.ds(c0, cw)], cast_into(wout_ref, c0)))
        _WeightStream(chunks, stage_ref, sem_ref).take(len(chunks))

    n_slots = k_ref.shape[0]
    slots = [lax.rem(j + 1 + i, n_slots) for i in range(n_slots)]

    @pl.when(j == 0)
    def _():
        k_ref[...] = jnp.zeros(k_ref.shape, bf16)
        vt_ref[...] = jnp.zeros(vt_ref.shape, bf16)

    x = x_ref[0]
    h = _rms_norm(x, gpre_ref[layer:layer + 1, :])
    hb = h.astype(bf16)
    hbt = h.T.astype(bf16)
    k = jnp.dot(hb, wk_ref[...], preferred_element_type=jnp.float32)
    k_ref[slots[-1]] = k.astype(bf16)
    qt = jnp.dot(wt_ref[0], hbt, preferred_element_type=jnp.float32)
    qt_ref[...] = (qt * (HEAD_DIM ** -0.5 * LOG2_E)).astype(bf16)
    vt = jnp.dot(wt_ref[1], hbt, preferred_element_type=jnp.float32)
    vt_ref[slots[-1]] = vt.astype(bf16)
    zt = jnp.dot(wt_ref[2], hbt, preferred_element_type=jnp.float32)
    gate_ref[...] = zt * jax.nn.sigmoid(zt)

    row = lax.broadcasted_iota(jnp.int32, (win, pair), 0)
    lane = lax.broadcasted_iota(jnp.int32, (win, pair), 1)
    k_aux = jnp.where(lane == 0, jnp.where(row < HIST - j * tq, 1.0, 0.0),
                      jnp.where(lane == 1, 1.0, 0.0)).astype(bf16)
    r2 = lax.broadcasted_iota(jnp.int32, (pair, half), 0)
    zeros_q = jnp.zeros((HEAD_DIM, half), bf16)
    ones_v = jnp.ones((BF16_ROWS, win), bf16)
    n_tiles = win // half
    own = HIST // half

    def penalty(shift=None):
        sub = 0.0 if shift is None else jnp.where(r2 == 1, -shift, 0.0)
        return jnp.where(r2 == 0, MASK_PENALTY, sub).astype(bf16)

    def rhs_tile(pr, m, shifts):
        p0 = pr * pair
        cols = slice(m * half, (m + 1) * half)
        q_a = qt_ref[p0:p0 + HEAD_DIM, cols]
        q_b = qt_ref[p0 + HEAD_DIM:p0 + pair, cols]
        pen = [penalty(None if shifts is None else shifts[e][m]) for e in range(2)]
        return jnp.concatenate([jnp.concatenate([q_a, zeros_q, pen[0]], axis=0),
                                jnp.concatenate([zeros_q, q_b, pen[1]], axis=0)], axis=1)

    def lhs_tile(pr, r):
        s = slots[r // (tq // half)]
        r0 = (r % (tq // half)) * half
        return jnp.concatenate([k_ref[s, r0:r0 + half, pr * pair:(pr + 1) * pair],
                                k_aux[r * half:(r + 1) * half, :]], axis=1)

    def score_dot(lhs, rhs):
        if rhs.shape[1] > tq:
            return jnp.dot(lhs, rhs, preferred_element_type=jnp.float32)
        h0 = lhs.shape[0] // 2
        return jnp.concatenate([jnp.dot(lhs[0:h0, :], rhs, preferred_element_type=jnp.float32),
                                jnp.dot(lhs[h0:, :], rhs, preferred_element_type=jnp.float32)], axis=0)

    def visible(r, skip_own=False):
        return [m for m in range(2)
                if 0 <= r - m < span // half and not (skip_own and r == own + m)]

    n_stage = pt_ref.shape[0]
    for buf in range(n_stage):
        for e in range(2):
            pt_ref[buf, e, span:win, 0:half] = jnp.zeros((win - span, half), bf16)
            pt_ref[buf, e, 0:win - span, half:tq] = jnp.zeros((win - span, half), bf16)

    def attend(pr):
        buf = pr % n_stage
        gated_t, denoms = [], []
        for e in range(2):
            rows = slice((2 * pr + e) * HEAD_DIM, (2 * pr + e + 1) * HEAD_DIM)
            vt_win = jnp.concatenate([vt_ref[s, rows, :] for s in slots], axis=1)
            v_ext = jnp.concatenate([vt_win, ones_v], axis=0)
            o_ext = jnp.dot(v_ext, pt_ref[buf, e], preferred_element_type=jnp.float32)
            denom = o_ext[HEAD_DIM:HEAD_DIM + SUBLANES, :]
            o_t = o_ext[0:HEAD_DIM, :] * jnp.concatenate([1.0 / denom] * (HEAD_DIM // SUBLANES), axis=0)
            gated_t.append(o_t * gate_ref[rows, :])
            denoms.append(denom)
        gated_ref[:, pr * pair:(pr + 1) * pair] = jnp.concatenate(gated_t, axis=0).T.astype(bf16)
        return denoms

    def project_out():
        for r0 in range(0, tq, half):
            y = jnp.dot(gated_ref[r0:r0 + half, :], wout_ref[...], preferred_element_type=jnp.float32)
            o_ref[0, r0:r0 + half, :] = x[r0:r0 + half, :] + _rms_norm(y, gpost_ref[layer:layer + 1, :])

    def own_scores(pr):
        buf = pr % n_stage
        maxima = [[None, None], [None, None]]
        for m in range(2):
            st = score_dot(lhs_tile(pr, own + m), rhs_tile(pr, m, None))
            for e in range(2):
                s = st[:, e * half:(e + 1) * half] + bias_ref[2 * pr + e, HIST:HIST + half, :]
                own_ref[buf, e, m] = s
                maxima[e][m] = jnp.max(s, axis=0, keepdims=True)
        return maxima

    def fast_probs(pr, maxima):
        buf = pr % n_stage
        shifts = [[mx.astype(bf16).astype(jnp.float32) for mx in me] for me in maxima]
        rhs = jnp.concatenate([rhs_tile(pr, m, shifts) for m in range(2)], axis=1)
        groups = [[0], [1, 2, 3], [4]]
        for rows in groups:
            ms = visible(rows[0], skip_own=True)
            lhs = jnp.concatenate([lhs_tile(pr, r) for r in rows], axis=0)
            st_all = score_dot(lhs, rhs[:, ms[0] * tq:(ms[-1] + 1) * tq])
            for ri, r in enumerate(rows):
                st = st_all[ri * half:(ri + 1) * half, :]
                for mi, m in enumerate(ms):
                    for e in range(2):
                        kr = slice((r - m) * half, (r - m + 1) * half)
                        c0 = mi * tq + e * half
                        p = jnp.exp2(st[:, c0:c0 + half] + bias_ref[2 * pr + e, kr, :])
                        pt_ref[buf, e, r * half:(r + 1) * half, m * half:(m + 1) * half] = p.astype(bf16)
        for m in range(2):
            for e in range(2):
                p = jnp.exp2(own_ref[buf, e, m] - shifts[e][m])
                pt_ref[buf, e, (own + m) * half:(own + m + 1) * half,
                       m * half:(m + 1) * half] = p.astype(bf16)

    n_pairs = n_heads // 2
    overflow = jnp.zeros((SUBLANES, tq), jnp.float32)
    maxima = {pr: own_scores(pr) for pr in range(min(2, n_pairs))}
    fast_probs(0, maxima.pop(0))
    for pr in range(n_pairs):
        if pr + 2 < n_pairs:
            maxima[pr + 2] = own_scores(pr + 2)
        if pr + 1 < n_pairs:
            fast_probs(pr + 1, maxima.pop(pr + 1))
        for denom in attend(pr):
            overflow = jnp.maximum(overflow, jnp.where(denom < SOFTMAX_SUM_LIMIT, 0.0, 1.0))
    project_out()

    def exact_scores(pr):
        rhs = jnp.concatenate([rhs_tile(pr, m, None) for m in range(2)], axis=1)
        partial = [[None, None], [None, None]]
        for r in range(n_tiles):
            ms = visible(r)
            st = score_dot(lhs_tile(pr, r), rhs[:, ms[0] * tq:(ms[-1] + 1) * tq])
            for mi, m in enumerate(ms):
                for e in range(2):
                    kr = slice((r - m) * half, (r - m + 1) * half)
                    c0 = mi * tq + e * half
                    s = st[:, c0:c0 + half] + bias_ref[2 * pr + e, kr, :]
                    s_ref[e, m, kr, :] = s
                    for i in range(0, half, SUBLANES):
                        tile = s[i:i + SUBLANES, :]
                        prev = partial[e][m]
                        partial[e][m] = tile if prev is None else jnp.maximum(prev, tile)
        return [[jnp.max(p, axis=0, keepdims=True) for p in pe] for pe in partial]

    @pl.when(jnp.max(overflow) > 0.0)
    def _():
        for pr in range(n_pairs):
            exact_maxima = exact_scores(pr)
            for e in range(2):
                for m in range(2):
                    p = jnp.exp2(s_ref[e, m] - exact_maxima[e][m]).astype(bf16)
                    pt_ref[pr % n_stage, e, m * half:m * half + span, m * half:(m + 1) * half] = p
            attend(pr)
        project_out()


def _resident(shape):
    zeros = (0,) * len(shape)
    return pl.BlockSpec(shape, lambda b, j: zeros, pipeline_mode=pl.Buffered(1))


def _row_block(rows, d):
    return pl.BlockSpec((1, rows, d), lambda b, j: (b, j, 0))


_IN_HBM = pl.BlockSpec(memory_space=pl.ANY)


_COMPILER_PARAMS = pltpu.CompilerParams(
    dimension_semantics=("arbitrary", "arbitrary"), vmem_limit_bytes=VMEM_LIMIT_BYTES)


def _pool_layer(x, layer, norm_pre, norm_post, w_in, w_group, scale, w_out):
    b, s, d = x.shape
    width = w_out.shape[1]
    tm = POOL_BLOCK_ROWS
    return pl.pallas_call(
        functools.partial(_pool_layer_kernel, layer),
        grid=(b, s // tm),
        in_specs=[_row_block(tm, d), _resident(norm_pre.shape), _resident(norm_post.shape),
                  _IN_HBM, _IN_HBM, _resident(scale.shape), _IN_HBM],
        out_specs=_row_block(tm, d),
        out_shape=jax.ShapeDtypeStruct(x.shape, x.dtype),
        scratch_shapes=[pltpu.VMEM((d, width), jnp.bfloat16),
                        pltpu.VMEM((d, width), jnp.bfloat16),
                        pltpu.VMEM(w_group.shape[1:], jnp.bfloat16),
                        pltpu.VMEM(w_out.shape[1:], jnp.bfloat16),
                        pltpu.VMEM((WEIGHT_STAGE_SLOTS, d, WEIGHT_TILE_COLS), jnp.float32),
                        pltpu.SemaphoreType.DMA((WEIGHT_STAGE_SLOTS,)),
                        pltpu.VMEM((d, width), jnp.bfloat16),
                        pltpu.VMEM((POOL_HALO, width), jnp.float32),
                        pltpu.VMEM((tm // POOL_SUB_ROWS, POOL_SUB_ROWS, width), jnp.bfloat16)],
        compiler_params=_COMPILER_PARAMS,
        name="pool_layer",
    )(x, norm_pre, norm_post, w_in, w_group, scale, w_out)


def _att_layer(x, layer, norm_pre, norm_post, w_in, rel_bias, w_out):
    b, s, d = x.shape
    width = w_out.shape[1]
    n_heads = width // HEAD_DIM
    tq = ATT_BLOCK_ROWS
    n_slots = HIST // tq + 1
    return pl.pallas_call(
        functools.partial(_att_layer_kernel, layer),
        grid=(b, s // tq),
        in_specs=[_row_block(tq, d), _resident(norm_pre.shape), _resident(norm_post.shape),
                  _IN_HBM, _resident(rel_bias.shape), _IN_HBM],
        out_specs=_row_block(tq, d),
        out_shape=jax.ShapeDtypeStruct(x.shape, x.dtype),
        scratch_shapes=[pltpu.VMEM((d, width), jnp.bfloat16),
                        pltpu.VMEM(w_out.shape[1:], jnp.bfloat16),
                        pltpu.VMEM((WEIGHT_STAGE_SLOTS, d, WEIGHT_TILE_COLS), jnp.float32),
                        pltpu.SemaphoreType.DMA((WEIGHT_STAGE_SLOTS,)),
                        pltpu.VMEM((len(ATT_T_BLOCKS), width, d), jnp.bfloat16),
                        pltpu.VMEM((n_slots, tq, width), jnp.bfloat16),
                        pltpu.VMEM((n_slots, width, tq), jnp.bfloat16),
                        pltpu.VMEM((width, tq), jnp.bfloat16),
                        pltpu.VMEM((width, tq), jnp.float32),
                        pltpu.VMEM((tq, width), jnp.bfloat16),
                        pltpu.VMEM((n_heads, ATT_SPAN, LANES), jnp.float32),
                        pltpu.VMEM((ATT_STAGES, 2, 2, LANES, LANES), jnp.float32),
                        pltpu.VMEM((2, 2, ATT_SPAN, LANES), jnp.float32),
                        pltpu.VMEM((ATT_STAGES, 2, HIST + tq, tq), jnp.bfloat16)],
        compiler_params=_COMPILER_PARAMS,
        name="att_layer",
    )(x, norm_pre, norm_post, w_in, rel_bias, w_out)


def kernel(x, norm_pre, norm_post, pool_w_in, pool_w_group, pool_scale, pool_w_out,
           att_w_in, att_rel_bias, att_w_out):
    x = _pool_layer(x, 0, norm_pre, norm_post, pool_w_in, pool_w_group, pool_scale, pool_w_out)
    x = _att_layer(x, 1, norm_pre, norm_post, att_w_in, att_rel_bias, att_w_out)
    return x
```

```python
import functools

import jax
import jax.numpy as jnp
from jax import lax
from jax.experimental import pallas as pl
from jax.experimental.pallas import tpu as pltpu

RMS_EPS = 1e-6
LOG2_E = 1.4426950408889634
SOFTMAX_SUM_LIMIT = 2.0 ** 100
POOL_WINDOWS = (2, 4, 8, 16)
POOL_HALO = 16
CHUNK = 64
LEFT_CHUNKS = 8
HEAD_DIM = 64
MAX_REL = 256
HIST = LEFT_CHUNKS * CHUNK

LANES = 128
SUBLANES = 8
BF16_ROWS = 16
V7X_VMEM_BYTES = 64 * 1024 * 1024
MASK_PENALTY = -1e30
WEIGHT_TILE_COLS = 256
WEIGHT_STAGE_SLOTS = 6
POOL_BLOCK_ROWS = 512
POOL_SUB_ROWS = 256
ATT_BLOCK_ROWS = 256
ATT_T_BLOCKS = (0, 2, 3)
ATT_STAGES = 4
ATT_SPAN = HIST + LANES
REL_LANES = HIST + ATT_BLOCK_ROWS
VMEM_LIMIT_BYTES = V7X_VMEM_BYTES * 7 // 8


def _rms_norm(x, g):
    ms = jnp.mean(x * x, axis=-1, keepdims=True)
    return x * lax.rsqrt(ms + RMS_EPS) * g


class _WeightStream:
    def __init__(self, chunks, stage_ref, sem_ref):
        self.chunks, self.stage_ref, self.sem_ref = chunks, stage_ref, sem_ref
        self.n_slots = stage_ref.shape[0]
        self.depth = self.n_slots - 1
        self.taken = 0
        for i in range(min(self.depth, len(chunks))):
            self._copy(i).start(priority=i % 2)

    def _copy(self, i):
        src = self.chunks[i][0]
        rows, cols = src.shape
        slot = i % self.n_slots
        return pltpu.make_async_copy(src, self.stage_ref.at[slot, pl.ds(0, rows), pl.ds(0, cols)],
                                     self.sem_ref.at[slot])

    def take(self, n):
        for i in range(self.taken, self.taken + n):
            src, consume = self.chunks[i]
            if i + self.depth < len(self.chunks):
                self._copy(i + self.depth).start(priority=(i + self.depth) % 2)
            self._copy(i).wait()
            rows, cols = src.shape
            consume(self.stage_ref[i % self.n_slots, 0:rows, 0:cols])
        self.taken += n


def _pool_layer_kernel(layer, x_ref, gpre_ref, gpost_ref, win_hbm, wg_hbm, scale_ref, wout_hbm,
                       o_ref, wa_ref, wz_ref, wg_ref, wout_ref, stage_ref, sem_ref, wmix_ref, carry_ref, act_ref):
    b = pl.program_id(0)
    j = pl.program_id(1)
    tm = x_ref.shape[1]
    n_sub, sub, width = act_ref.shape
    n_groups = len(POOL_WINDOWS)
    gw = width // n_groups

    first = (b == 0) & (j == 0)

    @pl.when(j == 0)
    def _():
        carry_ref[...] = jnp.zeros_like(carry_ref)

    def normed(sb):
        r0 = sb * sub
        hb = _rms_norm(x_ref[0, r0:r0 + sub, :], gpre_ref[layer:layer + 1, :]).astype(jnp.bfloat16)
        t = j * tm + r0 + lax.broadcasted_iota(jnp.int32, (sub, 1), 0)
        return hb, t

    def project(hb, gi):
        a = jnp.dot(hb, wmix_ref[:, gi * gw:(gi + 1) * gw], preferred_element_type=jnp.float32)
        z = jnp.dot(hb, wz_ref[:, gi * gw:(gi + 1) * gw], preferred_element_type=jnp.float32)
        return a, z

    def mix(sb, gi, a, z, t):
        w = POOL_WINDOWS[gi]
        cols = slice(gi * gw, (gi + 1) * gw)
        ext = jnp.concatenate([carry_ref[:, cols], a], axis=0)
        carry_ref[:, cols] = a[sub - POOL_HALO:, :]
        s = ext
        k = 1
        while k < w:
            s = s + pltpu.roll(s, k, axis=0)
            k *= 2
        cnt = jnp.minimum(t + 1, w).astype(jnp.float32)
        pooled = s[POOL_HALO:, :] / cnt
        m = (pooled - a) * scale_ref[:, cols]
        act_ref[sb, :, cols] = (m * (z * jax.nn.sigmoid(z))).astype(jnp.bfloat16)

    def finish(sb):
        r0 = sb * sub
        y = jnp.dot(act_ref[sb], wout_ref[...], preferred_element_type=jnp.float32)
        o_ref[0, r0:r0 + sub, :] = x_ref[0, r0:r0 + sub, :] + _rms_norm(y, gpost_ref[layer:layer + 1, :])

    @pl.when(first)
    def _():
        cw = stage_ref.shape[2]

        def cast_into(dst_ref, rows, c0):
            def consume(tile):
                dst_ref[rows, c0:c0 + cw] = tile.astype(jnp.bfloat16)
            return consume

        def mix_group(gi, last_consume):
            def consume(tile):
                last_consume(tile)
                cols = slice(gi * gw, (gi + 1) * gw)
                wmix_ref[:, cols] = jnp.dot(wa_ref[:, cols], wg_ref[gi],
                                            preferred_element_type=jnp.float32).astype(jnp.bfloat16)
            return consume

        d_in, d_grp = wa_ref.shape[0], wg_ref.shape[1]
        group_chunks = []
        for gi in range(n_groups):
            chunks = []
            for c0 in range(0, gw, cw):
                chunks.append((wg_hbm.at[0, gi, :, pl.ds(c0, cw)], cast_into(wg_ref.at[gi], slice(0, d_grp), c0)))
            for c0 in range(gi * gw, (gi + 1) * gw, cw):
                chunks.append((win_hbm.at[0, :, pl.ds(c0, cw)], cast_into(wa_ref, slice(0, d_in), c0)))
            chunks[-1] = (chunks[-1][0], mix_group(gi, chunks[-1][1]))
            for c0 in range(gi * gw, (gi + 1) * gw, cw):
                chunks.append((win_hbm.at[0, :, pl.ds(width + c0, cw)], cast_into(wz_ref, slice(0, d_in), c0)))
            group_chunks.append(chunks)
        out_chunks = []
        for r0 in range(0, width, d_in):
            for c0 in range(0, wout_ref.shape[1], cw):
                out_chunks.append((wout_hbm.at[0, pl.ds(r0, d_in), pl.ds(c0, cw)],
                                   cast_into(wout_ref, slice(r0, r0 + d_in), c0)))
        chunks = sum(group_chunks, []) + out_chunks
        _WeightStream(chunks, stage_ref, sem_ref).take(len(chunks))

    for sb in range(n_sub):
        hb, t = normed(sb)
        projected = project(hb, 0)
        for gi in range(n_groups):
            a, z = projected
            if gi + 1 < n_groups:
                projected = project(hb, gi + 1)
            mix(sb, gi, a, z, t)
        finish(sb)


def _build_bias_table(rb_ref, bias_ref, heads):
    n_heads, rows, cols = bias_ref.shape
    lanes = REL_LANES
    n_rel = 2 * MAX_REL
    u = lax.broadcasted_iota(jnp.int32, (1, lanes), 1)
    in_table = (u > cols) & (u < n_rel)
    jj = lax.broadcasted_iota(jnp.int32, (rows, cols), 0)
    c = lax.broadcasted_iota(jnp.int32, (rows, cols), 1)
    kc, qc = jj // CHUNK, c // CHUNK
    in_band = (kc >= qc) & (kc <= qc + LEFT_CHUNKS)
    for h in heads:
        far = rb_ref[0, h:h + 1, n_rel:n_rel + 1]
        row = jnp.concatenate([rb_ref[0, h:h + 1, 0:n_rel], jnp.zeros((1, lanes - n_rel), jnp.float32)], axis=1)
        g = jnp.where(in_table, row, far)
        t = pltpu.roll(jnp.broadcast_to(g, (rows, lanes)), 0, 1, stride=1, stride_axis=0)
        bias_ref[h] = jnp.where(in_band, t[:, 0:cols] * LOG2_E, -jnp.inf)


def _att_layer_kernel(layer, x_ref, gpre_ref, gpost_ref, win_hbm, rb_ref, wout_hbm,
                      o_ref, wk_ref, wout_ref, stage_ref, sem_ref, wt_ref, k_ref, vt_ref, qt_ref, gate_ref, gated_ref, bias_ref, own_ref, s_ref, pt_ref):
    b = pl.program_id(0)
    j = pl.program_id(1)
    tq = x_ref.shape[1]
    width = wout_ref.shape[0]
    n_heads = width // HEAD_DIM
    win = HIST + tq
    half = tq // 2
    span = bias_ref.shape[1]
    pair = 2 * HEAD_DIM
    bf16 = jnp.bfloat16

    @pl.when((b == 0) & (j == 0))
    def _():
        cw = stage_ref.shape[2]

        def cast_into(dst_ref, c0):
            def consume(tile):
                dst_ref[:, c0:c0 + cw] = tile.astype(bf16)
            return consume

        def transpose_into(t, c0, heads):
            def consume(tile):
                wt_ref[t, c0:c0 + cw, :] = tile.T.astype(bf16)
                _build_bias_table(rb_ref, bias_ref, heads)
            return consume

        t_tiles = [(t, blk, c0) for t, blk in enumerate(ATT_T_BLOCKS) for c0 in range(0, width, cw)]
        per_tile = -(-n_heads // len(t_tiles))
        chunks = []
        for i, (t, blk, c0) in enumerate(t_tiles):
            heads = range(min(i * per_tile, n_heads), min((i + 1) * per_tile, n_heads))
            chunks.append((win_hbm.at[0, :, pl.ds(blk * width + c0, cw)], transpose_into(t, c0, heads)))
        for c0 in range(0, width, cw):
            chunks.append((win_hbm.at[0, :, pl.ds(width + c0, cw)], cast_into(wk_ref, c0)))
        for c0 in range(0, wout_ref.shape[1], cw):
            chunks.append((wout_hbm.at[0, :, pl.ds(c0, cw)], cast_into(wout_ref, c0)))
        _WeightStream(chunks, stage_ref, sem_ref).take(len(chunks))

        for buf in range(pt_ref.shape[0]):
            for e in range(2):
                pt_ref[buf, e, span:win, 0:half] = jnp.zeros((win - span, half), bf16)
                pt_ref[buf, e, 0:win - span, half:tq] = jnp.zeros((win - span, half), bf16)

    n_slots = k_ref.shape[0]
    slots = [lax.rem(j + 1 + i, n_slots) for i in range(n_slots)]

    @pl.when(j == 0)
    def _():
        k_ref[...] = jnp.zeros(k_ref.shape, bf16)
        vt_ref[...] = jnp.zeros(vt_ref.shape, bf16)

    h = _rms_norm(x_ref[0], gpre_ref[layer:layer + 1, :])
    hb = h.astype(bf16)
    hbt = h.T.astype(bf16)
    k = jnp.dot(hb, wk_ref[...], preferred_element_type=jnp.float32)
    k_ref[slots[-1]] = k.astype(bf16)
    qt = jnp.dot(wt_ref[0], hbt, preferred_element_type=jnp.float32)
    qt_ref[...] = (qt * (HEAD_DIM ** -0.5 * LOG2_E)).astype(bf16)
    vt = jnp.dot(wt_ref[1], hbt, preferred_element_type=jnp.float32)
    vt_ref[slots[-1]] = vt.astype(bf16)
    zt = jnp.dot(wt_ref[2], hbt, preferred_element_type=jnp.float32)
    gate_ref[...] = zt * jax.nn.sigmoid(zt)

    row = lax.broadcasted_iota(jnp.int32, (win, pair), 0)
    lane = lax.broadcasted_iota(jnp.int32, (win, pair), 1)
    k_aux = jnp.where(lane == 0, jnp.where(row < HIST - j * tq, 1.0, 0.0),
                      jnp.where(lane == 1, 1.0, 0.0)).astype(bf16)
    r2 = lax.broadcasted_iota(jnp.int32, (pair, half), 0)
    zeros_q = jnp.zeros((HEAD_DIM, half), bf16)
    ones_v = jnp.ones((BF16_ROWS, win), bf16)
    n_tiles = win // half
    own = HIST // half

    def penalty(shift=None):
        sub = 0.0 if shift is None else jnp.where(r2 == 1, -shift, 0.0)
        return jnp.where(r2 == 0, MASK_PENALTY, sub).astype(bf16)

    def rhs_tile(pr, m, shifts):
        p0 = pr * pair
        cols = slice(m * half, (m + 1) * half)
        q_a = qt_ref[p0:p0 + HEAD_DIM, cols]
        q_b = qt_ref[p0 + HEAD_DIM:p0 + pair, cols]
        pen = [penalty(None if shifts is None else shifts[e][m]) for e in range(2)]
        return jnp.concatenate([jnp.concatenate([q_a, zeros_q, pen[0]], axis=0),
                                jnp.concatenate([zeros_q, q_b, pen[1]], axis=0)], axis=1)

    def lhs_tile(pr, r):
        s = slots[r // (tq // half)]
        r0 = (r % (tq // half)) * half
        return jnp.concatenate([k_ref[s, r0:r0 + half, pr * pair:(pr + 1) * pair],
                                k_aux[r * half:(r + 1) * half, :]], axis=1)

    def score_dot(lhs, rhs):
        if rhs.shape[1] > tq:
            return jnp.dot(lhs, rhs, preferred_element_type=jnp.float32)
        h0 = lhs.shape[0] // 2
        return jnp.concatenate([jnp.dot(lhs[0:h0, :], rhs, preferred_element_type=jnp.float32),
                                jnp.dot(lhs[h0:, :], rhs, preferred_element_type=jnp.float32)], axis=0)

    def visible(r, skip_own=False):
        return [m for m in range(2)
                if 0 <= r - m < span // half and not (skip_own and r == own + m)]

    n_stage = pt_ref.shape[0]

    def attend(pr):
        buf = pr % n_stage
        gated_t, denoms = [], []
        for e in range(2):
            rows = slice((2 * pr + e) * HEAD_DIM, (2 * pr + e + 1) * HEAD_DIM)
            vt_win = jnp.concatenate([vt_ref[s, rows, :] for s in slots], axis=1)
            v_ext = jnp.concatenate([vt_win, ones_v], axis=0)
            o_ext = jnp.dot(v_ext, pt_ref[buf, e], preferred_element_type=jnp.float32)
            denom = o_ext[HEAD_DIM:HEAD_DIM + SUBLANES, :]
            o_t = o_ext[0:HEAD_DIM, :] * jnp.concatenate([1.0 / denom] * (HEAD_DIM // SUBLANES), axis=0)
            gated_t.append(o_t * gate_ref[rows, :])
            denoms.append(denom)
        gated_ref[:, pr * pair:(pr + 1) * pair] = jnp.concatenate(gated_t, axis=0).T.astype(bf16)
        return denoms

    def project_out():
        for r0 in range(0, tq, half):
            y = jnp.dot(gated_ref[r0:r0 + half, :], wout_ref[...], preferred_element_type=jnp.float32)
            o_ref[0, r0:r0 + half, :] = x_ref[0, r0:r0 + half, :] + _rms_norm(y, gpost_ref[layer:layer + 1, :])

    def own_scores(pr):
        buf = pr % n_stage
        maxima = [[None, None], [None, None]]
        for m in range(2):
            st = score_dot(lhs_tile(pr, own + m), rhs_tile(pr, m, None))
            for e in range(2):
                s = st[:, e * half:(e + 1) * half] + bias_ref[2 * pr + e, HIST:HIST + half, :]
                own_ref[buf, e, m] = s
                maxima[e][m] = jnp.max(s, axis=0, keepdims=True)
        return maxima

    def fast_probs(pr, maxima):
        buf = pr % n_stage
        shifts = [[mx.astype(bf16).astype(jnp.float32) for mx in me] for me in maxima]
        rhs = jnp.concatenate([rhs_tile(pr, m, shifts) for m in range(2)], axis=1)
        groups = [[0], [1, 2, 3], [4]]
        for rows in groups:
            ms = visible(rows[0], skip_own=True)
            lhs = jnp.concatenate([lhs_tile(pr, r) for r in rows], axis=0)
            st_all = score_dot(lhs, rhs[:, ms[0] * tq:(ms[-1] + 1) * tq])
            for ri, r in enumerate(rows):
                st = st_all[ri * half:(ri + 1) * half, :]
                for mi, m in enumerate(ms):
                    for e in range(2):
                        kr = slice((r - m) * half, (r - m + 1) * half)
                        c0 = mi * tq + e * half
                        p = jnp.exp2(st[:, c0:c0 + half] + bias_ref[2 * pr + e, kr, :])
                        pt_ref[buf, e, r * half:(r + 1) * half, m * half:(m + 1) * half] = p.astype(bf16)
        for m in range(2):
            for e in range(2):
                p = jnp.exp2(own_ref[buf, e, m] - shifts[e][m])
                pt_ref[buf, e, (own + m) * half:(own + m + 1) * half,
                       m * half:(m + 1) * half] = p.astype(bf16)

    n_pairs = n_heads // 2
    overflow = jnp.zeros((SUBLANES, tq), jnp.float32)
    maxima = {pr: own_scores(pr) for pr in range(min(2, n_pairs))}
    fast_probs(0, maxima.pop(0))
    for pr in range(n_pairs):
        if pr + 2 < n_pairs:
            maxima[pr + 2] = own_scores(pr + 2)
        if pr + 1 < n_pairs:
            fast_probs(pr + 1, maxima.pop(pr + 1))
        for denom in attend(pr):
            overflow = jnp.maximum(overflow, jnp.where(denom < SOFTMAX_SUM_LIMIT, 0.0, 1.0))
    project_out()

    def exact_scores(pr):
        rhs = jnp.concatenate([rhs_tile(pr, m, None) for m in range(2)], axis=1)
        partial = [[None, None], [None, None]]
        for r in range(n_tiles):
            ms = visible(r)
            st = score_dot(lhs_tile(pr, r), rhs[:, ms[0] * tq:(ms[-1] + 1) * tq])
            for mi, m in enumerate(ms):
                for e in range(2):
                    kr = slice((r - m) * half, (r - m + 1) * half)
                    c0 = mi * tq + e * half
                    s = st[:, c0:c0 + half] + bias_ref[2 * pr + e, kr, :]
                    s_ref[e, m, kr, :] = s
                    for i in range(0, half, SUBLANES):
                        tile = s[i:i + SUBLANES, :]
                        prev = partial[e][m]
                        partial[e][m] = tile if prev is None else jnp.maximum(prev, tile)
        return [[jnp.max(p, axis=0, keepdims=True) for p in pe] for pe in partial]

    @pl.when(jnp.max(overflow) > 0.0)
    def _():
        for pr in range(n_pairs):
            exact_maxima = exact_scores(pr)
            for e in range(2):
                for m in range(2):
                    p = jnp.exp2(s_ref[e, m] - exact_maxima[e][m]).astype(bf16)
                    pt_ref[pr % n_stage, e, m * half:m * half + span, m * half:(m + 1) * half] = p
            attend(pr)
        project_out()


def _resident(shape):
    zeros = (0,) * len(shape)
    return pl.BlockSpec(shape, lambda b, j: zeros, pipeline_mode=pl.Buffered(1))


def _row_block(rows, d):
    return pl.BlockSpec((1, rows, d), lambda b, j: (b, j, 0))


_IN_HBM = pl.BlockSpec(memory_space=pl.ANY)


_COMPILER_PARAMS = pltpu.CompilerParams(
    dimension_semantics=("arbitrary", "arbitrary"), vmem_limit_bytes=VMEM_LIMIT_BYTES)


def _pool_layer(x, layer, norm_pre, norm_post, w_in, w_group, scale, w_out):
    b, s, d = x.shape
    width = w_out.shape[1]
    tm = POOL_BLOCK_ROWS
    return pl.pallas_call(
        functools.partial(_pool_layer_kernel, layer),
        grid=(b, s // tm),
        in_specs=[_row_block(tm, d), _resident(norm_pre.shape), _resident(norm_post.shape),
                  _IN_HBM, _IN_HBM, _resident(scale.shape), _IN_HBM],
        out_specs=_row_block(tm, d),
        out_shape=jax.ShapeDtypeStruct(x.shape, x.dtype),
        scratch_shapes=[pltpu.VMEM((d, width), jnp.bfloat16),
                        pltpu.VMEM((d, width), jnp.bfloat16),
                        pltpu.VMEM(w_group.shape[1:], jnp.bfloat16),
                        pltpu.VMEM(w_out.shape[1:], jnp.bfloat16),
                        pltpu.VMEM((WEIGHT_STAGE_SLOTS, d, WEIGHT_TILE_COLS), jnp.float32),
                        pltpu.SemaphoreType.DMA((WEIGHT_STAGE_SLOTS,)),
                        pltpu.VMEM((d, width), jnp.bfloat16),
                        pltpu.VMEM((POOL_HALO, width), jnp.float32),
                        pltpu.VMEM((tm // POOL_SUB_ROWS, POOL_SUB_ROWS, width), jnp.bfloat16)],
        compiler_params=_COMPILER_PARAMS,
        name="pool_layer",
    )(x, norm_pre, norm_post, w_in, w_group, scale, w_out)


def _att_layer(x, layer, norm_pre, norm_post, w_in, rel_bias, w_out):
    b, s, d = x.shape
    width = w_out.shape[1]
    n_heads = width // HEAD_DIM
    tq = ATT_BLOCK_ROWS
    n_slots = HIST // tq + 1
    return pl.pallas_call(
        functools.partial(_att_layer_kernel, layer),
        grid=(b, s // tq),
        in_specs=[_row_block(tq, d), _resident(norm_pre.shape), _resident(norm_post.shape),
                  _IN_HBM, _resident(rel_bias.shape), _IN_HBM],
        out_specs=_row_block(tq, d),
        out_shape=jax.ShapeDtypeStruct(x.shape, x.dtype),
        scratch_shapes=[pltpu.VMEM((d, width), jnp.bfloat16),
                        pltpu.VMEM(w_out.shape[1:], jnp.bfloat16),
                        pltpu.VMEM((WEIGHT_STAGE_SLOTS, d, WEIGHT_TILE_COLS), jnp.float32),
                        pltpu.SemaphoreType.DMA((WEIGHT_STAGE_SLOTS,)),
                        pltpu.VMEM((len(ATT_T_BLOCKS), width, d), jnp.bfloat16),
                        pltpu.VMEM((n_slots, tq, width), jnp.bfloat16),
                        pltpu.VMEM((n_slots, width, tq), jnp.bfloat16),
                        pltpu.VMEM((width, tq), jnp.bfloat16),
                        pltpu.VMEM((width, tq), jnp.float32),
                        pltpu.VMEM((tq, width), jnp.bfloat16),
                        pltpu.VMEM((n_heads, ATT_SPAN, LANES), jnp.float32),
                        pltpu.VMEM((ATT_STAGES, 2, 2, LANES, LANES), jnp.float32),
                        pltpu.VMEM((2, 2, ATT_SPAN, LANES), jnp.float32),
                        pltpu.VMEM((ATT_STAGES, 2, HIST + tq, tq), jnp.bfloat16)],
        compiler_params=_COMPILER_PARAMS,
        name="att_layer",
    )(x, norm_pre, norm_post, w_in, rel_bias, w_out)


def kernel(x, norm_pre, norm_post, pool_w_in, pool_w_group, pool_scale, pool_w_out,
           att_w_in, att_rel_bias, att_w_out):
    x = _pool_layer(x, 0, norm_pre, norm_post, pool_w_in, pool_w_group, pool_scale, pool_w_out)
    x = _att_layer(x, 1, norm_pre, norm_post, att_w_in, att_rel_bias, att_w_out)
    return x
```

```python
import functools

import jax
import jax.numpy as jnp
from jax import lax
from jax.experimental import pallas as pl
from jax.experimental.pallas import tpu as pltpu

RMS_EPS = 1e-6
LOG2_E = 1.4426950408889634
SOFTMAX_SUM_LIMIT = 2.0 ** 100
POOL_WINDOWS = (2, 4, 8, 16)
POOL_HALO = 16
CHUNK = 64
LEFT_CHUNKS = 8
HEAD_DIM = 64
MAX_REL = 256
HIST = LEFT_CHUNKS * CHUNK

LANES = 128
SUBLANES = 8
BF16_ROWS = 16
V7X_VMEM_BYTES = 64 * 1024 * 1024
MASK_PENALTY = -1e30
WEIGHT_TILE_COLS = 256
WEIGHT_STAGE_SLOTS = 8
POOL_BLOCK_ROWS = 512
POOL_SUB_ROWS = 256
ATT_BLOCK_ROWS = 256
ATT_T_BLOCKS = (0, 2, 3)
ATT_STAGES = 4
ATT_SPAN = HIST + LANES
REL_LANES = HIST + ATT_BLOCK_ROWS
VMEM_LIMIT_BYTES = V7X_VMEM_BYTES * 7 // 8


def _rms_norm(x, g):
    ms = jnp.mean(x * x, axis=-1, keepdims=True)
    return x * lax.rsqrt(ms + RMS_EPS) * g


class _WeightStream:
    def __init__(self, chunks, stage_ref, sem_ref):
        self.chunks, self.stage_ref, self.sem_ref = chunks, stage_ref, sem_ref
        self.n_slots = stage_ref.shape[0]
        self.depth = self.n_slots - 1
        self.taken = 0
        for i in range(min(self.depth, len(chunks))):
            self._copy(i).start(priority=i % 2)

    def _copy(self, i):
        src = self.chunks[i][0]
        rows, cols = src.shape
        slot = i % self.n_slots
        return pltpu.make_async_copy(src, self.stage_ref.at[slot, pl.ds(0, rows), pl.ds(0, cols)],
                                     self.sem_ref.at[slot])

    def take(self, n):
        for i in range(self.taken, self.taken + n):
            src, consume = self.chunks[i]
            if i + self.depth < len(self.chunks):
                self._copy(i + self.depth).start(priority=(i + self.depth) % 2)
            self._copy(i).wait()
            rows, cols = src.shape
            consume(self.stage_ref[i % self.n_slots, 0:rows, 0:cols])
        self.taken += n


def _pool_layer_kernel(layer, x_ref, gpre_ref, gpost_ref, win_hbm, wg_hbm, scale_ref, wout_hbm,
                       o_ref, wa_ref, wz_ref, wg_ref, wout_ref, stage_ref, sem_ref, wmix_ref, carry_ref, act_ref):
    b = pl.program_id(0)
    j = pl.program_id(1)
    tm = x_ref.shape[1]
    n_sub, sub, width = act_ref.shape
    n_groups = len(POOL_WINDOWS)
    gw = width // n_groups

    first = (b == 0) & (j == 0)

    @pl.when(j == 0)
    def _():
        carry_ref[...] = jnp.zeros_like(carry_ref)

    def normed(sb):
        r0 = sb * sub
        hb = _rms_norm(x_ref[0, r0:r0 + sub, :], gpre_ref[layer:layer + 1, :]).astype(jnp.bfloat16)
        t = j * tm + r0 + lax.broadcasted_iota(jnp.int32, (sub, 1), 0)
        return hb, t

    def project(hb, gi):
        a = jnp.dot(hb, wmix_ref[:, gi * gw:(gi + 1) * gw], preferred_element_type=jnp.float32)
        z = jnp.dot(hb, wz_ref[:, gi * gw:(gi + 1) * gw], preferred_element_type=jnp.float32)
        return a, z

    def mix(sb, gi, a, z, t):
        w = POOL_WINDOWS[gi]
        cols = slice(gi * gw, (gi + 1) * gw)
        ext = jnp.concatenate([carry_ref[:, cols], a], axis=0)
        carry_ref[:, cols] = a[sub - POOL_HALO:, :]
        s = ext
        k = 1
        while k < w:
            s = s + pltpu.roll(s, k, axis=0)
            k *= 2
        cnt = jnp.minimum(t + 1, w).astype(jnp.float32)
        pooled = s[POOL_HALO:, :] / cnt
        m = (pooled - a) * scale_ref[:, cols]
        act_ref[sb, :, cols] = (m * (z * jax.nn.sigmoid(z))).astype(jnp.bfloat16)

    def finish(sb):
        r0 = sb * sub
        y = jnp.dot(act_ref[sb], wout_ref[...], preferred_element_type=jnp.float32)
        o_ref[0, r0:r0 + sub, :] = x_ref[0, r0:r0 + sub, :] + _rms_norm(y, gpost_ref[layer:layer + 1, :])

    @pl.when(first)
    def _():
        cw = stage_ref.shape[2]

        def cast_into(dst_ref, rows, c0):
            def consume(tile):
                dst_ref[rows, c0:c0 + cw] = tile.astype(jnp.bfloat16)
            return consume

        def mix_group(gi, last_consume):
            def consume(tile):
                last_consume(tile)
                cols = slice(gi * gw, (gi + 1) * gw)
                wmix_ref[:, cols] = jnp.dot(wa_ref[:, cols], wg_ref[gi],
                                            preferred_element_type=jnp.float32).astype(jnp.bfloat16)
            return consume

        d_in, d_grp = wa_ref.shape[0], wg_ref.shape[1]
        group_chunks = []
        for gi in range(n_groups):
            chunks = []
            for c0 in range(0, gw, cw):
                chunks.append((wg_hbm.at[0, gi, :, pl.ds(c0, cw)], cast_into(wg_ref.at[gi], slice(0, d_grp), c0)))
            for c0 in range(gi * gw, (gi + 1) * gw, cw):
                chunks.append((win_hbm.at[0, :, pl.ds(c0, cw)], cast_into(wa_ref, slice(0, d_in), c0)))
            chunks[-1] = (chunks[-1][0], mix_group(gi, chunks[-1][1]))
            for c0 in range(gi * gw, (gi + 1) * gw, cw):
                chunks.append((win_hbm.at[0, :, pl.ds(width + c0, cw)], cast_into(wz_ref, slice(0, d_in), c0)))
            group_chunks.append(chunks)
        out_chunks = []
        for r0 in range(0, width, d_in):
            for c0 in range(0, wout_ref.shape[1], cw):
                out_chunks.append((wout_hbm.at[0, pl.ds(r0, d_in), pl.ds(c0, cw)],
                                   cast_into(wout_ref, slice(r0, r0 + d_in), c0)))
        chunks = sum(group_chunks, []) + out_chunks
        _WeightStream(chunks, stage_ref, sem_ref).take(len(chunks))

    for sb in range(n_sub):
        hb, t = normed(sb)
        projected = project(hb, 0)
        for gi in range(n_groups):
            a, z = projected
            if gi + 1 < n_groups:
                projected = project(hb, gi + 1)
            mix(sb, gi, a, z, t)
        finish(sb)


def _build_bias_table(rb_ref, bias_ref, heads):
    n_heads, rows, cols = bias_ref.shape
    lanes = REL_LANES
    n_rel = 2 * MAX_REL
    u = lax.broadcasted_iota(jnp.int32, (1, lanes), 1)
    in_table = (u > cols) & (u < n_rel)
    jj = lax.broadcasted_iota(jnp.int32, (rows, cols), 0)
    c = lax.broadcasted_iota(jnp.int32, (rows, cols), 1)
    kc, qc = jj // CHUNK, c // CHUNK
    in_band = (kc >= qc) & (kc <= qc + LEFT_CHUNKS)
    for h in heads:
        far = rb_ref[0, h:h + 1, n_rel:n_rel + 1]
        row = jnp.concatenate([rb_ref[0, h:h + 1, 0:n_rel], jnp.zeros((1, lanes - n_rel), jnp.float32)], axis=1)
        g = jnp.where(in_table, row, far)
        t = pltpu.roll(jnp.broadcast_to(g, (rows, lanes)), 0, 1, stride=1, stride_axis=0)
        bias_ref[h] = jnp.where(in_band, t[:, 0:cols] * LOG2_E, -jnp.inf)


def _att_layer_kernel(layer, x_ref, gpre_ref, gpost_ref, win_hbm, rb_ref, wout_hbm,
                      o_ref, wk_ref, wout_ref, stage_ref, sem_ref, wt_ref, k_ref, vt_ref, qt_ref, gate_ref, gated_ref, bias_ref, own_ref, s_ref, pt_ref):
    b = pl.program_id(0)
    j = pl.program_id(1)
    tq = x_ref.shape[1]
    width = wout_ref.shape[0]
    n_heads = width // HEAD_DIM
    win = HIST + tq
    half = tq // 2
    span = bias_ref.shape[1]
    pair = 2 * HEAD_DIM
    bf16 = jnp.bfloat16

    @pl.when((b == 0) & (j == 0))
    def _():
        cw = stage_ref.shape[2]

        def cast_into(dst_ref, c0):
            def consume(tile):
                dst_ref[:, c0:c0 + cw] = tile.astype(bf16)
            return consume

        def transpose_into(t, c0, heads):
            def consume(tile):
                wt_ref[t, c0:c0 + cw, :] = tile.T.astype(bf16)
                _build_bias_table(rb_ref, bias_ref, heads)
            return consume

        t_tiles = [(t, blk, c0) for t, blk in enumerate(ATT_T_BLOCKS) for c0 in range(0, width, cw)]
        per_tile = -(-n_heads // len(t_tiles))
        chunks = []
        for i, (t, blk, c0) in enumerate(t_tiles):
            heads = range(min(i * per_tile, n_heads), min((i + 1) * per_tile, n_heads))
            chunks.append((win_hbm.at[0, :, pl.ds(blk * width + c0, cw)], transpose_into(t, c0, heads)))
        for c0 in range(0, width, cw):
            chunks.append((win_hbm.at[0, :, pl.ds(width + c0, cw)], cast_into(wk_ref, c0)))
        for c0 in range(0, wout_ref.shape[1], cw):
            chunks.append((wout_hbm.at[0, :, pl.ds(c0, cw)], cast_into(wout_ref, c0)))
        _WeightStream(chunks, stage_ref, sem_ref).take(len(chunks))

        for buf in range(pt_ref.shape[0]):
            for e in range(2):
                pt_ref[buf, e, span:win, 0:half] = jnp.zeros((win - span, half), bf16)
                pt_ref[buf, e, 0:win - span, half:tq] = jnp.zeros((win - span, half), bf16)

    n_slots = k_ref.shape[0]
    slots = [lax.rem(j + 1 + i, n_slots) for i in range(n_slots)]

    @pl.when(j == 0)
    def _():
        k_ref[...] = jnp.zeros(k_ref.shape, bf16)
        vt_ref[...] = jnp.zeros(vt_ref.shape, bf16)

    h = _rms_norm(x_ref[0], gpre_ref[layer:layer + 1, :])
    hb = h.astype(bf16)
    hbt = h.T.astype(bf16)
    k = jnp.dot(hb, wk_ref[...], preferred_element_type=jnp.float32)
    k_ref[slots[-1]] = k.astype(bf16)
    qt = jnp.dot(wt_ref[0], hbt, preferred_element_type=jnp.float32)
    qt_ref[...] = (qt * (HEAD_DIM ** -0.5 * LOG2_E)).astype(bf16)
    vt = jnp.dot(wt_ref[1], hbt, preferred_element_type=jnp.float32)
    vt_ref[slots[-1]] = vt.astype(bf16)
    zt = jnp.dot(wt_ref[2], hbt, preferred_element_type=jnp.float32)
    gate_ref[...] = zt * jax.nn.sigmoid(zt)

    row = lax.broadcasted_iota(jnp.int32, (win, pair), 0)
    lane = lax.broadcasted_iota(jnp.int32, (win, pair), 1)
    k_aux = jnp.where(lane == 0, jnp.where(row < HIST - j * tq, 1.0, 0.0),
                      jnp.where(lane == 1, 1.0, 0.0)).astype(bf16)
    r2 = lax.broadcasted_iota(jnp.int32, (pair, half), 0)
    zeros_q = jnp.zeros((HEAD_DIM, half), bf16)
    ones_v = jnp.ones((BF16_ROWS, win), bf16)
    n_tiles = win // half
    own = HIST // half

    def penalty(shift=None):
        sub = 0.0 if shift is None else jnp.where(r2 == 1, -shift, 0.0)
        return jnp.where(r2 == 0, MASK_PENALTY, sub).astype(bf16)

    def rhs_tile(pr, m, shifts):
        p0 = pr * pair
        cols = slice(m * half, (m + 1) * half)
        q_a = qt_ref[p0:p0 + HEAD_DIM, cols]
        q_b = qt_ref[p0 + HEAD_DIM:p0 + pair, cols]
        pen = [penalty(None if shifts is None else shifts[e][m]) for e in range(2)]
        return jnp.concatenate([jnp.concatenate([q_a, zeros_q, pen[0]], axis=0),
                                jnp.concatenate([zeros_q, q_b, pen[1]], axis=0)], axis=1)

    def lhs_tile(pr, r):
        s = slots[r // (tq // half)]
        r0 = (r % (tq // half)) * half
        return jnp.concatenate([k_ref[s, r0:r0 + half, pr * pair:(pr + 1) * pair],
                                k_aux[r * half:(r + 1) * half, :]], axis=1)

    def score_dot(lhs, rhs):
        if rhs.shape[1] > tq:
            return jnp.dot(lhs, rhs, preferred_element_type=jnp.float32)
        h0 = lhs.shape[0] // 2
        return jnp.concatenate([jnp.dot(lhs[0:h0, :], rhs, preferred_element_type=jnp.float32),
                                jnp.dot(lhs[h0:, :], rhs, preferred_element_type=jnp.float32)], axis=0)

    def visible(r, skip_own=False):
        return [m for m in range(2)
                if 0 <= r - m < span // half and not (skip_own and r == own + m)]

    n_stage = pt_ref.shape[0]

    def attend(pr):
        buf = pr % n_stage
        gated_t, denoms = [], []
        for e in range(2):
            rows = slice((2 * pr + e) * HEAD_DIM, (2 * pr + e + 1) * HEAD_DIM)
            vt_win = jnp.concatenate([vt_ref[s, rows, :] for s in slots], axis=1)
            v_ext = jnp.concatenate([vt_win, ones_v], axis=0)
            o_ext = jnp.dot(v_ext, pt_ref[buf, e], preferred_element_type=jnp.float32)
            denom = o_ext[HEAD_DIM:HEAD_DIM + SUBLANES, :]
            o_t = o_ext[0:HEAD_DIM, :] * jnp.concatenate([1.0 / denom] * (HEAD_DIM // SUBLANES), axis=0)
            gated_t.append(o_t * gate_ref[rows, :])
            denoms.append(denom)
        gated_ref[:, pr * pair:(pr + 1) * pair] = jnp.concatenate(gated_t, axis=0).T.astype(bf16)
        return denoms

    def project_out():
        for r0 in range(0, tq, half):
            y = jnp.dot(gated_ref[r0:r0 + half, :], wout_ref[...], preferred_element_type=jnp.float32)
            o_ref[0, r0:r0 + half, :] = x_ref[0, r0:r0 + half, :] + _rms_norm(y, gpost_ref[layer:layer + 1, :])

    def own_scores(pr):
        buf = pr % n_stage
        maxima = [[None, None], [None, None]]
        for m in range(2):
            st = score_dot(lhs_tile(pr, own + m), rhs_tile(pr, m, None))
            for e in range(2):
                s = st[:, e * half:(e + 1) * half] + bias_ref[2 * pr + e, HIST:HIST + half, :]
                own_ref[buf, e, m] = s
                maxima[e][m] = jnp.max(s, axis=0, keepdims=True)
        return maxima

    def fast_probs(pr, maxima):
        buf = pr % n_stage
        shifts = [[mx.astype(bf16).astype(jnp.float32) for mx in me] for me in maxima]
        rhs = jnp.concatenate([rhs_tile(pr, m, shifts) for m in range(2)], axis=1)
        groups = [[0], [1, 2, 3], [4]]
        for rows in groups:
            ms = visible(rows[0], skip_own=True)
            lhs = jnp.concatenate([lhs_tile(pr, r) for r in rows], axis=0)
            st_all = score_dot(lhs, rhs[:, ms[0] * tq:(ms[-1] + 1) * tq])
            for ri, r in enumerate(rows):
                st = st_all[ri * half:(ri + 1) * half, :]
                for mi, m in enumerate(ms):
                    for e in range(2):
                        kr = slice((r - m) * half, (r - m + 1) * half)
                        c0 = mi * tq + e * half
                        p = jnp.exp2(st[:, c0:c0 + half] + bias_ref[2 * pr + e, kr, :])
                        pt_ref[buf, e, r * half:(r + 1) * half, m * half:(m + 1) * half] = p.astype(bf16)
        for m in range(2):
            for e in range(2):
                p = jnp.exp2(own_ref[buf, e, m] - shifts[e][m])
                pt_ref[buf, e, (own + m) * half:(own + m + 1) * half,
                       m * half:(m + 1) * half] = p.astype(bf16)

    n_pairs = n_heads // 2
    overflow = jnp.zeros((SUBLANES, tq), jnp.float32)
    maxima = {pr: own_scores(pr) for pr in range(min(2, n_pairs))}
    fast_probs(0, maxima.pop(0))
    for pr in range(n_pairs):
        if pr + 2 < n_pairs:
            maxima[pr + 2] = own_scores(pr + 2)
        if pr + 1 < n_pairs:
            fast_probs(pr + 1, maxima.pop(pr + 1))
        for denom in attend(pr):
            overflow = jnp.maximum(overflow, jnp.where(denom < SOFTMAX_SUM_LIMIT, 0.0, 1.0))
    project_out()

    def exact_scores(pr):
        rhs = jnp.concatenate([rhs_tile(pr, m, None) for m in range(2)], axis=1)
        partial = [[None, None], [None, None]]
        for r in range(n_tiles):
            ms = visible(r)
            st = score_dot(lhs_tile(pr, r), rhs[:, ms[0] * tq:(ms[-1] + 1) * tq])
            for mi, m in enumerate(ms):
                for e in range(2):
                    kr = slice((r - m) * half, (r - m + 1) * half)
                    c0 = mi * tq + e * half
                    s = st[:, c0:c0 + half] + bias_ref[2 * pr + e, kr, :]
                    s_ref[e, m, kr, :] = s
                    for i in range(0, half, SUBLANES):
                        tile = s[i:i + SUBLANES, :]
                        prev = partial[e][m]
                        partial[e][m] = tile if prev is None else jnp.maximum(prev, tile)
        return [[jnp.max(p, axis=0, keepdims=True) for p in pe] for pe in partial]

    @pl.when(jnp.max(overflow) > 0.0)
    def _():
        for pr in range(n_pairs):
            exact_maxima = exact_scores(pr)
            for e in range(2):
                for m in range(2):
                    p = jnp.exp2(s_ref[e, m] - exact_maxima[e][m]).astype(bf16)
                    pt_ref[pr % n_stage, e, m * half:m * half + span, m * half:(m + 1) * half] = p
            attend(pr)
        project_out()


def _resident(shape):
    zeros = (0,) * len(shape)
    return pl.BlockSpec(shape, lambda b, j: zeros, pipeline_mode=pl.Buffered(1))


def _row_block(rows, d):
    return pl.BlockSpec((1, rows, d), lambda b, j: (b, j, 0))


_IN_HBM = pl.BlockSpec(memory_space=pl.ANY)


_COMPILER_PARAMS = pltpu.CompilerParams(
    dimension_semantics=("arbitrary", "arbitrary"), vmem_limit_bytes=VMEM_LIMIT_BYTES)


def _pool_layer(x, layer, norm_pre, norm_post, w_in, w_group, scale, w_out):
    b, s, d = x.shape
    width = w_out.shape[1]
    tm = POOL_BLOCK_ROWS
    return pl.pallas_call(
        functools.partial(_pool_layer_kernel, layer),
        grid=(b, s // tm),
        in_specs=[_row_block(tm, d), _resident(norm_pre.shape), _resident(norm_post.shape),
                  _IN_HBM, _IN_HBM, _resident(scale.shape), _IN_HBM],
        out_specs=_row_block(tm, d),
        out_shape=jax.ShapeDtypeStruct(x.shape, x.dtype),
        scratch_shapes=[pltpu.VMEM((d, width), jnp.bfloat16),
                        pltpu.VMEM((d, width), jnp.bfloat16),
                        pltpu.VMEM(w_group.shape[1:], jnp.bfloat16),
                        pltpu.VMEM(w_out.shape[1:], jnp.bfloat16),
                        pltpu.VMEM((WEIGHT_STAGE_SLOTS, d, WEIGHT_TILE_COLS), jnp.float32),
                        pltpu.SemaphoreType.DMA((WEIGHT_STAGE_SLOTS,)),
                        pltpu.VMEM((d, width), jnp.bfloat16),
                        pltpu.VMEM((POOL_HALO, width), jnp.float32),
                        pltpu.VMEM((tm // POOL_SUB_ROWS, POOL_SUB_ROWS, width), jnp.bfloat16)],
        compiler_params=_COMPILER_PARAMS,
        name="pool_layer",
    )(x, norm_pre, norm_post, w_in, w_group, scale, w_out)


def _att_layer(x, layer, norm_pre, norm_post, w_in, rel_bias, w_out):
    b, s, d = x.shape
    width = w_out.shape[1]
    n_heads = width // HEAD_DIM
    tq = ATT_BLOCK_ROWS
    n_slots = HIST // tq + 1
    return pl.pallas_call(
        functools.partial(_att_layer_kernel, layer),
        grid=(b, s // tq),
        in_specs=[_row_block(tq, d), _resident(norm_pre.shape), _resident(norm_post.shape),
                  _IN_HBM, _resident(rel_bias.shape), _IN_HBM],
        out_specs=_row_block(tq, d),
        out_shape=jax.ShapeDtypeStruct(x.shape, x.dtype),
        scratch_shapes=[pltpu.VMEM((d, width), jnp.bfloat16),
                        pltpu.VMEM(w_out.shape[1:], jnp.bfloat16),
                        pltpu.VMEM((WEIGHT_STAGE_SLOTS, d, WEIGHT_TILE_COLS), jnp.float32),
                        pltpu.SemaphoreType.DMA((WEIGHT_STAGE_SLOTS,)),
                        pltpu.VMEM((len(ATT_T_BLOCKS), width, d), jnp.bfloat16),
                        pltpu.VMEM((n_slots, tq, width), jnp.bfloat16),
                        pltpu.VMEM((n_slots, width, tq), jnp.bfloat16),
                        pltpu.VMEM((width, tq), jnp.bfloat16),
                        pltpu.VMEM((width, tq), jnp.float32),
                        pltpu.VMEM((tq, width), jnp.bfloat16),
                        pltpu.VMEM((n_heads, ATT_SPAN, LANES), jnp.float32),
                        pltpu.VMEM((ATT_STAGES, 2, 2, LANES, LANES), jnp.float32),
                        pltpu.VMEM((2, 2, ATT_SPAN, LANES), jnp.float32),
                        pltpu.VMEM((ATT_STAGES, 2, HIST + tq, tq), jnp.bfloat16)],
        compiler_params=_COMPILER_PARAMS,
        name="att_layer",
    )(x, norm_pre, norm_post, w_in, rel_bias, w_out)


def kernel(x, norm_pre, norm_post, pool_w_in, pool_w_group, pool_scale, pool_w_out,
           att_w_in, att_rel_bias, att_w_out):
    x = _pool_layer(x, 0, norm_pre, norm_post, pool_w_in, pool_w_group, pool_scale, pool_w_out)
    x = _att_layer(x, 1, norm_pre, norm_post, att_w_in, att_rel_bias, att_w_out)
    return x
```

```python
import functools

import jax
import jax.numpy as jnp
from jax import lax
from jax.experimental import pallas as pl
from jax.experimental.pallas import tpu as pltpu

RMS_EPS = 1e-6
LOG2_E = 1.4426950408889634
SOFTMAX_SUM_LIMIT = 2.0 ** 100
POOL_WINDOWS = (2, 4, 8, 16)
POOL_HALO = 16
CHUNK = 64
LEFT_CHUNKS = 8
HEAD_DIM = 64
MAX_REL = 256
HIST = LEFT_CHUNKS * CHUNK

LANES = 128
SUBLANES = 8
BF16_ROWS = 16
V7X_VMEM_BYTES = 64 * 1024 * 1024
MASK_PENALTY = -1e30
WEIGHT_TILE_COLS = 256
WEIGHT_STAGE_SLOTS = 12
POOL_BLOCK_ROWS = 512
POOL_SUB_ROWS = 256
ATT_BLOCK_ROWS = 256
ATT_T_BLOCKS = (0, 2, 3)
ATT_STAGES = 4
ATT_SPAN = HIST + LANES
REL_LANES = HIST + ATT_BLOCK_ROWS
VMEM_LIMIT_BYTES = V7X_VMEM_BYTES * 7 // 8


def _rms_norm(x, g):
    ms = jnp.mean(x * x, axis=-1, keepdims=True)
    return x * lax.rsqrt(ms + RMS_EPS) * g


class _WeightStream:
    def __init__(self, chunks, stage_ref, sem_ref):
        self.chunks, self.stage_ref, self.sem_ref = chunks, stage_ref, sem_ref
        self.n_slots = stage_ref.shape[0]
        self.depth = self.n_slots - 1
        self.taken = 0
        for i in range(min(self.depth, len(chunks))):
            self._copy(i).start(priority=i % 2)

    def _copy(self, i):
        src = self.chunks[i][0]
        rows, cols = src.shape
        slot = i % self.n_slots
        return pltpu.make_async_copy(src, self.stage_ref.at[slot, pl.ds(0, rows), pl.ds(0, cols)],
                                     self.sem_ref.at[slot])

    def take(self, n):
        for i in range(self.taken, self.taken + n):
            src, consume = self.chunks[i]
            if i + self.depth < len(self.chunks):
                self._copy(i + self.depth).start(priority=(i + self.depth) % 2)
            self._copy(i).wait()
            rows, cols = src.shape
            consume(self.stage_ref[i % self.n_slots, 0:rows, 0:cols])
        self.taken += n


def _pool_layer_kernel(layer, x_ref, gpre_ref, gpost_ref, win_hbm, wg_hbm, scale_ref, wout_hbm,
                       o_ref, wa_ref, wz_ref, wg_ref, wout_ref, stage_ref, sem_ref, wmix_ref, carry_ref, act_ref):
    b = pl.program_id(0)
    j = pl.program_id(1)
    tm = x_ref.shape[1]
    n_sub, sub, width = act_ref.shape
    n_groups = len(POOL_WINDOWS)
    gw = width // n_groups

    first = (b == 0) & (j == 0)

    @pl.when(j == 0)
    def _():
        carry_ref[...] = jnp.zeros_like(carry_ref)

    def normed(sb):
        r0 = sb * sub
        hb = _rms_norm(x_ref[0, r0:r0 + sub, :], gpre_ref[layer:layer + 1, :]).astype(jnp.bfloat16)
        t = j * tm + r0 + lax.broadcasted_iota(jnp.int32, (sub, 1), 0)
        return hb, t

    def project(hb, gi):
        a = jnp.dot(hb, wmix_ref[:, gi * gw:(gi + 1) * gw], preferred_element_type=jnp.float32)
        z = jnp.dot(hb, wz_ref[:, gi * gw:(gi + 1) * gw], preferred_element_type=jnp.float32)
        return a, z

    def mix(sb, gi, a, z, t):
        w = POOL_WINDOWS[gi]
        cols = slice(gi * gw, (gi + 1) * gw)
        ext = jnp.concatenate([carry_ref[:, cols], a], axis=0)
        carry_ref[:, cols] = a[sub - POOL_HALO:, :]
        s = ext
        k = 1
        while k < w:
            s = s + pltpu.roll(s, k, axis=0)
            k *= 2
        cnt = jnp.minimum(t + 1, w).astype(jnp.float32)
        pooled = s[POOL_HALO:, :] / cnt
        m = (pooled - a) * scale_ref[:, cols]
        act_ref[sb, :, cols] = (m * (z * jax.nn.sigmoid(z))).astype(jnp.bfloat16)

    def finish(sb):
        r0 = sb * sub
        y = jnp.dot(act_ref[sb], wout_ref[...], preferred_element_type=jnp.float32)
        o_ref[0, r0:r0 + sub, :] = x_ref[0, r0:r0 + sub, :] + _rms_norm(y, gpost_ref[layer:layer + 1, :])

    @pl.when(first)
    def _():
        cw = stage_ref.shape[2]

        def cast_into(dst_ref, rows, c0):
            def consume(tile):
                dst_ref[rows, c0:c0 + cw] = tile.astype(jnp.bfloat16)
            return consume

        def mix_group(gi, last_consume):
            def consume(tile):
                last_consume(tile)
                cols = slice(gi * gw, (gi + 1) * gw)
                wmix_ref[:, cols] = jnp.dot(wa_ref[:, cols], wg_ref[gi],
                                            preferred_element_type=jnp.float32).astype(jnp.bfloat16)
            return consume

        d_in, d_grp = wa_ref.shape[0], wg_ref.shape[1]
        group_chunks = []
        for gi in range(n_groups):
            chunks = []
            for c0 in range(0, gw, cw):
                chunks.append((wg_hbm.at[0, gi, :, pl.ds(c0, cw)], cast_into(wg_ref.at[gi], slice(0, d_grp), c0)))
            for c0 in range(gi * gw, (gi + 1) * gw, cw):
                chunks.append((win_hbm.at[0, :, pl.ds(c0, cw)], cast_into(wa_ref, slice(0, d_in), c0)))
            chunks[-1] = (chunks[-1][0], mix_group(gi, chunks[-1][1]))
            for c0 in range(gi * gw, (gi + 1) * gw, cw):
                chunks.append((win_hbm.at[0, :, pl.ds(width + c0, cw)], cast_into(wz_ref, slice(0, d_in), c0)))
            group_chunks.append(chunks)
        out_chunks = []
        for r0 in range(0, width, d_in):
            for c0 in range(0, wout_ref.shape[1], cw):
                out_chunks.append((wout_hbm.at[0, pl.ds(r0, d_in), pl.ds(c0, cw)],
                                   cast_into(wout_ref, slice(r0, r0 + d_in), c0)))
        chunks = sum(group_chunks, []) + out_chunks
        _WeightStream(chunks, stage_ref, sem_ref).take(len(chunks))

    for sb in range(n_sub):
        hb, t = normed(sb)
        projected = project(hb, 0)
        for gi in range(n_groups):
            a, z = projected
            if gi + 1 < n_groups:
                projected = project(hb, gi + 1)
            mix(sb, gi, a, z, t)
        finish(sb)


def _build_bias_table(rb_ref, bias_ref, heads):
    n_heads, rows, cols = bias_ref.shape
    lanes = REL_LANES
    n_rel = 2 * MAX_REL
    u = lax.broadcasted_iota(jnp.int32, (1, lanes), 1)
    in_table = (u > cols) & (u < n_rel)
    jj = lax.broadcasted_iota(jnp.int32, (rows, cols), 0)
    c = lax.broadcasted_iota(jnp.int32, (rows, cols), 1)
    kc, qc = jj // CHUNK, c // CHUNK
    in_band = (kc >= qc) & (kc <= qc + LEFT_CHUNKS)
    for h in heads:
        far = rb_ref[0, h:h + 1, n_rel:n_rel + 1]
        row = jnp.concatenate([rb_ref[0, h:h + 1, 0:n_rel], jnp.zeros((1, lanes - n_rel), jnp.float32)], axis=1)
        g = jnp.where(in_table, row, far)
        t = pltpu.roll(jnp.broadcast_to(g, (rows, lanes)), 0, 1, stride=1, stride_axis=0)
        bias_ref[h] = jnp.where(in_band, t[:, 0:cols] * LOG2_E, -jnp.inf)


def _att_layer_kernel(layer, x_ref, gpre_ref, gpost_ref, win_hbm, rb_ref, wout_hbm,
                      o_ref, wk_ref, wout_ref, stage_ref, sem_ref, wt_ref, k_ref, vt_ref, qt_ref, gate_ref, gated_ref, bias_ref, own_ref, s_ref, pt_ref):
    b = pl.program_id(0)
    j = pl.program_id(1)
    tq = x_ref.shape[1]
    width = wout_ref.shape[0]
    n_heads = width // HEAD_DIM
    win = HIST + tq
    half = tq // 2
    span = bias_ref.shape[1]
    pair = 2 * HEAD_DIM
    bf16 = jnp.bfloat16

    @pl.when((b == 0) & (j == 0))
    def _():
        cw = stage_ref.shape[2]

        def cast_into(dst_ref, c0):
            def consume(tile):
                dst_ref[:, c0:c0 + cw] = tile.astype(bf16)
            return consume

        def transpose_into(t, c0, heads):
            def consume(tile):
                wt_ref[t, c0:c0 + cw, :] = tile.T.astype(bf16)
                _build_bias_table(rb_ref, bias_ref, heads)
            return consume

        t_tiles = [(t, blk, c0) for t, blk in enumerate(ATT_T_BLOCKS) for c0 in range(0, width, cw)]
        per_tile = -(-n_heads // len(t_tiles))
        chunks = []
        for i, (t, blk, c0) in enumerate(t_tiles):
            heads = range(min(i * per_tile, n_heads), min((i + 1) * per_tile, n_heads))
            chunks.append((win_hbm.at[0, :, pl.ds(blk * width + c0, cw)], transpose_into(t, c0, heads)))
        for c0 in range(0, width, cw):
            chunks.append((win_hbm.at[0, :, pl.ds(width + c0, cw)], cast_into(wk_ref, c0)))
        for c0 in range(0, wout_ref.shape[1], cw):
            chunks.append((wout_hbm.at[0, :, pl.ds(c0, cw)], cast_into(wout_ref, c0)))
        _WeightStream(chunks, stage_ref, sem_ref).take(len(chunks))

        for buf in range(pt_ref.shape[0]):
            for e in range(2):
                pt_ref[buf, e, span:win, 0:half] = jnp.zeros((win - span, half), bf16)
                pt_ref[buf, e, 0:win - span, half:tq] = jnp.zeros((win - span, half), bf16)

    n_slots = k_ref.shape[0]
    slots = [lax.rem(j + 1 + i, n_slots) for i in range(n_slots)]

    @pl.when(j == 0)
    def _():
        k_ref[...] = jnp.zeros(k_ref.shape, bf16)
        vt_ref[...] = jnp.zeros(vt_ref.shape, bf16)

    h = _rms_norm(x_ref[0], gpre_ref[layer:layer + 1, :])
    hb = h.astype(bf16)
    hbt = h.T.astype(bf16)
    k = jnp.dot(hb, wk_ref[...], preferred_element_type=jnp.float32)
    k_ref[slots[-1]] = k.astype(bf16)
    qt = jnp.dot(wt_ref[0], hbt, preferred_element_type=jnp.float32)
    qt_ref[...] = (qt * (HEAD_DIM ** -0.5 * LOG2_E)).astype(bf16)
    vt = jnp.dot(wt_ref[1], hbt, preferred_element_type=jnp.float32)
    vt_ref[slots[-1]] = vt.astype(bf16)
    zt = jnp.dot(wt_ref[2], hbt, preferred_element_type=jnp.float32)
    gate_ref[...] = zt * jax.nn.sigmoid(zt)

    row = lax.broadcasted_iota(jnp.int32, (win, pair), 0)
    lane = lax.broadcasted_iota(jnp.int32, (win, pair), 1)
    k_aux = jnp.where(lane == 0, jnp.where(row < HIST - j * tq, 1.0, 0.0),
                      jnp.where(lane == 1, 1.0, 0.0)).astype(bf16)
    r2 = lax.broadcasted_iota(jnp.int32, (pair, half), 0)
    zeros_q = jnp.zeros((HEAD_DIM, half), bf16)
    ones_v = jnp.ones((BF16_ROWS, win), bf16)
    n_tiles = win // half
    own = HIST // half

    def penalty(shift=None):
        sub = 0.0 if shift is None else jnp.where(r2 == 1, -shift, 0.0)
        return jnp.where(r2 == 0, MASK_PENALTY, sub).astype(bf16)

    def rhs_tile(pr, m, shifts):
        p0 = pr * pair
        cols = slice(m * half, (m + 1) * half)
        q_a = qt_ref[p0:p0 + HEAD_DIM, cols]
        q_b = qt_ref[p0 + HEAD_DIM:p0 + pair, cols]
        pen = [penalty(None if shifts is None else shifts[e][m]) for e in range(2)]
        return jnp.concatenate([jnp.concatenate([q_a, zeros_q, pen[0]], axis=0),
                                jnp.concatenate([zeros_q, q_b, pen[1]], axis=0)], axis=1)

    def lhs_tile(pr, r):
        s = slots[r // (tq // half)]
        r0 = (r % (tq // half)) * half
        return jnp.concatenate([k_ref[s, r0:r0 + half, pr * pair:(pr + 1) * pair],
                                k_aux[r * half:(r + 1) * half, :]], axis=1)

    def score_dot(lhs, rhs):
        if rhs.shape[1] > tq:
            return jnp.dot(lhs, rhs, preferred_element_type=jnp.float32)
        h0 = lhs.shape[0] // 2
        return jnp.concatenate([jnp.dot(lhs[0:h0, :], rhs, preferred_element_type=jnp.float32),
                                jnp.dot(lhs[h0:, :], rhs, preferred_element_type=jnp.float32)], axis=0)

    def visible(r, skip_own=False):
        return [m for m in range(2)
                if 0 <= r - m < span // half and not (skip_own and r == own + m)]

    n_stage = pt_ref.shape[0]

    def attend(pr):
        buf = pr % n_stage
        gated_t, denoms = [], []
        for e in range(2):
            rows = slice((2 * pr + e) * HEAD_DIM, (2 * pr + e + 1) * HEAD_DIM)
            vt_win = jnp.concatenate([vt_ref[s, rows, :] for s in slots], axis=1)
            v_ext = jnp.concatenate([vt_win, ones_v], axis=0)
            o_ext = jnp.dot(v_ext, pt_ref[buf, e], preferred_element_type=jnp.float32)
            denom = o_ext[HEAD_DIM:HEAD_DIM + SUBLANES, :]
            o_t = o_ext[0:HEAD_DIM, :] * jnp.concatenate([1.0 / denom] * (HEAD_DIM // SUBLANES), axis=0)
            gated_t.append(o_t * gate_ref[rows, :])
            denoms.append(denom)
        gated_ref[:, pr * pair:(pr + 1) * pair] = jnp.concatenate(gated_t, axis=0).T.astype(bf16)
        return denoms

    def project_out():
        for r0 in range(0, tq, half):
            y = jnp.dot(gated_ref[r0:r0 + half, :], wout_ref[...], preferred_element_type=jnp.float32)
            o_ref[0, r0:r0 + half, :] = x_ref[0, r0:r0 + half, :] + _rms_norm(y, gpost_ref[layer:layer + 1, :])

    def own_scores(pr):
        buf = pr % n_stage
        maxima = [[None, None], [None, None]]
        for m in range(2):
            st = score_dot(lhs_tile(pr, own + m), rhs_tile(pr, m, None))
            for e in range(2):
                s = st[:, e * half:(e + 1) * half] + bias_ref[2 * pr + e, HIST:HIST + half, :]
                own_ref[buf, e, m] = s
                maxima[e][m] = jnp.max(s, axis=0, keepdims=True)
        return maxima

    def fast_probs(pr, maxima):
        buf = pr % n_stage
        shifts = [[mx.astype(bf16).astype(jnp.float32) for mx in me] for me in maxima]
        rhs = jnp.concatenate([rhs_tile(pr, m, shifts) for m in range(2)], axis=1)
        groups = [[0], [1, 2, 3], [4]]
        for rows in groups:
            ms = visible(rows[0], skip_own=True)
            lhs = jnp.concatenate([lhs_tile(pr, r) for r in rows], axis=0)
            st_all = score_dot(lhs, rhs[:, ms[0] * tq:(ms[-1] + 1) * tq])
            for ri, r in enumerate(rows):
                st = st_all[ri * half:(ri + 1) * half, :]
                for mi, m in enumerate(ms):
                    for e in range(2):
                        kr = slice((r - m) * half, (r - m + 1) * half)
                        c0 = mi * tq + e * half
                        p = jnp.exp2(st[:, c0:c0 + half] + bias_ref[2 * pr + e, kr, :])
                        pt_ref[buf, e, r * half:(r + 1) * half, m * half:(m + 1) * half] = p.astype(bf16)
        for m in range(2):
            for e in range(2):
                p = jnp.exp2(own_ref[buf, e, m] - shifts[e][m])
                pt_ref[buf, e, (own + m) * half:(own + m + 1) * half,
                       m * half:(m + 1) * half] = p.astype(bf16)

    n_pairs = n_heads // 2
    overflow = jnp.zeros((SUBLANES, tq), jnp.float32)
    maxima = {pr: own_scores(pr) for pr in range(min(2, n_pairs))}
    fast_probs(0, maxima.pop(0))
    for pr in range(n_pairs):
        if pr + 2 < n_pairs:
            maxima[pr + 2] = own_scores(pr + 2)
        if pr + 1 < n_pairs:
            fast_probs(pr + 1, maxima.pop(pr + 1))
        for denom in attend(pr):
            overflow = jnp.maximum(overflow, jnp.where(denom < SOFTMAX_SUM_LIMIT, 0.0, 1.0))
    project_out()

    def exact_scores(pr):
        rhs = jnp.concatenate([rhs_tile(pr, m, None) for m in range(2)], axis=1)
        partial = [[None, None], [None, None]]
        for r in range(n_tiles):
            ms = visible(r)
            st = score_dot(lhs_tile(pr, r), rhs[:, ms[0] * tq:(ms[-1] + 1) * tq])
            for mi, m in enumerate(ms):
                for e in range(2):
                    kr = slice((r - m) * half, (r - m + 1) * half)
                    c0 = mi * tq + e * half
                    s = st[:, c0:c0 + half] + bias_ref[2 * pr + e, kr, :]
                    s_ref[e, m, kr, :] = s
                    for i in range(0, half, SUBLANES):
                        tile = s[i:i + SUBLANES, :]
                        prev = partial[e][m]
                        partial[e][m] = tile if prev is None else jnp.maximum(prev, tile)
        return [[jnp.max(p, axis=0, keepdims=True) for p in pe] for pe in partial]

    @pl.when(jnp.max(overflow) > 0.0)
    def _():
        for pr in range(n_pairs):
            exact_maxima = exact_scores(pr)
            for e in range(2):
                for m in range(2):
                    p = jnp.exp2(s_ref[e, m] - exact_maxima[e][m]).astype(bf16)
                    pt_ref[pr % n_stage, e, m * half:m * half + span, m * half:(m + 1) * half] = p
            attend(pr)
        project_out()


def _resident(shape):
    zeros = (0,) * len(shape)
    return pl.BlockSpec(shape, lambda b, j: zeros, pipeline_mode=pl.Buffered(1))


def _row_block(rows, d):
    return pl.BlockSpec((1, rows, d), lambda b, j: (b, j, 0))


_IN_HBM = pl.BlockSpec(memory_space=pl.ANY)


_COMPILER_PARAMS = pltpu.CompilerParams(
    dimension_semantics=("arbitrary", "arbitrary"), vmem_limit_bytes=VMEM_LIMIT_BYTES)


def _pool_layer(x, layer, norm_pre, norm_post, w_in, w_group, scale, w_out):
    b, s, d = x.shape
    width = w_out.shape[1]
    tm = POOL_BLOCK_ROWS
    return pl.pallas_call(
        functools.partial(_pool_layer_kernel, layer),
        grid=(b, s // tm),
        in_specs=[_row_block(tm, d), _resident(norm_pre.shape), _resident(norm_post.shape),
                  _IN_HBM, _IN_HBM, _resident(scale.shape), _IN_HBM],
        out_specs=_row_block(tm, d),
        out_shape=jax.ShapeDtypeStruct(x.shape, x.dtype),
        scratch_shapes=[pltpu.VMEM((d, width), jnp.bfloat16),
                        pltpu.VMEM((d, width), jnp.bfloat16),
                        pltpu.VMEM(w_group.shape[1:], jnp.bfloat16),
                        pltpu.VMEM(w_out.shape[1:], jnp.bfloat16),
                        pltpu.VMEM((WEIGHT_STAGE_SLOTS, d, WEIGHT_TILE_COLS), jnp.float32),
                        pltpu.SemaphoreType.DMA((WEIGHT_STAGE_SLOTS,)),
                        pltpu.VMEM((d, width), jnp.bfloat16),
                        pltpu.VMEM((POOL_HALO, width), jnp.float32),
                        pltpu.VMEM((tm // POOL_SUB_ROWS, POOL_SUB_ROWS, width), jnp.bfloat16)],
        compiler_params=_COMPILER_PARAMS,
        name="pool_layer",
    )(x, norm_pre, norm_post, w_in, w_group, scale, w_out)


def _att_layer(x, layer, norm_pre, norm_post, w_in, rel_bias, w_out):
    b, s, d = x.shape
    width = w_out.shape[1]
    n_heads = width // HEAD_DIM
    tq = ATT_BLOCK_ROWS
    n_slots = HIST // tq + 1
    return pl.pallas_call(
        functools.partial(_att_layer_kernel, layer),
        grid=(b, s // tq),
        in_specs=[_row_block(tq, d), _resident(norm_pre.shape), _resident(norm_post.shape),
                  _IN_HBM, _resident(rel_bias.shape), _IN_HBM],
        out_specs=_row_block(tq, d),
        out_shape=jax.ShapeDtypeStruct(x.shape, x.dtype),
        scratch_shapes=[pltpu.VMEM((d, width), jnp.bfloat16),
                        pltpu.VMEM(w_out.shape[1:], jnp.bfloat16),
                        pltpu.VMEM((WEIGHT_STAGE_SLOTS, d, WEIGHT_TILE_COLS), jnp.float32),
                        pltpu.SemaphoreType.DMA((WEIGHT_STAGE_SLOTS,)),
                        pltpu.VMEM((len(ATT_T_BLOCKS), width, d), jnp.bfloat16),
                        pltpu.VMEM((n_slots, tq, width), jnp.bfloat16),
                        pltpu.VMEM((n_slots, width, tq), jnp.bfloat16),
                        pltpu.VMEM((width, tq), jnp.bfloat16),
                        pltpu.VMEM((width, tq), jnp.float32),
                        pltpu.VMEM((tq, width), jnp.bfloat16),
                        pltpu.VMEM((n_heads, ATT_SPAN, LANES), jnp.float32),
                        pltpu.VMEM((ATT_STAGES, 2, 2, LANES, LANES), jnp.float32),
                        pltpu.VMEM((2, 2, ATT_SPAN, LANES), jnp.float32),
                        pltpu.VMEM((ATT_STAGES, 2, HIST + tq, tq), jnp.bfloat16)],
        compiler_params=_COMPILER_PARAMS,
        name="att_layer",
    )(x, norm_pre, norm_post, w_in, rel_bias, w_out)


def kernel(x, norm_pre, norm_post, pool_w_in, pool_w_group, pool_scale, pool_w_out,
           att_w_in, att_rel_bias, att_w_out):
    x = _pool_layer(x, 0, norm_pre, norm_post, pool_w_in, pool_w_group, pool_scale, pool_w_out)
    x = _att_layer(x, 1, norm_pre, norm_post, att_w_in, att_rel_bias, att_w_out)
    return x
```

```python
import functools

import jax
import jax.numpy as jnp
from jax import lax
from jax.experimental import pallas as pl
from jax.experimental.pallas import tpu as pltpu

RMS_EPS = 1e-6
LOG2_E = 1.4426950408889634
SOFTMAX_SUM_LIMIT = 2.0 ** 100
POOL_WINDOWS = (2, 4, 8, 16)
POOL_HALO = 16
CHUNK = 64
LEFT_CHUNKS = 8
HEAD_DIM = 64
MAX_REL = 256
HIST = LEFT_CHUNKS * CHUNK

LANES = 128
SUBLANES = 8
BF16_ROWS = 16
V7X_VMEM_BYTES = 64 * 1024 * 1024
MASK_PENALTY = -1e30
WEIGHT_TILE_COLS = 256
WEIGHT_STAGE_SLOTS = 8
POOL_BLOCK_ROWS = 512
POOL_SUB_ROWS = 256
ATT_BLOCK_ROWS = 256
ATT_T_BLOCKS = (0, 2, 3)
ATT_STAGES = 4
ATT_SPAN = HIST + LANES
REL_LANES = HIST + ATT_BLOCK_ROWS
VMEM_LIMIT_BYTES = V7X_VMEM_BYTES * 7 // 8


def _rms_norm(x, g):
    ms = jnp.mean(x * x, axis=-1, keepdims=True)
    return x * lax.rsqrt(ms + RMS_EPS) * g


class _WeightStream:
    def __init__(self, chunks, stage_ref, sem_ref):
        self.chunks, self.stage_ref, self.sem_ref = chunks, stage_ref, sem_ref
        self.n_slots = stage_ref.shape[0]
        self.depth = self.n_slots - 1
        self.taken = 0
        for i in range(min(self.depth, len(chunks))):
            self._copy(i).start(priority=i % 2)

    def _copy(self, i):
        src = self.chunks[i][0]
        rows, cols = src.shape
        slot = i % self.n_slots
        return pltpu.make_async_copy(src, self.stage_ref.at[slot, pl.ds(0, rows), pl.ds(0, cols)],
                                     self.sem_ref.at[slot])

    def take(self, n):
        for i in range(self.taken, self.taken + n):
            src, consume = self.chunks[i]
            if i + self.depth < len(self.chunks):
                self._copy(i + self.depth).start(priority=(i + self.depth) % 2)
            self._copy(i).wait()
            rows, cols = src.shape
            consume(self.stage_ref[i % self.n_slots, 0:rows, 0:cols])
        self.taken += n


def _pool_layer_kernel(layer, x_ref, gpre_ref, gpost_ref, win_hbm, wg_hbm, scale_ref, wout_hbm,
                       o_ref, wa_ref, wz_ref, wg_ref, wout_ref, stage_ref, sem_ref, wmix_ref, carry_ref, act_ref):
    b = pl.program_id(0)
    j = pl.program_id(1)
    tm = x_ref.shape[1]
    n_sub, sub, width = act_ref.shape
    n_groups = len(POOL_WINDOWS)
    gw = width // n_groups

    first = (b == 0) & (j == 0)

    @pl.when(j == 0)
    def _():
        carry_ref[...] = jnp.zeros_like(carry_ref)

    def normed(sb):
        r0 = sb * sub
        hb = _rms_norm(x_ref[0, r0:r0 + sub, :], gpre_ref[layer:layer + 1, :]).astype(jnp.bfloat16)
        t = j * tm + r0 + lax.broadcasted_iota(jnp.int32, (sub, 1), 0)
        return hb, t

    def project(hb, gi):
        a = jnp.dot(hb, wmix_ref[:, gi * gw:(gi + 1) * gw], preferred_element_type=jnp.float32)
        z = jnp.dot(hb, wz_ref[:, gi * gw:(gi + 1) * gw], preferred_element_type=jnp.float32)
        return a, z

    def mix(sb, gi, a, z, t):
        w = POOL_WINDOWS[gi]
        cols = slice(gi * gw, (gi + 1) * gw)
        ext = jnp.concatenate([carry_ref[:, cols], a], axis=0)
        carry_ref[:, cols] = a[sub - POOL_HALO:, :]
        s = ext
        k = 1
        while k < w:
            s = s + pltpu.roll(s, k, axis=0)
            k *= 2
        cnt = jnp.minimum(t + 1, w).astype(jnp.float32)
        pooled = s[POOL_HALO:, :] / cnt
        m = (pooled - a) * scale_ref[:, cols]
        act_ref[sb, :, cols] = (m * (z * jax.nn.sigmoid(z))).astype(jnp.bfloat16)

    def finish(sb):
        r0 = sb * sub
        y = jnp.dot(act_ref[sb], wout_ref[...], preferred_element_type=jnp.float32)
        o_ref[0, r0:r0 + sub, :] = x_ref[0, r0:r0 + sub, :] + _rms_norm(y, gpost_ref[layer:layer + 1, :])

    @pl.when(first)
    def _():
        cw = stage_ref.shape[2]

        def cast_into(dst_ref, rows, c0):
            def consume(tile):
                dst_ref[rows, c0:c0 + cw] = tile.astype(jnp.bfloat16)
            return consume

        def mix_group(gi, last_consume):
            def consume(tile):
                last_consume(tile)
                cols = slice(gi * gw, (gi + 1) * gw)
                wmix_ref[:, cols] = jnp.dot(wa_ref[:, cols], wg_ref[gi],
                                            preferred_element_type=jnp.float32).astype(jnp.bfloat16)
            return consume

        d_in, d_grp = wa_ref.shape[0], wg_ref.shape[1]
        group_chunks = []
        for gi in range(n_groups):
            chunks = []
            for c0 in range(0, gw, cw):
                chunks.append((wg_hbm.at[0, gi, :, pl.ds(c0, cw)], cast_into(wg_ref.at[gi], slice(0, d_grp), c0)))
            for c0 in range(gi * gw, (gi + 1) * gw, cw):
                chunks.append((win_hbm.at[0, :, pl.ds(c0, cw)], cast_into(wa_ref, slice(0, d_in), c0)))
            chunks[-1] = (chunks[-1][0], mix_group(gi, chunks[-1][1]))
            for c0 in range(gi * gw, (gi + 1) * gw, cw):
                chunks.append((win_hbm.at[0, :, pl.ds(width + c0, cw)], cast_into(wz_ref, slice(0, d_in), c0)))
            group_chunks.append(chunks)
        out_chunks = []
        for r0 in range(0, width, d_in):
            for c0 in range(0, wout_ref.shape[1], cw):
                out_chunks.append((wout_hbm.at[0, pl.ds(r0, d_in), pl.ds(c0, cw)],
                                   cast_into(wout_ref, slice(r0, r0 + d_in), c0)))
        chunks = sum(group_chunks, []) + out_chunks
        _WeightStream(chunks, stage_ref, sem_ref).take(len(chunks))

    for sb in range(n_sub):
        hb, t = normed(sb)
        projected = project(hb, 0)
        for gi in range(n_groups):
            a, z = projected
            if gi + 1 < n_groups:
                projected = project(hb, gi + 1)
            mix(sb, gi, a, z, t)
        finish(sb)


def _build_bias_table(rb_ref, bias_ref, heads):
    n_heads, rows, cols = bias_ref.shape
    lanes = REL_LANES
    n_rel = 2 * MAX_REL
    u = lax.broadcasted_iota(jnp.int32, (1, lanes), 1)
    in_table = (u > cols) & (u < n_rel)
    jj = lax.broadcasted_iota(jnp.int32, (rows, cols), 0)
    c = lax.broadcasted_iota(jnp.int32, (rows, cols), 1)
    kc, qc = jj // CHUNK, c // CHUNK
    in_band = (kc >= qc) & (kc <= qc + LEFT_CHUNKS)
    for h in heads:
        far = rb_ref[0, h:h + 1, n_rel:n_rel + 1]
        row = jnp.concatenate([rb_ref[0, h:h + 1, 0:n_rel], jnp.zeros((1, lanes - n_rel), jnp.float32)], axis=1)
        g = jnp.where(in_table, row, far)
        t = pltpu.roll(jnp.broadcast_to(g, (rows, lanes)), 0, 1, stride=1, stride_axis=0)
        bias_ref[h] = jnp.where(in_band, t[:, 0:cols] * LOG2_E, -jnp.inf)


def _att_layer_kernel(layer, x_ref, gpre_ref, gpost_ref, win_hbm, rb_ref, wout_hbm,
                      o_ref, wk_ref, wout_ref, stage_ref, sem_ref, wt_ref, k_ref, vt_ref, qt_ref, gate_ref, gated_ref, bias_ref, own_ref, s_ref, pt_ref):
    b = pl.program_id(0)
    j = pl.program_id(1)
    tq = x_ref.shape[1]
    width = wout_ref.shape[0]
    n_heads = width // HEAD_DIM
    win = HIST + tq
    half = tq // 2
    span = bias_ref.shape[1]
    pair = 2 * HEAD_DIM
    bf16 = jnp.bfloat16

    @pl.when((b == 0) & (j == 0))
    def _():
        cw = stage_ref.shape[2]

        def cast_into(dst_ref, c0, heads):
            def consume(tile):
                dst_ref[:, c0:c0 + cw] = tile.astype(bf16)
                _build_bias_table(rb_ref, bias_ref, heads)
            return consume

        def transpose_into(t, c0):
            def consume(tile):
                wt_ref[t, c0:c0 + cw, :] = tile.T.astype(bf16)
            return consume

        cast_tiles = ([(win_hbm.at[0, :, pl.ds(width + c0, cw)], wk_ref, c0) for c0 in range(0, width, cw)]
                      + [(wout_hbm.at[0, :, pl.ds(c0, cw)], wout_ref, c0)
                         for c0 in range(0, wout_ref.shape[1], cw)])
        per_tile = -(-n_heads // len(cast_tiles))
        chunks = []
        for t, blk in enumerate(ATT_T_BLOCKS):
            for c0 in range(0, width, cw):
                chunks.append((win_hbm.at[0, :, pl.ds(blk * width + c0, cw)], transpose_into(t, c0)))
        for i, (src, dst_ref, c0) in enumerate(cast_tiles):
            heads = range(min(i * per_tile, n_heads), min((i + 1) * per_tile, n_heads))
            chunks.append((src, cast_into(dst_ref, c0, heads)))
        n_t = len(chunks) - len(cast_tiles)
        order = []
        for i in range(max(n_t, len(cast_tiles))):
            if i < n_t:
                order.append(chunks[i])
            if i < len(cast_tiles):
                order.append(chunks[n_t + i])
        _WeightStream(order, stage_ref, sem_ref).take(len(order))

        for buf in range(pt_ref.shape[0]):
            for e in range(2):
                pt_ref[buf, e, span:win, 0:half] = jnp.zeros((win - span, half), bf16)
                pt_ref[buf, e, 0:win - span, half:tq] = jnp.zeros((win - span, half), bf16)

    n_slots = k_ref.shape[0]
    slots = [lax.rem(j + 1 + i, n_slots) for i in range(n_slots)]

    @pl.when(j == 0)
    def _():
        k_ref[...] = jnp.zeros(k_ref.shape, bf16)
        vt_ref[...] = jnp.zeros(vt_ref.shape, bf16)

    h = _rms_norm(x_ref[0], gpre_ref[layer:layer + 1, :])
    hb = h.astype(bf16)
    hbt = h.T.astype(bf16)
    k = jnp.dot(hb, wk_ref[...], preferred_element_type=jnp.float32)
    k_ref[slots[-1]] = k.astype(bf16)
    qt = jnp.dot(wt_ref[0], hbt, preferred_element_type=jnp.float32)
    qt_ref[...] = (qt * (HEAD_DIM ** -0.5 * LOG2_E)).astype(bf16)
    vt = jnp.dot(wt_ref[1], hbt, preferred_element_type=jnp.float32)
    vt_ref[slots[-1]] = vt.astype(bf16)
    zt = jnp.dot(wt_ref[2], hbt, preferred_element_type=jnp.float32)
    gate_ref[...] = zt * jax.nn.sigmoid(zt)

    row = lax.broadcasted_iota(jnp.int32, (win, pair), 0)
    lane = lax.broadcasted_iota(jnp.int32, (win, pair), 1)
    k_aux = jnp.where(lane == 0, jnp.where(row < HIST - j * tq, 1.0, 0.0),
                      jnp.where(lane == 1, 1.0, 0.0)).astype(bf16)
    r2 = lax.broadcasted_iota(jnp.int32, (pair, half), 0)
    zeros_q = jnp.zeros((HEAD_DIM, half), bf16)
    ones_v = jnp.ones((BF16_ROWS, win), bf16)
    n_tiles = win // half
    own = HIST // half

    def penalty(shift=None):
        sub = 0.0 if shift is None else jnp.where(r2 == 1, -shift, 0.0)
        return jnp.where(r2 == 0, MASK_PENALTY, sub).astype(bf16)

    def rhs_tile(pr, m, shifts):
        p0 = pr * pair
        cols = slice(m * half, (m + 1) * half)
        q_a = qt_ref[p0:p0 + HEAD_DIM, cols]
        q_b = qt_ref[p0 + HEAD_DIM:p0 + pair, cols]
        pen = [penalty(None if shifts is None else shifts[e][m]) for e in range(2)]
        return jnp.concatenate([jnp.concatenate([q_a, zeros_q, pen[0]], axis=0),
                                jnp.concatenate([zeros_q, q_b, pen[1]], axis=0)], axis=1)

    def lhs_tile(pr, r):
        s = slots[r // (tq // half)]
        r0 = (r % (tq // half)) * half
        return jnp.concatenate([k_ref[s, r0:r0 + half, pr * pair:(pr + 1) * pair],
                                k_aux[r * half:(r + 1) * half, :]], axis=1)

    def score_dot(lhs, rhs):
        if rhs.shape[1] > tq:
            return jnp.dot(lhs, rhs, preferred_element_type=jnp.float32)
        h0 = lhs.shape[0] // 2
        return jnp.concatenate([jnp.dot(lhs[0:h0, :], rhs, preferred_element_type=jnp.float32),
                                jnp.dot(lhs[h0:, :], rhs, preferred_element_type=jnp.float32)], axis=0)

    def visible(r, skip_own=False):
        return [m for m in range(2)
                if 0 <= r - m < span // half and not (skip_own and r == own + m)]

    n_stage = pt_ref.shape[0]

    def attend(pr):
        buf = pr % n_stage
        gated_t, denoms = [], []
        for e in range(2):
            rows = slice((2 * pr + e) * HEAD_DIM, (2 * pr + e + 1) * HEAD_DIM)
            vt_win = jnp.concatenate([vt_ref[s, rows, :] for s in slots], axis=1)
            v_ext = jnp.concatenate([vt_win, ones_v], axis=0)
            o_ext = jnp.dot(v_ext, pt_ref[buf, e], preferred_element_type=jnp.float32)
            denom = o_ext[HEAD_DIM:HEAD_DIM + SUBLANES, :]
            o_t = o_ext[0:HEAD_DIM, :] * jnp.concatenate([1.0 / denom] * (HEAD_DIM // SUBLANES), axis=0)
            gated_t.append(o_t * gate_ref[rows, :])
            denoms.append(denom)
        gated_ref[:, pr * pair:(pr + 1) * pair] = jnp.concatenate(gated_t, axis=0).T.astype(bf16)
        return denoms

    def project_out():
        for r0 in range(0, tq, half):
            y = jnp.dot(gated_ref[r0:r0 + half, :], wout_ref[...], preferred_element_type=jnp.float32)
            o_ref[0, r0:r0 + half, :] = x_ref[0, r0:r0 + half, :] + _rms_norm(y, gpost_ref[layer:layer + 1, :])

    def own_scores(pr):
        buf = pr % n_stage
        maxima = [[None, None], [None, None]]
        for m in range(2):
            st = score_dot(lhs_tile(pr, own + m), rhs_tile(pr, m, None))
            for e in range(2):
                s = st[:, e * half:(e + 1) * half] + bias_ref[2 * pr + e, HIST:HIST + half, :]
                own_ref[buf, e, m] = s
                maxima[e][m] = jnp.max(s, axis=0, keepdims=True)
        return maxima

    def fast_probs(pr, maxima):
        buf = pr % n_stage
        shifts = [[mx.astype(bf16).astype(jnp.float32) for mx in me] for me in maxima]
        rhs = jnp.concatenate([rhs_tile(pr, m, shifts) for m in range(2)], axis=1)
        groups = [[0], [1, 2, 3], [4]]
        for rows in groups:
            ms = visible(rows[0], skip_own=True)
            lhs = jnp.concatenate([lhs_tile(pr, r) for r in rows], axis=0)
            st_all = score_dot(lhs, rhs[:, ms[0] * tq:(ms[-1] + 1) * tq])
            for ri, r in enumerate(rows):
                st = st_all[ri * half:(ri + 1) * half, :]
                for mi, m in enumerate(ms):
                    for e in range(2):
                        kr = slice((r - m) * half, (r - m + 1) * half)
                        c0 = mi * tq + e * half
                        p = jnp.exp2(st[:, c0:c0 + half] + bias_ref[2 * pr + e, kr, :])
                        pt_ref[buf, e, r * half:(r + 1) * half, m * half:(m + 1) * half] = p.astype(bf16)
        for m in range(2):
            for e in range(2):
                p = jnp.exp2(own_ref[buf, e, m] - shifts[e][m])
                pt_ref[buf, e, (own + m) * half:(own + m + 1) * half,
                       m * half:(m + 1) * half] = p.astype(bf16)

    n_pairs = n_heads // 2
    overflow = jnp.zeros((SUBLANES, tq), jnp.float32)
    maxima = {pr: own_scores(pr) for pr in range(min(2, n_pairs))}
    fast_probs(0, maxima.pop(0))
    for pr in range(n_pairs):
        if pr + 2 < n_pairs:
            maxima[pr + 2] = own_scores(pr + 2)
        if pr + 1 < n_pairs:
            fast_probs(pr + 1, maxima.pop(pr + 1))
        for denom in attend(pr):
            overflow = jnp.maximum(overflow, jnp.where(denom < SOFTMAX_SUM_LIMIT, 0.0, 1.0))
    project_out()

    def exact_scores(pr):
        rhs = jnp.concatenate([rhs_tile(pr, m, None) for m in range(2)], axis=1)
        partial = [[None, None], [None, None]]
        for r in range(n_tiles):
            ms = visible(r)
            st = score_dot(lhs_tile(pr, r), rhs[:, ms[0] * tq:(ms[-1] + 1) * tq])
            for mi, m in enumerate(ms):
                for e in range(2):
                    kr = slice((r - m) * half, (r - m + 1) * half)
                    c0 = mi * tq + e * half
                    s = st[:, c0:c0 + half] + bias_ref[2 * pr + e, kr, :]
                    s_ref[e, m, kr, :] = s
                    for i in range(0, half, SUBLANES):
                        tile = s[i:i + SUBLANES, :]
                        prev = partial[e][m]
                        partial[e][m] = tile if prev is None else jnp.maximum(prev, tile)
        return [[jnp.max(p, axis=0, keepdims=True) for p in pe] for pe in partial]

    @pl.when(jnp.max(overflow) > 0.0)
    def _():
        for pr in range(n_pairs):
            exact_maxima = exact_scores(pr)
            for e in range(2):
                for m in range(2):
                    p = jnp.exp2(s_ref[e, m] - exact_maxima[e][m]).astype(bf16)
                    pt_ref[pr % n_stage, e, m * half:m * half + span, m * half:(m + 1) * half] = p
            attend(pr)
        project_out()


def _resident(shape):
    zeros = (0,) * len(shape)
    return pl.BlockSpec(shape, lambda b, j: zeros, pipeline_mode=pl.Buffered(1))


def _row_block(rows, d):
    return pl.BlockSpec((1, rows, d), lambda b, j: (b, j, 0))


_IN_HBM = pl.BlockSpec(memory_space=pl.ANY)


_COMPILER_PARAMS = pltpu.CompilerParams(
    dimension_semantics=("arbitrary", "arbitrary"), vmem_limit_bytes=VMEM_LIMIT_BYTES)


def _pool_layer(x, layer, norm_pre, norm_post, w_in, w_group, scale, w_out):
    b, s, d = x.shape
    width = w_out.shape[1]
    tm = POOL_BLOCK_ROWS
    return pl.pallas_call(
        functools.partial(_pool_layer_kernel, layer),
        grid=(b, s // tm),
        in_specs=[_row_block(tm, d), _resident(norm_pre.shape), _resident(norm_post.shape),
                  _IN_HBM, _IN_HBM, _resident(scale.shape), _IN_HBM],
        out_specs=_row_block(tm, d),
        out_shape=jax.ShapeDtypeStruct(x.shape, x.dtype),
        scratch_shapes=[pltpu.VMEM((d, width), jnp.bfloat16),
                        pltpu.VMEM((d, width), jnp.bfloat16),
                        pltpu.VMEM(w_group.shape[1:], jnp.bfloat16),
                        pltpu.VMEM(w_out.shape[1:], jnp.bfloat16),
                        pltpu.VMEM((WEIGHT_STAGE_SLOTS, d, WEIGHT_TILE_COLS), jnp.float32),
                        pltpu.SemaphoreType.DMA((WEIGHT_STAGE_SLOTS,)),
                        pltpu.VMEM((d, width), jnp.bfloat16),
                        pltpu.VMEM((POOL_HALO, width), jnp.float32),
                        pltpu.VMEM((tm // POOL_SUB_ROWS, POOL_SUB_ROWS, width), jnp.bfloat16)],
        compiler_params=_COMPILER_PARAMS,
        name="pool_layer",
    )(x, norm_pre, norm_post, w_in, w_group, scale, w_out)


def _att_layer(x, layer, norm_pre, norm_post, w_in, rel_bias, w_out):
    b, s, d = x.shape
    width = w_out.shape[1]
    n_heads = width // HEAD_DIM
    tq = ATT_BLOCK_ROWS
    n_slots = HIST // tq + 1
    return pl.pallas_call(
        functools.partial(_att_layer_kernel, layer),
        grid=(b, s // tq),
        in_specs=[_row_block(tq, d), _resident(norm_pre.shape), _resident(norm_post.shape),
                  _IN_HBM, _resident(rel_bias.shape), _IN_HBM],
        out_specs=_row_block(tq, d),
        out_shape=jax.ShapeDtypeStruct(x.shape, x.dtype),
        scratch_shapes=[pltpu.VMEM((d, width), jnp.bfloat16),
                        pltpu.VMEM(w_out.shape[1:], jnp.bfloat16),
                        pltpu.VMEM((WEIGHT_STAGE_SLOTS, d, WEIGHT_TILE_COLS), jnp.float32),
                        pltpu.SemaphoreType.DMA((WEIGHT_STAGE_SLOTS,)),
                        pltpu.VMEM((len(ATT_T_BLOCKS), width, d), jnp.bfloat16),
                        pltpu.VMEM((n_slots, tq, width), jnp.bfloat16),
                        pltpu.VMEM((n_slots, width, tq), jnp.bfloat16),
                        pltpu.VMEM((width, tq), jnp.bfloat16),
                        pltpu.VMEM((width, tq), jnp.float32),
                        pltpu.VMEM((tq, width), jnp.bfloat16),
                        pltpu.VMEM((n_heads, ATT_SPAN, LANES), jnp.float32),
                        pltpu.VMEM((ATT_STAGES, 2, 2, LANES, LANES), jnp.float32),
                        pltpu.VMEM((2, 2, ATT_SPAN, LANES), jnp.float32),
                        pltpu.VMEM((ATT_STAGES, 2, HIST + tq, tq), jnp.bfloat16)],
        compiler_params=_COMPILER_PARAMS,
        name="att_layer",
    )(x, norm_pre, norm_post, w_in, rel_bias, w_out)


def kernel(x, norm_pre, norm_post, pool_w_in, pool_w_group, pool_scale, pool_w_out,
           att_w_in, att_rel_bias, att_w_out):
    x = _pool_layer(x, 0, norm_pre, norm_post, pool_w_in, pool_w_group, pool_scale, pool_w_out)
    x = _att_layer(x, 1, norm_pre, norm_post, att_w_in, att_rel_bias, att_w_out)
    return x
```

```python
import functools

import jax
import jax.numpy as jnp
from jax import lax
from jax.experimental import pallas as pl
from jax.experimental.pallas import tpu as pltpu

RMS_EPS = 1e-6
LOG2_E = 1.4426950408889634
SOFTMAX_SUM_LIMIT = 2.0 ** 100
POOL_WINDOWS = (2, 4, 8, 16)
POOL_HALO = 16
CHUNK = 64
LEFT_CHUNKS = 8
HEAD_DIM = 64
MAX_REL = 256
HIST = LEFT_CHUNKS * CHUNK

LANES = 128
SUBLANES = 8
BF16_ROWS = 16
V7X_VMEM_BYTES = 64 * 1024 * 1024
MASK_PENALTY = -1e30
WEIGHT_TILE_COLS = 512
WEIGHT_STAGE_SLOTS = 8
POOL_BLOCK_ROWS = 512
POOL_SUB_ROWS = 256
ATT_BLOCK_ROWS = 256
ATT_T_BLOCKS = (0, 2, 3)
ATT_STAGES = 4
ATT_SPAN = HIST + LANES
REL_LANES = HIST + ATT_BLOCK_ROWS
VMEM_LIMIT_BYTES = V7X_VMEM_BYTES * 7 // 8


def _rms_norm(x, g):
    ms = jnp.mean(x * x, axis=-1, keepdims=True)
    return x * lax.rsqrt(ms + RMS_EPS) * g


class _WeightStream:
    def __init__(self, chunks, stage_ref, sem_ref):
        self.chunks, self.stage_ref, self.sem_ref = chunks, stage_ref, sem_ref
        self.n_slots = stage_ref.shape[0]
        self.depth = self.n_slots - 1
        self.taken = 0
        for i in range(min(self.depth, len(chunks))):
            self._copy(i).start(priority=i % 2)

    def _copy(self, i):
        src = self.chunks[i][0]
        rows, cols = src.shape
        slot = i % self.n_slots
        return pltpu.make_async_copy(src, self.stage_ref.at[slot, pl.ds(0, rows), pl.ds(0, cols)],
                                     self.sem_ref.at[slot])

    def take(self, n):
        for i in range(self.taken, self.taken + n):
            src, consume = self.chunks[i]
            if i + self.depth < len(self.chunks):
                self._copy(i + self.depth).start(priority=(i + self.depth) % 2)
            self._copy(i).wait()
            rows, cols = src.shape
            consume(self.stage_ref[i % self.n_slots, 0:rows, 0:cols])
        self.taken += n


def _pool_layer_kernel(layer, x_ref, gpre_ref, gpost_ref, win_hbm, wg_hbm, scale_ref, wout_hbm,
                       o_ref, wa_ref, wz_ref, wg_ref, wout_ref, stage_ref, sem_ref, wmix_ref, carry_ref, act_ref):
    b = pl.program_id(0)
    j = pl.program_id(1)
    tm = x_ref.shape[1]
    n_sub, sub, width = act_ref.shape
    n_groups = len(POOL_WINDOWS)
    gw = width // n_groups

    first = (b == 0) & (j == 0)

    @pl.when(j == 0)
    def _():
        carry_ref[...] = jnp.zeros_like(carry_ref)

    def normed(sb):
        r0 = sb * sub
        hb = _rms_norm(x_ref[0, r0:r0 + sub, :], gpre_ref[layer:layer + 1, :]).astype(jnp.bfloat16)
        t = j * tm + r0 + lax.broadcasted_iota(jnp.int32, (sub, 1), 0)
        return hb, t

    def project(hb, gi):
        a = jnp.dot(hb, wmix_ref[:, gi * gw:(gi + 1) * gw], preferred_element_type=jnp.float32)
        z = jnp.dot(hb, wz_ref[:, gi * gw:(gi + 1) * gw], preferred_element_type=jnp.float32)
        return a, z

    def mix(sb, gi, a, z, t):
        w = POOL_WINDOWS[gi]
        cols = slice(gi * gw, (gi + 1) * gw)
        ext = jnp.concatenate([carry_ref[:, cols], a], axis=0)
        carry_ref[:, cols] = a[sub - POOL_HALO:, :]
        s = ext
        k = 1
        while k < w:
            s = s + pltpu.roll(s, k, axis=0)
            k *= 2
        cnt = jnp.minimum(t + 1, w).astype(jnp.float32)
        pooled = s[POOL_HALO:, :] / cnt
        m = (pooled - a) * scale_ref[:, cols]
        act_ref[sb, :, cols] = (m * (z * jax.nn.sigmoid(z))).astype(jnp.bfloat16)

    def finish(sb):
        r0 = sb * sub
        y = jnp.dot(act_ref[sb], wout_ref[...], preferred_element_type=jnp.float32)
        o_ref[0, r0:r0 + sub, :] = x_ref[0, r0:r0 + sub, :] + _rms_norm(y, gpost_ref[layer:layer + 1, :])

    @pl.when(first)
    def _():
        cw = stage_ref.shape[2]

        def cast_into(dst_ref, rows, c0):
            def consume(tile):
                dst_ref[rows, c0:c0 + cw] = tile.astype(jnp.bfloat16)
            return consume

        def mix_group(gi, last_consume):
            def consume(tile):
                last_consume(tile)
                cols = slice(gi * gw, (gi + 1) * gw)
                wmix_ref[:, cols] = jnp.dot(wa_ref[:, cols], wg_ref[gi],
                                            preferred_element_type=jnp.float32).astype(jnp.bfloat16)
            return consume

        d_in, d_grp = wa_ref.shape[0], wg_ref.shape[1]
        group_chunks = []
        for gi in range(n_groups):
            chunks = []
            for c0 in range(0, gw, cw):
                chunks.append((wg_hbm.at[0, gi, :, pl.ds(c0, cw)], cast_into(wg_ref.at[gi], slice(0, d_grp), c0)))
            for c0 in range(gi * gw, (gi + 1) * gw, cw):
                chunks.append((win_hbm.at[0, :, pl.ds(c0, cw)], cast_into(wa_ref, slice(0, d_in), c0)))
            chunks[-1] = (chunks[-1][0], mix_group(gi, chunks[-1][1]))
            for c0 in range(gi * gw, (gi + 1) * gw, cw):
                chunks.append((win_hbm.at[0, :, pl.ds(width + c0, cw)], cast_into(wz_ref, slice(0, d_in), c0)))
            group_chunks.append(chunks)
        out_chunks = []
        for r0 in range(0, width, d_in):
            for c0 in range(0, wout_ref.shape[1], cw):
                out_chunks.append((wout_hbm.at[0, pl.ds(r0, d_in), pl.ds(c0, cw)],
                                   cast_into(wout_ref, slice(r0, r0 + d_in), c0)))
        chunks = sum(group_chunks, []) + out_chunks
        _WeightStream(chunks, stage_ref, sem_ref).take(len(chunks))

    for sb in range(n_sub):
        hb, t = normed(sb)
        projected = project(hb, 0)
        for gi in range(n_groups):
            a, z = projected
            if gi + 1 < n_groups:
                projected = project(hb, gi + 1)
            mix(sb, gi, a, z, t)
        finish(sb)


def _build_bias_table(rb_ref, bias_ref, heads):
    n_heads, rows, cols = bias_ref.shape
    lanes = REL_LANES
    n_rel = 2 * MAX_REL
    u = lax.broadcasted_iota(jnp.int32, (1, lanes), 1)
    in_table = (u > cols) & (u < n_rel)
    jj = lax.broadcasted_iota(jnp.int32, (rows, cols), 0)
    c = lax.broadcasted_iota(jnp.int32, (rows, cols), 1)
    kc, qc = jj // CHUNK, c // CHUNK
    in_band = (kc >= qc) & (kc <= qc + LEFT_CHUNKS)
    for h in heads:
        far = rb_ref[0, h:h + 1, n_rel:n_rel + 1]
        row = jnp.concatenate([rb_ref[0, h:h + 1, 0:n_rel], jnp.zeros((1, lanes - n_rel), jnp.float32)], axis=1)
        g = jnp.where(in_table, row, far)
        t = pltpu.roll(jnp.broadcast_to(g, (rows, lanes)), 0, 1, stride=1, stride_axis=0)
        bias_ref[h] = jnp.where(in_band, t[:, 0:cols] * LOG2_E, -jnp.inf)


def _att_layer_kernel(layer, x_ref, gpre_ref, gpost_ref, win_hbm, rb_ref, wout_hbm,
                      o_ref, wk_ref, wout_ref, stage_ref, sem_ref, wt_ref, k_ref, vt_ref, qt_ref, gate_ref, gated_ref, bias_ref, own_ref, s_ref, pt_ref):
    b = pl.program_id(0)
    j = pl.program_id(1)
    tq = x_ref.shape[1]
    width = wout_ref.shape[0]
    n_heads = width // HEAD_DIM
    win = HIST + tq
    half = tq // 2
    span = bias_ref.shape[1]
    pair = 2 * HEAD_DIM
    bf16 = jnp.bfloat16

    @pl.when((b == 0) & (j == 0))
    def _():
        cw = stage_ref.shape[2]

        def cast_into(dst_ref, c0):
            def consume(tile):
                dst_ref[:, c0:c0 + cw] = tile.astype(bf16)
            return consume

        def transpose_into(t, c0, heads):
            def consume(tile):
                wt_ref[t, c0:c0 + cw, :] = tile.T.astype(bf16)
                _build_bias_table(rb_ref, bias_ref, heads)
            return consume

        t_tiles = [(t, blk, c0) for t, blk in enumerate(ATT_T_BLOCKS) for c0 in range(0, width, cw)]
        per_tile = -(-n_heads // len(t_tiles))
        chunks = []
        for i, (t, blk, c0) in enumerate(t_tiles):
            heads = range(min(i * per_tile, n_heads), min((i + 1) * per_tile, n_heads))
            chunks.append((win_hbm.at[0, :, pl.ds(blk * width + c0, cw)], transpose_into(t, c0, heads)))
        for c0 in range(0, width, cw):
            chunks.append((win_hbm.at[0, :, pl.ds(width + c0, cw)], cast_into(wk_ref, c0)))
        for c0 in range(0, wout_ref.shape[1], cw):
            chunks.append((wout_hbm.at[0, :, pl.ds(c0, cw)], cast_into(wout_ref, c0)))
        _WeightStream(chunks, stage_ref, sem_ref).take(len(chunks))

        for buf in range(pt_ref.shape[0]):
            for e in range(2):
                pt_ref[buf, e, span:win, 0:half] = jnp.zeros((win - span, half), bf16)
                pt_ref[buf, e, 0:win - span, half:tq] = jnp.zeros((win - span, half), bf16)

    n_slots = k_ref.shape[0]
    slots = [lax.rem(j + 1 + i, n_slots) for i in range(n_slots)]

    @pl.when(j == 0)
    def _():
        k_ref[...] = jnp.zeros(k_ref.shape, bf16)
        vt_ref[...] = jnp.zeros(vt_ref.shape, bf16)

    h = _rms_norm(x_ref[0], gpre_ref[layer:layer + 1, :])
    hb = h.astype(bf16)
    hbt = h.T.astype(bf16)
    k = jnp.dot(hb, wk_ref[...], preferred_element_type=jnp.float32)
    k_ref[slots[-1]] = k.astype(bf16)
    qt = jnp.dot(wt_ref[0], hbt, preferred_element_type=jnp.float32)
    qt_ref[...] = (qt * (HEAD_DIM ** -0.5 * LOG2_E)).astype(bf16)
    vt = jnp.dot(wt_ref[1], hbt, preferred_element_type=jnp.float32)
    vt_ref[slots[-1]] = vt.astype(bf16)
    zt = jnp.dot(wt_ref[2], hbt, preferred_element_type=jnp.float32)
    gate_ref[...] = zt * jax.nn.sigmoid(zt)

    row = lax.broadcasted_iota(jnp.int32, (win, pair), 0)
    lane = lax.broadcasted_iota(jnp.int32, (win, pair), 1)
    k_aux = jnp.where(lane == 0, jnp.where(row < HIST - j * tq, 1.0, 0.0),
                      jnp.where(lane == 1, 1.0, 0.0)).astype(bf16)
    r2 = lax.broadcasted_iota(jnp.int32, (pair, half), 0)
    zeros_q = jnp.zeros((HEAD_DIM, half), bf16)
    ones_v = jnp.ones((BF16_ROWS, win), bf16)
    n_tiles = win // half
    own = HIST // half

    def penalty(shift=None):
        sub = 0.0 if shift is None else jnp.where(r2 == 1, -shift, 0.0)
        return jnp.where(r2 == 0, MASK_PENALTY, sub).astype(bf16)

    def rhs_tile(pr, m, shifts):
        p0 = pr * pair
        cols = slice(m * half, (m + 1) * half)
        q_a = qt_ref[p0:p0 + HEAD_DIM, cols]
        q_b = qt_ref[p0 + HEAD_DIM:p0 + pair, cols]
        pen = [penalty(None if shifts is None else shifts[e][m]) for e in range(2)]
        return jnp.concatenate([jnp.concatenate([q_a, zeros_q, pen[0]], axis=0),
                                jnp.concatenate([zeros_q, q_b, pen[1]], axis=0)], axis=1)

    def lhs_tile(pr, r):
        s = slots[r // (tq // half)]
        r0 = (r % (tq // half)) * half
        return jnp.concatenate([k_ref[s, r0:r0 + half, pr * pair:(pr + 1) * pair],
                                k_aux[r * half:(r + 1) * half, :]], axis=1)

    def score_dot(lhs, rhs):
        if rhs.shape[1] > tq:
            return jnp.dot(lhs, rhs, preferred_element_type=jnp.float32)
        h0 = lhs.shape[0] // 2
        return jnp.concatenate([jnp.dot(lhs[0:h0, :], rhs, preferred_element_type=jnp.float32),
                                jnp.dot(lhs[h0:, :], rhs, preferred_element_type=jnp.float32)], axis=0)

    def visible(r, skip_own=False):
        return [m for m in range(2)
                if 0 <= r - m < span // half and not (skip_own and r == own + m)]

    n_stage = pt_ref.shape[0]

    def attend(pr):
        buf = pr % n_stage
        gated_t, denoms = [], []
        for e in range(2):
            rows = slice((2 * pr + e) * HEAD_DIM, (2 * pr + e + 1) * HEAD_DIM)
            vt_win = jnp.concatenate([vt_ref[s, rows, :] for s in slots], axis=1)
            v_ext = jnp.concatenate([vt_win, ones_v], axis=0)
            o_ext = jnp.dot(v_ext, pt_ref[buf, e], preferred_element_type=jnp.float32)
            denom = o_ext[HEAD_DIM:HEAD_DIM + SUBLANES, :]
            o_t = o_ext[0:HEAD_DIM, :] * jnp.concatenate([1.0 / denom] * (HEAD_DIM // SUBLANES), axis=0)
            gated_t.append(o_t * gate_ref[rows, :])
            denoms.append(denom)
        gated_ref[:, pr * pair:(pr + 1) * pair] = jnp.concatenate(gated_t, axis=0).T.astype(bf16)
        return denoms

    def project_out():
        for r0 in range(0, tq, half):
            y = jnp.dot(gated_ref[r0:r0 + half, :], wout_ref[...], preferred_element_type=jnp.float32)
            o_ref[0, r0:r0 + half, :] = x_ref[0, r0:r0 + half, :] + _rms_norm(y, gpost_ref[layer:layer + 1, :])

    def own_scores(pr):
        buf = pr % n_stage
        maxima = [[None, None], [None, None]]
        for m in range(2):
            st = score_dot(lhs_tile(pr, own + m), rhs_tile(pr, m, None))
            for e in range(2):
                s = st[:, e * half:(e + 1) * half] + bias_ref[2 * pr + e, HIST:HIST + half, :]
                own_ref[buf, e, m] = s
                maxima[e][m] = jnp.max(s, axis=0, keepdims=True)
        return maxima

    def fast_probs(pr, maxima):
        buf = pr % n_stage
        shifts = [[mx.astype(bf16).astype(jnp.float32) for mx in me] for me in maxima]
        rhs = jnp.concatenate([rhs_tile(pr, m, shifts) for m in range(2)], axis=1)
        groups = [[0], [1, 2, 3], [4]]
        for rows in groups:
            ms = visible(rows[0], skip_own=True)
            lhs = jnp.concatenate([lhs_tile(pr, r) for r in rows], axis=0)
            st_all = score_dot(lhs, rhs[:, ms[0] * tq:(ms[-1] + 1) * tq])
            for ri, r in enumerate(rows):
                st = st_all[ri * half:(ri + 1) * half, :]
                for mi, m in enumerate(ms):
                    for e in range(2):
                        kr = slice((r - m) * half, (r - m + 1) * half)
                        c0 = mi * tq + e * half
                        p = jnp.exp2(st[:, c0:c0 + half] + bias_ref[2 * pr + e, kr, :])
                        pt_ref[buf, e, r * half:(r + 1) * half, m * half:(m + 1) * half] = p.astype(bf16)
        for m in range(2):
            for e in range(2):
                p = jnp.exp2(own_ref[buf, e, m] - shifts[e][m])
                pt_ref[buf, e, (own + m) * half:(own + m + 1) * half,
                       m * half:(m + 1) * half] = p.astype(bf16)

    n_pairs = n_heads // 2
    overflow = jnp.zeros((SUBLANES, tq), jnp.float32)
    maxima = {pr: own_scores(pr) for pr in range(min(2, n_pairs))}
    fast_probs(0, maxima.pop(0))
    for pr in range(n_pairs):
        if pr + 2 < n_pairs:
            maxima[pr + 2] = own_scores(pr + 2)
        if pr + 1 < n_pairs:
            fast_probs(pr + 1, maxima.pop(pr + 1))
        for denom in attend(pr):
            overflow = jnp.maximum(overflow, jnp.where(denom < SOFTMAX_SUM_LIMIT, 0.0, 1.0))
    project_out()

    def exact_scores(pr):
        rhs = jnp.concatenate([rhs_tile(pr, m, None) for m in range(2)], axis=1)
        partial = [[None, None], [None, None]]
        for r in range(n_tiles):
            ms = visible(r)
            st = score_dot(lhs_tile(pr, r), rhs[:, ms[0] * tq:(ms[-1] + 1) * tq])
            for mi, m in enumerate(ms):
                for e in range(2):
                    kr = slice((r - m) * half, (r - m + 1) * half)
                    c0 = mi * tq + e * half
                    s = st[:, c0:c0 + half] + bias_ref[2 * pr + e, kr, :]
                    s_ref[e, m, kr, :] = s
                    for i in range(0, half, SUBLANES):
                        tile = s[i:i + SUBLANES, :]
                        prev = partial[e][m]
                        partial[e][m] = tile if prev is None else jnp.maximum(prev, tile)
        return [[jnp.max(p, axis=0, keepdims=True) for p in pe] for pe in partial]

    @pl.when(jnp.max(overflow) > 0.0)
    def _():
        for pr in range(n_pairs):
            exact_maxima = exact_scores(pr)
            for e in range(2):
                for m in range(2):
                    p = jnp.exp2(s_ref[e, m] - exact_maxima[e][m]).astype(bf16)
                    pt_ref[pr % n_stage, e, m * half:m * half + span, m * half:(m + 1) * half] = p
            attend(pr)
        project_out()


def _resident(shape):
    zeros = (0,) * len(shape)
    return pl.BlockSpec(shape, lambda b, j: zeros, pipeline_mode=pl.Buffered(1))


def _row_block(rows, d):
    return pl.BlockSpec((1, rows, d), lambda b, j: (b, j, 0))


_IN_HBM = pl.BlockSpec(memory_space=pl.ANY)


_COMPILER_PARAMS = pltpu.CompilerParams(
    dimension_semantics=("arbitrary", "arbitrary"), vmem_limit_bytes=VMEM_LIMIT_BYTES)


def _pool_layer(x, layer, norm_pre, norm_post, w_in, w_group, scale, w_out):
    b, s, d = x.shape
    width = w_out.shape[1]
    tm = POOL_BLOCK_ROWS
    return pl.pallas_call(
        functools.partial(_pool_layer_kernel, layer),
        grid=(b, s // tm),
        in_specs=[_row_block(tm, d), _resident(norm_pre.shape), _resident(norm_post.shape),
                  _IN_HBM, _IN_HBM, _resident(scale.shape), _IN_HBM],
        out_specs=_row_block(tm, d),
        out_shape=jax.ShapeDtypeStruct(x.shape, x.dtype),
        scratch_shapes=[pltpu.VMEM((d, width), jnp.bfloat16),
                        pltpu.VMEM((d, width), jnp.bfloat16),
                        pltpu.VMEM(w_group.shape[1:], jnp.bfloat16),
                        pltpu.VMEM(w_out.shape[1:], jnp.bfloat16),
                        pltpu.VMEM((WEIGHT_STAGE_SLOTS, d, WEIGHT_TILE_COLS), jnp.float32),
                        pltpu.SemaphoreType.DMA((WEIGHT_STAGE_SLOTS,)),
                        pltpu.VMEM((d, width), jnp.bfloat16),
                        pltpu.VMEM((POOL_HALO, width), jnp.float32),
                        pltpu.VMEM((tm // POOL_SUB_ROWS, POOL_SUB_ROWS, width), jnp.bfloat16)],
        compiler_params=_COMPILER_PARAMS,
        name="pool_layer",
    )(x, norm_pre, norm_post, w_in, w_group, scale, w_out)


def _att_layer(x, layer, norm_pre, norm_post, w_in, rel_bias, w_out):
    b, s, d = x.shape
    width = w_out.shape[1]
    n_heads = width // HEAD_DIM
    tq = ATT_BLOCK_ROWS
    n_slots = HIST // tq + 1
    return pl.pallas_call(
        functools.partial(_att_layer_kernel, layer),
        grid=(b, s // tq),
        in_specs=[_row_block(tq, d), _resident(norm_pre.shape), _resident(norm_post.shape),
                  _IN_HBM, _resident(rel_bias.shape), _IN_HBM],
        out_specs=_row_block(tq, d),
        out_shape=jax.ShapeDtypeStruct(x.shape, x.dtype),
        scratch_shapes=[pltpu.VMEM((d, width), jnp.bfloat16),
                        pltpu.VMEM(w_out.shape[1:], jnp.bfloat16),
                        pltpu.VMEM((WEIGHT_STAGE_SLOTS, d, WEIGHT_TILE_COLS), jnp.float32),
                        pltpu.SemaphoreType.DMA((WEIGHT_STAGE_SLOTS,)),
                        pltpu.VMEM((len(ATT_T_BLOCKS), width, d), jnp.bfloat16),
                        pltpu.VMEM((n_slots, tq, width), jnp.bfloat16),
                        pltpu.VMEM((n_slots, width, tq), jnp.bfloat16),
                        pltpu.VMEM((width, tq), jnp.bfloat16),
                        pltpu.VMEM((width, tq), jnp.float32),
                        pltpu.VMEM((tq, width), jnp.bfloat16),
                        pltpu.VMEM((n_heads, ATT_SPAN, LANES), jnp.float32),
                        pltpu.VMEM((ATT_STAGES, 2, 2, LANES, LANES), jnp.float32),
                        pltpu.VMEM((2, 2, ATT_SPAN, LANES), jnp.float32),
                        pltpu.VMEM((ATT_STAGES, 2, HIST + tq, tq), jnp.bfloat16)],
        compiler_params=_COMPILER_PARAMS,
        name="att_layer",
    )(x, norm_pre, norm_post, w_in, rel_bias, w_out)


def kernel(x, norm_pre, norm_post, pool_w_in, pool_w_group, pool_scale, pool_w_out,
           att_w_in, att_rel_bias, att_w_out):
    x = _pool_layer(x, 0, norm_pre, norm_post, pool_w_in, pool_w_group, pool_scale, pool_w_out)
    x = _att_layer(x, 1, norm_pre, norm_post, att_w_in, att_rel_bias, att_w_out)
    return x
```

```python
import functools

import jax
import jax.numpy as jnp
from jax import lax
from jax.experimental import pallas as pl
from jax.experimental.pallas import tpu as pltpu

RMS_EPS = 1e-6
LOG2_E = 1.4426950408889634
SOFTMAX_SUM_LIMIT = 2.0 ** 100
POOL_WINDOWS = (2, 4, 8, 16)
POOL_HALO = 16
CHUNK = 64
LEFT_CHUNKS = 8
HEAD_DIM = 64
MAX_REL = 256
HIST = LEFT_CHUNKS * CHUNK

LANES = 128
SUBLANES = 8
BF16_ROWS = 16
V7X_VMEM_BYTES = 64 * 1024 * 1024
MASK_PENALTY = -1e30
WEIGHT_TILE_COLS = 256
WEIGHT_STAGE_SLOTS = 8
POOL_BLOCK_ROWS = 512
POOL_SUB_ROWS = 256
ATT_BLOCK_ROWS = 256
ATT_T_BLOCKS = (0, 2, 3)
ATT_STAGES = 4
ATT_SPAN = HIST + LANES
REL_LANES = HIST + ATT_BLOCK_ROWS
VMEM_LIMIT_BYTES = V7X_VMEM_BYTES * 7 // 8


def _rms_norm(x, g):
    ms = jnp.mean(x * x, axis=-1, keepdims=True)
    return x * lax.rsqrt(ms + RMS_EPS) * g


class _WeightStream:
    def __init__(self, chunks, stage_ref, sem_ref):
        self.chunks, self.stage_ref, self.sem_ref = chunks, stage_ref, sem_ref
        self.n_slots = stage_ref.shape[0]
        self.depth = self.n_slots - 1
        self.taken = 0
        for i in range(min(self.depth, len(chunks))):
            self._copy(i).start(priority=i % 2)

    def _copy(self, i):
        src = self.chunks[i][0]
        rows, cols = src.shape
        slot = i % self.n_slots
        return pltpu.make_async_copy(src, self.stage_ref.at[slot, pl.ds(0, rows), pl.ds(0, cols)],
                                     self.sem_ref.at[slot])

    def take(self, n):
        for i in range(self.taken, self.taken + n):
            src, consume = self.chunks[i]
            self._copy(i).wait()
            rows, cols = src.shape
            tile = self.stage_ref[i % self.n_slots, 0:rows, 0:cols]
            if i + self.depth < len(self.chunks):
                self._copy(i + self.depth).start(priority=(i + self.depth) % 2)
            consume(tile)
        self.taken += n


def _pool_layer_kernel(layer, x_ref, gpre_ref, gpost_ref, win_hbm, wg_hbm, scale_ref, wout_hbm,
                       o_ref, wa_ref, wz_ref, wg_ref, wout_ref, stage_ref, sem_ref, wmix_ref, carry_ref, act_ref):
    b = pl.program_id(0)
    j = pl.program_id(1)
    tm = x_ref.shape[1]
    n_sub, sub, width = act_ref.shape
    n_groups = len(POOL_WINDOWS)
    gw = width // n_groups

    first = (b == 0) & (j == 0)

    @pl.when(j == 0)
    def _():
        carry_ref[...] = jnp.zeros_like(carry_ref)

    def normed(sb):
        r0 = sb * sub
        hb = _rms_norm(x_ref[0, r0:r0 + sub, :], gpre_ref[layer:layer + 1, :]).astype(jnp.bfloat16)
        t = j * tm + r0 + lax.broadcasted_iota(jnp.int32, (sub, 1), 0)
        return hb, t

    def project(hb, gi):
        a = jnp.dot(hb, wmix_ref[:, gi * gw:(gi + 1) * gw], preferred_element_type=jnp.float32)
        z = jnp.dot(hb, wz_ref[:, gi * gw:(gi + 1) * gw], preferred_element_type=jnp.float32)
        return a, z

    def mix(sb, gi, a, z, t):
        w = POOL_WINDOWS[gi]
        cols = slice(gi * gw, (gi + 1) * gw)
        ext = jnp.concatenate([carry_ref[:, cols], a], axis=0)
        carry_ref[:, cols] = a[sub - POOL_HALO:, :]
        s = ext
        k = 1
        while k < w:
            s = s + pltpu.roll(s, k, axis=0)
            k *= 2
        cnt = jnp.minimum(t + 1, w).astype(jnp.float32)
        pooled = s[POOL_HALO:, :] / cnt
        m = (pooled - a) * scale_ref[:, cols]
        act_ref[sb, :, cols] = (m * (z * jax.nn.sigmoid(z))).astype(jnp.bfloat16)

    def finish(sb):
        r0 = sb * sub
        y = jnp.dot(act_ref[sb], wout_ref[...], preferred_element_type=jnp.float32)
        o_ref[0, r0:r0 + sub, :] = x_ref[0, r0:r0 + sub, :] + _rms_norm(y, gpost_ref[layer:layer + 1, :])

    @pl.when(first)
    def _():
        cw = stage_ref.shape[2]

        def cast_into(dst_ref, rows, c0):
            def consume(tile):
                dst_ref[rows, c0:c0 + cw] = tile.astype(jnp.bfloat16)
            return consume

        def mix_group(gi, last_consume):
            def consume(tile):
                last_consume(tile)
                cols = slice(gi * gw, (gi + 1) * gw)
                wmix_ref[:, cols] = jnp.dot(wa_ref[:, cols], wg_ref[gi],
                                            preferred_element_type=jnp.float32).astype(jnp.bfloat16)
            return consume

        d_in, d_grp = wa_ref.shape[0], wg_ref.shape[1]
        group_chunks = []
        for gi in range(n_groups):
            chunks = []
            for c0 in range(0, gw, cw):
                chunks.append((wg_hbm.at[0, gi, :, pl.ds(c0, cw)], cast_into(wg_ref.at[gi], slice(0, d_grp), c0)))
            for c0 in range(gi * gw, (gi + 1) * gw, cw):
                chunks.append((win_hbm.at[0, :, pl.ds(c0, cw)], cast_into(wa_ref, slice(0, d_in), c0)))
            chunks[-1] = (chunks[-1][0], mix_group(gi, chunks[-1][1]))
            for c0 in range(gi * gw, (gi + 1) * gw, cw):
                chunks.append((win_hbm.at[0, :, pl.ds(width + c0, cw)], cast_into(wz_ref, slice(0, d_in), c0)))
            group_chunks.append(chunks)
        out_chunks = []
        for r0 in range(0, width, d_in):
            for c0 in range(0, wout_ref.shape[1], cw):
                out_chunks.append((wout_hbm.at[0, pl.ds(r0, d_in), pl.ds(c0, cw)],
                                   cast_into(wout_ref, slice(r0, r0 + d_in), c0)))
        chunks = sum(group_chunks, []) + out_chunks
        _WeightStream(chunks, stage_ref, sem_ref).take(len(chunks))

    for sb in range(n_sub):
        hb, t = normed(sb)
        projected = project(hb, 0)
        for gi in range(n_groups):
            a, z = projected
            if gi + 1 < n_groups:
                projected = project(hb, gi + 1)
            mix(sb, gi, a, z, t)
        finish(sb)


def _build_bias_table(rb_ref, bias_ref, heads):
    n_heads, rows, cols = bias_ref.shape
    lanes = REL_LANES
    n_rel = 2 * MAX_REL
    u = lax.broadcasted_iota(jnp.int32, (1, lanes), 1)
    in_table = (u > cols) & (u < n_rel)
    jj = lax.broadcasted_iota(jnp.int32, (rows, cols), 0)
    c = lax.broadcasted_iota(jnp.int32, (rows, cols), 1)
    kc, qc = jj // CHUNK, c // CHUNK
    in_band = (kc >= qc) & (kc <= qc + LEFT_CHUNKS)
    for h in heads:
        far = rb_ref[0, h:h + 1, n_rel:n_rel + 1]
        row = jnp.concatenate([rb_ref[0, h:h + 1, 0:n_rel], jnp.zeros((1, lanes - n_rel), jnp.float32)], axis=1)
        g = jnp.where(in_table, row, far)
        t = pltpu.roll(jnp.broadcast_to(g, (rows, lanes)), 0, 1, stride=1, stride_axis=0)
        bias_ref[h] = jnp.where(in_band, t[:, 0:cols] * LOG2_E, -jnp.inf)


def _att_layer_kernel(layer, x_ref, gpre_ref, gpost_ref, win_hbm, rb_ref, wout_hbm,
                      o_ref, wk_ref, wout_ref, stage_ref, sem_ref, wt_ref, k_ref, vt_ref, qt_ref, gate_ref, gated_ref, bias_ref, own_ref, s_ref, pt_ref):
    b = pl.program_id(0)
    j = pl.program_id(1)
    tq = x_ref.shape[1]
    width = wout_ref.shape[0]
    n_heads = width // HEAD_DIM
    win = HIST + tq
    half = tq // 2
    span = bias_ref.shape[1]
    pair = 2 * HEAD_DIM
    bf16 = jnp.bfloat16

    @pl.when((b == 0) & (j == 0))
    def _():
        cw = stage_ref.shape[2]

        def cast_into(dst_ref, c0):
            def consume(tile):
                dst_ref[:, c0:c0 + cw] = tile.astype(bf16)
            return consume

        def transpose_into(t, c0, heads):
            def consume(tile):
                wt_ref[t, c0:c0 + cw, :] = tile.T.astype(bf16)
                _build_bias_table(rb_ref, bias_ref, heads)
            return consume

        t_tiles = [(t, blk, c0) for t, blk in enumerate(ATT_T_BLOCKS) for c0 in range(0, width, cw)]
        per_tile = -(-n_heads // len(t_tiles))
        chunks = []
        for i, (t, blk, c0) in enumerate(t_tiles):
            heads = range(min(i * per_tile, n_heads), min((i + 1) * per_tile, n_heads))
            chunks.append((win_hbm.at[0, :, pl.ds(blk * width + c0, cw)], transpose_into(t, c0, heads)))
        for c0 in range(0, width, cw):
            chunks.append((win_hbm.at[0, :, pl.ds(width + c0, cw)], cast_into(wk_ref, c0)))
        for c0 in range(0, wout_ref.shape[1], cw):
            chunks.append((wout_hbm.at[0, :, pl.ds(c0, cw)], cast_into(wout_ref, c0)))
        _WeightStream(chunks, stage_ref, sem_ref).take(len(chunks))

        for buf in range(pt_ref.shape[0]):
            for e in range(2):
                pt_ref[buf, e, span:win, 0:half] = jnp.zeros((win - span, half), bf16)
                pt_ref[buf, e, 0:win - span, half:tq] = jnp.zeros((win - span, half), bf16)

    n_slots = k_ref.shape[0]
    slots = [lax.rem(j + 1 + i, n_slots) for i in range(n_slots)]

    @pl.when(j == 0)
    def _():
        k_ref[...] = jnp.zeros(k_ref.shape, bf16)
        vt_ref[...] = jnp.zeros(vt_ref.shape, bf16)

    h = _rms_norm(x_ref[0], gpre_ref[layer:layer + 1, :])
    hb = h.astype(bf16)
    hbt = h.T.astype(bf16)
    k = jnp.dot(hb, wk_ref[...], preferred_element_type=jnp.float32)
    k_ref[slots[-1]] = k.astype(bf16)
    qt = jnp.dot(wt_ref[0], hbt, preferred_element_type=jnp.float32)
    qt_ref[...] = (qt * (HEAD_DIM ** -0.5 * LOG2_E)).astype(bf16)
    vt = jnp.dot(wt_ref[1], hbt, preferred_element_type=jnp.float32)
    vt_ref[slots[-1]] = vt.astype(bf16)
    zt = jnp.dot(wt_ref[2], hbt, preferred_element_type=jnp.float32)
    gate_ref[...] = zt * jax.nn.sigmoid(zt)

    row = lax.broadcasted_iota(jnp.int32, (win, pair), 0)
    lane = lax.broadcasted_iota(jnp.int32, (win, pair), 1)
    k_aux = jnp.where(lane == 0, jnp.where(row < HIST - j * tq, 1.0, 0.0),
                      jnp.where(lane == 1, 1.0, 0.0)).astype(bf16)
    r2 = lax.broadcasted_iota(jnp.int32, (pair, half), 0)
    zeros_q = jnp.zeros((HEAD_DIM, half), bf16)
    ones_v = jnp.ones((BF16_ROWS, win), bf16)
    n_tiles = win // half
    own = HIST // half

    def penalty(shift=None):
        sub = 0.0 if shift is None else jnp.where(r2 == 1, -shift, 0.0)
        return jnp.where(r2 == 0, MASK_PENALTY, sub).astype(bf16)

    def rhs_tile(pr, m, shifts):
        p0 = pr * pair
        cols = slice(m * half, (m + 1) * half)
        q_a = qt_ref[p0:p0 + HEAD_DIM, cols]
        q_b = qt_ref[p0 + HEAD_DIM:p0 + pair, cols]
        pen = [penalty(None if shifts is None else shifts[e][m]) for e in range(2)]
        return jnp.concatenate([jnp.concatenate([q_a, zeros_q, pen[0]], axis=0),
                                jnp.concatenate([zeros_q, q_b, pen[1]], axis=0)], axis=1)

    def lhs_tile(pr, r):
        s = slots[r // (tq // half)]
        r0 = (r % (tq // half)) * half
        return jnp.concatenate([k_ref[s, r0:r0 + half, pr * pair:(pr + 1) * pair],
                                k_aux[r * half:(r + 1) * half, :]], axis=1)

    def score_dot(lhs, rhs):
        if rhs.shape[1] > tq:
            return jnp.dot(lhs, rhs, preferred_element_type=jnp.float32)
        h0 = lhs.shape[0] // 2
        return jnp.concatenate([jnp.dot(lhs[0:h0, :], rhs, preferred_element_type=jnp.float32),
                                jnp.dot(lhs[h0:, :], rhs, preferred_element_type=jnp.float32)], axis=0)

    def visible(r, skip_own=False):
        return [m for m in range(2)
                if 0 <= r - m < span // half and not (skip_own and r == own + m)]

    n_stage = pt_ref.shape[0]

    def attend(pr):
        buf = pr % n_stage
        gated_t, denoms = [], []
        for e in range(2):
            rows = slice((2 * pr + e) * HEAD_DIM, (2 * pr + e + 1) * HEAD_DIM)
            vt_win = jnp.concatenate([vt_ref[s, rows, :] for s in slots], axis=1)
            v_ext = jnp.concatenate([vt_win, ones_v], axis=0)
            o_ext = jnp.dot(v_ext, pt_ref[buf, e], preferred_element_type=jnp.float32)
            denom = o_ext[HEAD_DIM:HEAD_DIM + SUBLANES, :]
            o_t = o_ext[0:HEAD_DIM, :] * jnp.concatenate([1.0 / denom] * (HEAD_DIM // SUBLANES), axis=0)
            gated_t.append(o_t * gate_ref[rows, :])
            denoms.append(denom)
        gated_ref[:, pr * pair:(pr + 1) * pair] = jnp.concatenate(gated_t, axis=0).T.astype(bf16)
        return denoms

    def project_out():
        for r0 in range(0, tq, half):
            y = jnp.dot(gated_ref[r0:r0 + half, :], wout_ref[...], preferred_element_type=jnp.float32)
            o_ref[0, r0:r0 + half, :] = x_ref[0, r0:r0 + half, :] + _rms_norm(y, gpost_ref[layer:layer + 1, :])

    def own_scores(pr):
        buf = pr % n_stage
        maxima = [[None, None], [None, None]]
        for m in range(2):
            st = score_dot(lhs_tile(pr, own + m), rhs_tile(pr, m, None))
            for e in range(2):
                s = st[:, e * half:(e + 1) * half] + bias_ref[2 * pr + e, HIST:HIST + half, :]
                own_ref[buf, e, m] = s
                maxima[e][m] = jnp.max(s, axis=0, keepdims=True)
        return maxima

    def fast_probs(pr, maxima):
        buf = pr % n_stage
        shifts = [[mx.astype(bf16).astype(jnp.float32) for mx in me] for me in maxima]
        rhs = jnp.concatenate([rhs_tile(pr, m, shifts) for m in range(2)], axis=1)
        groups = [[0], [1, 2, 3], [4]]
        for rows in groups:
            ms = visible(rows[0], skip_own=True)
            lhs = jnp.concatenate([lhs_tile(pr, r) for r in rows], axis=0)
            st_all = score_dot(lhs, rhs[:, ms[0] * tq:(ms[-1] + 1) * tq])
            for ri, r in enumerate(rows):
                st = st_all[ri * half:(ri + 1) * half, :]
                for mi, m in enumerate(ms):
                    for e in range(2):
                        kr = slice((r - m) * half, (r - m + 1) * half)
                        c0 = mi * tq + e * half
                        p = jnp.exp2(st[:, c0:c0 + half] + bias_ref[2 * pr + e, kr, :])
                        pt_ref[buf, e, r * half:(r + 1) * half, m * half:(m + 1) * half] = p.astype(bf16)
        for m in range(2):
            for e in range(2):
                p = jnp.exp2(own_ref[buf, e, m] - shifts[e][m])
                pt_ref[buf, e, (own + m) * half:(own + m + 1) * half,
                       m * half:(m + 1) * half] = p.astype(bf16)

    n_pairs = n_heads // 2
    overflow = jnp.zeros((SUBLANES, tq), jnp.float32)
    maxima = {pr: own_scores(pr) for pr in range(min(2, n_pairs))}
    fast_probs(0, maxima.pop(0))
    for pr in range(n_pairs):
        if pr + 2 < n_pairs:
            maxima[pr + 2] = own_scores(pr + 2)
        if pr + 1 < n_pairs:
            fast_probs(pr + 1, maxima.pop(pr + 1))
        for denom in attend(pr):
            overflow = jnp.maximum(overflow, jnp.where(denom < SOFTMAX_SUM_LIMIT, 0.0, 1.0))
    project_out()

    def exact_scores(pr):
        rhs = jnp.concatenate([rhs_tile(pr, m, None) for m in range(2)], axis=1)
        partial = [[None, None], [None, None]]
        for r in range(n_tiles):
            ms = visible(r)
            st = score_dot(lhs_tile(pr, r), rhs[:, ms[0] * tq:(ms[-1] + 1) * tq])
            for mi, m in enumerate(ms):
                for e in range(2):
                    kr = slice((r - m) * half, (r - m + 1) * half)
                    c0 = mi * tq + e * half
                    s = st[:, c0:c0 + half] + bias_ref[2 * pr + e, kr, :]
                    s_ref[e, m, kr, :] = s
                    for i in range(0, half, SUBLANES):
                        tile = s[i:i + SUBLANES, :]
                        prev = partial[e][m]
                        partial[e][m] = tile if prev is None else jnp.maximum(prev, tile)
        return [[jnp.max(p, axis=0, keepdims=True) for p in pe] for pe in partial]

    @pl.when(jnp.max(overflow) > 0.0)
    def _():
        for pr in range(n_pairs):
            exact_maxima = exact_scores(pr)
            for e in range(2):
                for m in range(2):
                    p = jnp.exp2(s_ref[e, m] - exact_maxima[e][m]).astype(bf16)
                    pt_ref[pr % n_stage, e, m * half:m * half + span, m * half:(m + 1) * half] = p
            attend(pr)
        project_out()


def _resident(shape):
    zeros = (0,) * len(shape)
    return pl.BlockSpec(shape, lambda b, j: zeros, pipeline_mode=pl.Buffered(1))


def _row_block(rows, d):
    return pl.BlockSpec((1, rows, d), lambda b, j: (b, j, 0))


_IN_HBM = pl.BlockSpec(memory_space=pl.ANY)


_COMPILER_PARAMS = pltpu.CompilerParams(
    dimension_semantics=("arbitrary", "arbitrary"), vmem_limit_bytes=VMEM_LIMIT_BYTES)


def _pool_layer(x, layer, norm_pre, norm_post, w_in, w_group, scale, w_out):
    b, s, d = x.shape
    width = w_out.shape[1]
    tm = POOL_BLOCK_ROWS
    return pl.pallas_call(
        functools.partial(_pool_layer_kernel, layer),
        grid=(b, s // tm),
        in_specs=[_row_block(tm, d), _resident(norm_pre.shape), _resident(norm_post.shape),
                  _IN_HBM, _IN_HBM, _resident(scale.shape), _IN_HBM],
        out_specs=_row_block(tm, d),
        out_shape=jax.ShapeDtypeStruct(x.shape, x.dtype),
        scratch_shapes=[pltpu.VMEM((d, width), jnp.bfloat16),
                        pltpu.VMEM((d, width), jnp.bfloat16),
                        pltpu.VMEM(w_group.shape[1:], jnp.bfloat16),
                        pltpu.VMEM(w_out.shape[1:], jnp.bfloat16),
                        pltpu.VMEM((WEIGHT_STAGE_SLOTS, d, WEIGHT_TILE_COLS), jnp.float32),
                        pltpu.SemaphoreType.DMA((WEIGHT_STAGE_SLOTS,)),
                        pltpu.VMEM((d, width), jnp.bfloat16),
                        pltpu.VMEM((POOL_HALO, width), jnp.float32),
                        pltpu.VMEM((tm // POOL_SUB_ROWS, POOL_SUB_ROWS, width), jnp.bfloat16)],
        compiler_params=_COMPILER_PARAMS,
        name="pool_layer",
    )(x, norm_pre, norm_post, w_in, w_group, scale, w_out)


def _att_layer(x, layer, norm_pre, norm_post, w_in, rel_bias, w_out):
    b, s, d = x.shape
    width = w_out.shape[1]
    n_heads = width // HEAD_DIM
    tq = ATT_BLOCK_ROWS
    n_slots = HIST // tq + 1
    return pl.pallas_call(
        functools.partial(_att_layer_kernel, layer),
        grid=(b, s // tq),
        in_specs=[_row_block(tq, d), _resident(norm_pre.shape), _resident(norm_post.shape),
                  _IN_HBM, _resident(rel_bias.shape), _IN_HBM],
        out_specs=_row_block(tq, d),
        out_shape=jax.ShapeDtypeStruct(x.shape, x.dtype),
        scratch_shapes=[pltpu.VMEM((d, width), jnp.bfloat16),
                        pltpu.VMEM(w_out.shape[1:], jnp.bfloat16),
                        pltpu.VMEM((WEIGHT_STAGE_SLOTS, d, WEIGHT_TILE_COLS), jnp.float32),
                        pltpu.SemaphoreType.DMA((WEIGHT_STAGE_SLOTS,)),
                        pltpu.VMEM((len(ATT_T_BLOCKS), width, d), jnp.bfloat16),
                        pltpu.VMEM((n_slots, tq, width), jnp.bfloat16),
                        pltpu.VMEM((n_slots, width, tq), jnp.bfloat16),
                        pltpu.VMEM((width, tq), jnp.bfloat16),
                        pltpu.VMEM((width, tq), jnp.float32),
                        pltpu.VMEM((tq, width), jnp.bfloat16),
                        pltpu.VMEM((n_heads, ATT_SPAN, LANES), jnp.float32),
                        pltpu.VMEM((ATT_STAGES, 2, 2, LANES, LANES), jnp.float32),
                        pltpu.VMEM((2, 2, ATT_SPAN, LANES), jnp.float32),
                        pltpu.VMEM((ATT_STAGES, 2, HIST + tq, tq), jnp.bfloat16)],
        compiler_params=_COMPILER_PARAMS,
        name="att_layer",
    )(x, norm_pre, norm_post, w_in, rel_bias, w_out)


def kernel(x, norm_pre, norm_post, pool_w_in, pool_w_group, pool_scale, pool_w_out,
           att_w_in, att_rel_bias, att_w_out):
    x = _pool_layer(x, 0, norm_pre, norm_post, pool_w_in, pool_w_group, pool_scale, pool_w_out)
    x = _att_layer(x, 1, norm_pre, norm_post, att_w_in, att_rel_bias, att_w_out)
    return x
```

```python
import functools

import jax
import jax.numpy as jnp
from jax import lax
from jax.experimental import pallas as pl
from jax.experimental.pallas import tpu as pltpu

RMS_EPS = 1e-6
LOG2_E = 1.4426950408889634
SOFTMAX_SUM_LIMIT = 2.0 ** 100
POOL_WINDOWS = (2, 4, 8, 16)
POOL_HALO = 16
CHUNK = 64
LEFT_CHUNKS = 8
HEAD_DIM = 64
MAX_REL = 256
HIST = LEFT_CHUNKS * CHUNK

LANES = 128
SUBLANES = 8
BF16_ROWS = 16
V7X_VMEM_BYTES = 64 * 1024 * 1024
MASK_PENALTY = -1e30
WEIGHT_TILE_COLS = 256
WEIGHT_STAGE_SLOTS = 8
POOL_BLOCK_ROWS = 512
POOL_SUB_ROWS = 256
ATT_BLOCK_ROWS = 256
ATT_T_BLOCKS = (0, 2, 3)
ATT_STAGES = 4
ATT_SPAN = HIST + LANES
REL_LANES = HIST + ATT_BLOCK_ROWS
VMEM_LIMIT_BYTES = V7X_VMEM_BYTES * 7 // 8


def _rms_norm(x, g):
    ms = jnp.mean(x * x, axis=-1, keepdims=True)
    return x * lax.rsqrt(ms + RMS_EPS) * g


class _WeightStream:
    def __init__(self, chunks, stage_ref, sem_ref):
        self.chunks, self.stage_ref, self.sem_ref = chunks, stage_ref, sem_ref
        self.n_slots = stage_ref.shape[0]
        self.depth = self.n_slots
        self.taken = 0
        for i in range(min(self.depth, len(chunks))):
            self._copy(i).start(priority=i % 2)

    def _copy(self, i):
        src = self.chunks[i][0]
        rows, cols = src.shape
        slot = i % self.n_slots
        return pltpu.make_async_copy(src, self.stage_ref.at[slot, pl.ds(0, rows), pl.ds(0, cols)],
                                     self.sem_ref.at[slot])

    def take(self, n):
        for i in range(self.taken, self.taken + n):
            src, consume = self.chunks[i]
            self._copy(i).wait()
            rows, cols = src.shape
            tile = self.stage_ref[i % self.n_slots, 0:rows, 0:cols]
            if i + self.depth < len(self.chunks):
                self._copy(i + self.depth).start(priority=(i + self.depth) % 2)
            consume(tile)
        self.taken += n


def _pool_layer_kernel(layer, x_ref, gpre_ref, gpost_ref, win_hbm, wg_hbm, scale_ref, wout_hbm,
                       o_ref, wa_ref, wz_ref, wg_ref, wout_ref, stage_ref, sem_ref, wmix_ref, carry_ref, act_ref):
    b = pl.program_id(0)
    j = pl.program_id(1)
    tm = x_ref.shape[1]
    n_sub, sub, width = act_ref.shape
    n_groups = len(POOL_WINDOWS)
    gw = width // n_groups

    first = (b == 0) & (j == 0)

    @pl.when(j == 0)
    def _():
        carry_ref[...] = jnp.zeros_like(carry_ref)

    def normed(sb):
        r0 = sb * sub
        hb = _rms_norm(x_ref[0, r0:r0 + sub, :], gpre_ref[layer:layer + 1, :]).astype(jnp.bfloat16)
        t = j * tm + r0 + lax.broadcasted_iota(jnp.int32, (sub, 1), 0)
        return hb, t

    def project(hb, gi):
        a = jnp.dot(hb, wmix_ref[:, gi * gw:(gi + 1) * gw], preferred_element_type=jnp.float32)
        z = jnp.dot(hb, wz_ref[:, gi * gw:(gi + 1) * gw], preferred_element_type=jnp.float32)
        return a, z

    def mix(sb, gi, a, z, t):
        w = POOL_WINDOWS[gi]
        cols = slice(gi * gw, (gi + 1) * gw)
        ext = jnp.concatenate([carry_ref[:, cols], a], axis=0)
        carry_ref[:, cols] = a[sub - POOL_HALO:, :]
        s = ext
        k = 1
        while k < w:
            s = s + pltpu.roll(s, k, axis=0)
            k *= 2
        cnt = jnp.minimum(t + 1, w).astype(jnp.float32)
        pooled = s[POOL_HALO:, :] / cnt
        m = (pooled - a) * scale_ref[:, cols]
        act_ref[sb, :, cols] = (m * (z * jax.nn.sigmoid(z))).astype(jnp.bfloat16)

    def finish(sb):
        r0 = sb * sub
        y = jnp.dot(act_ref[sb], wout_ref[...], preferred_element_type=jnp.float32)
        o_ref[0, r0:r0 + sub, :] = x_ref[0, r0:r0 + sub, :] + _rms_norm(y, gpost_ref[layer:layer + 1, :])

    @pl.when(first)
    def _():
        cw = stage_ref.shape[2]

        def cast_into(dst_ref, rows, c0):
            def consume(tile):
                dst_ref[rows, c0:c0 + cw] = tile.astype(jnp.bfloat16)
            return consume

        def mix_group(gi, last_consume):
            def consume(tile):
                last_consume(tile)
                cols = slice(gi * gw, (gi + 1) * gw)
                wmix_ref[:, cols] = jnp.dot(wa_ref[:, cols], wg_ref[gi],
                                            preferred_element_type=jnp.float32).astype(jnp.bfloat16)
            return consume

        d_in, d_grp = wa_ref.shape[0], wg_ref.shape[1]
        group_chunks = []
        for gi in range(n_groups):
            chunks = []
            for c0 in range(0, gw, cw):
                chunks.append((wg_hbm.at[0, gi, :, pl.ds(c0, cw)], cast_into(wg_ref.at[gi], slice(0, d_grp), c0)))
            for c0 in range(gi * gw, (gi + 1) * gw, cw):
                chunks.append((win_hbm.at[0, :, pl.ds(c0, cw)], cast_into(wa_ref, slice(0, d_in), c0)))
            chunks[-1] = (chunks[-1][0], mix_group(gi, chunks[-1][1]))
            for c0 in range(gi * gw, (gi + 1) * gw, cw):
                chunks.append((win_hbm.at[0, :, pl.ds(width + c0, cw)], cast_into(wz_ref, slice(0, d_in), c0)))
            group_chunks.append(chunks)
        out_chunks = []
        for r0 in range(0, width, d_in):
            for c0 in range(0, wout_ref.shape[1], cw):
                out_chunks.append((wout_hbm.at[0, pl.ds(r0, d_in), pl.ds(c0, cw)],
                                   cast_into(wout_ref, slice(r0, r0 + d_in), c0)))
        chunks = sum(group_chunks, []) + out_chunks
        _WeightStream(chunks, stage_ref, sem_ref).take(len(chunks))

    for sb in range(n_sub):
        hb, t = normed(sb)
        projected = project(hb, 0)
        for gi in range(n_groups):
            a, z = projected
            if gi + 1 < n_groups:
                projected = project(hb, gi + 1)
            mix(sb, gi, a, z, t)
        finish(sb)


def _build_bias_table(rb_ref, bias_ref, heads):
    n_heads, rows, cols = bias_ref.shape
    lanes = REL_LANES
    n_rel = 2 * MAX_REL
    u = lax.broadcasted_iota(jnp.int32, (1, lanes), 1)
    in_table = (u > cols) & (u < n_rel)
    jj = lax.broadcasted_iota(jnp.int32, (rows, cols), 0)
    c = lax.broadcasted_iota(jnp.int32, (rows, cols), 1)
    kc, qc = jj // CHUNK, c // CHUNK
    in_band = (kc >= qc) & (kc <= qc + LEFT_CHUNKS)
    for h in heads:
        far = rb_ref[0, h:h + 1, n_rel:n_rel + 1]
        row = jnp.concatenate([rb_ref[0, h:h + 1, 0:n_rel], jnp.zeros((1, lanes - n_rel), jnp.float32)], axis=1)
        g = jnp.where(in_table, row, far)
        t = pltpu.roll(jnp.broadcast_to(g, (rows, lanes)), 0, 1, stride=1, stride_axis=0)
        bias_ref[h] = jnp.where(in_band, t[:, 0:cols] * LOG2_E, -jnp.inf)


def _att_layer_kernel(layer, x_ref, gpre_ref, gpost_ref, win_hbm, rb_ref, wout_hbm,
                      o_ref, wk_ref, wout_ref, stage_ref, sem_ref, wt_ref, k_ref, vt_ref, qt_ref, gate_ref, gated_ref, bias_ref, own_ref, s_ref, pt_ref):
    b = pl.program_id(0)
    j = pl.program_id(1)
    tq = x_ref.shape[1]
    width = wout_ref.shape[0]
    n_heads = width // HEAD_DIM
    win = HIST + tq
    half = tq // 2
    span = bias_ref.shape[1]
    pair = 2 * HEAD_DIM
    bf16 = jnp.bfloat16

    @pl.when((b == 0) & (j == 0))
    def _():
        cw = stage_ref.shape[2]

        def cast_into(dst_ref, c0):
            def consume(tile):
                dst_ref[:, c0:c0 + cw] = tile.astype(bf16)
            return consume

        def transpose_into(t, c0, heads):
            def consume(tile):
                wt_ref[t, c0:c0 + cw, :] = tile.T.astype(bf16)
                _build_bias_table(rb_ref, bias_ref, heads)
            return consume

        t_tiles = [(t, blk, c0) for t, blk in enumerate(ATT_T_BLOCKS) for c0 in range(0, width, cw)]
        per_tile = -(-n_heads // len(t_tiles))
        chunks = []
        for i, (t, blk, c0) in enumerate(t_tiles):
            heads = range(min(i * per_tile, n_heads), min((i + 1) * per_tile, n_heads))
            chunks.append((win_hbm.at[0, :, pl.ds(blk * width + c0, cw)], transpose_into(t, c0, heads)))
        for c0 in range(0, width, cw):
            chunks.append((win_hbm.at[0, :, pl.ds(width + c0, cw)], cast_into(wk_ref, c0)))
        for c0 in range(0, wout_ref.shape[1], cw):
            chunks.append((wout_hbm.at[0, :, pl.ds(c0, cw)], cast_into(wout_ref, c0)))
        _WeightStream(chunks, stage_ref, sem_ref).take(len(chunks))

        for buf in range(pt_ref.shape[0]):
            for e in range(2):
                pt_ref[buf, e, span:win, 0:half] = jnp.zeros((win - span, half), bf16)
                pt_ref[buf, e, 0:win - span, half:tq] = jnp.zeros((win - span, half), bf16)

    n_slots = k_ref.shape[0]
    slots = [lax.rem(j + 1 + i, n_slots) for i in range(n_slots)]

    @pl.when(j == 0)
    def _():
        k_ref[...] = jnp.zeros(k_ref.shape, bf16)
        vt_ref[...] = jnp.zeros(vt_ref.shape, bf16)

    h = _rms_norm(x_ref[0], gpre_ref[layer:layer + 1, :])
    hb = h.astype(bf16)
    hbt = h.T.astype(bf16)
    k = jnp.dot(hb, wk_ref[...], preferred_element_type=jnp.float32)
    k_ref[slots[-1]] = k.astype(bf16)
    qt = jnp.dot(wt_ref[0], hbt, preferred_element_type=jnp.float32)
    qt_ref[...] = (qt * (HEAD_DIM ** -0.5 * LOG2_E)).astype(bf16)
    vt = jnp.dot(wt_ref[1], hbt, preferred_element_type=jnp.float32)
    vt_ref[slots[-1]] = vt.astype(bf16)
    zt = jnp.dot(wt_ref[2], hbt, preferred_element_type=jnp.float32)
    gate_ref[...] = zt * jax.nn.sigmoid(zt)

    row = lax.broadcasted_iota(jnp.int32, (win, pair), 0)
    lane = lax.broadcasted_iota(jnp.int32, (win, pair), 1)
    k_aux = jnp.where(lane == 0, jnp.where(row < HIST - j * tq, 1.0, 0.0),
                      jnp.where(lane == 1, 1.0, 0.0)).astype(bf16)
    r2 = lax.broadcasted_iota(jnp.int32, (pair, half), 0)
    zeros_q = jnp.zeros((HEAD_DIM, half), bf16)
    ones_v = jnp.ones((BF16_ROWS, win), bf16)
    n_tiles = win // half
    own = HIST // half

    def penalty(shift=None):
        sub = 0.0 if shift is None else jnp.where(r2 == 1, -shift, 0.0)
        return jnp.where(r2 == 0, MASK_PENALTY, sub).astype(bf16)

    def rhs_tile(pr, m, shifts):
        p0 = pr * pair
        cols = slice(m * half, (m + 1) * half)
        q_a = qt_ref[p0:p0 + HEAD_DIM, cols]
        q_b = qt_ref[p0 + HEAD_DIM:p0 + pair, cols]
        pen = [penalty(None if shifts is None else shifts[e][m]) for e in range(2)]
        return jnp.concatenate([jnp.concatenate([q_a, zeros_q, pen[0]], axis=0),
                                jnp.concatenate([zeros_q, q_b, pen[1]], axis=0)], axis=1)

    def lhs_tile(pr, r):
        s = slots[r // (tq // half)]
        r0 = (r % (tq // half)) * half
        return jnp.concatenate([k_ref[s, r0:r0 + half, pr * pair:(pr + 1) * pair],
                                k_aux[r * half:(r + 1) * half, :]], axis=1)

    def score_dot(lhs, rhs):
        if rhs.shape[1] > tq:
            return jnp.dot(lhs, rhs, preferred_element_type=jnp.float32)
        h0 = lhs.shape[0] // 2
        return jnp.concatenate([jnp.dot(lhs[0:h0, :], rhs, preferred_element_type=jnp.float32),
                                jnp.dot(lhs[h0:, :], rhs, preferred_element_type=jnp.float32)], axis=0)

    def visible(r, skip_own=False):
        return [m for m in range(2)
                if 0 <= r - m < span // half and not (skip_own and r == own + m)]

    n_stage = pt_ref.shape[0]

    def attend(pr):
        buf = pr % n_stage
        gated_t, denoms = [], []
        for e in range(2):
            rows = slice((2 * pr + e) * HEAD_DIM, (2 * pr + e + 1) * HEAD_DIM)
            vt_win = jnp.concatenate([vt_ref[s, rows, :] for s in slots], axis=1)
            v_ext = jnp.concatenate([vt_win, ones_v], axis=0)
            o_ext = jnp.dot(v_ext, pt_ref[buf, e], preferred_element_type=jnp.float32)
            denom = o_ext[HEAD_DIM:HEAD_DIM + SUBLANES, :]
            o_t = o_ext[0:HEAD_DIM, :] * jnp.concatenate([1.0 / denom] * (HEAD_DIM // SUBLANES), axis=0)
            gated_t.append(o_t * gate_ref[rows, :])
            denoms.append(denom)
        gated_ref[:, pr * pair:(pr + 1) * pair] = jnp.concatenate(gated_t, axis=0).T.astype(bf16)
        return denoms

    def project_out():
        for r0 in range(0, tq, half):
            y = jnp.dot(gated_ref[r0:r0 + half, :], wout_ref[...], preferred_element_type=jnp.float32)
            o_ref[0, r0:r0 + half, :] = x_ref[0, r0:r0 + half, :] + _rms_norm(y, gpost_ref[layer:layer + 1, :])

    def own_scores(pr):
        buf = pr % n_stage
        maxima = [[None, None], [None, None]]
        for m in range(2):
            st = score_dot(lhs_tile(pr, own + m), rhs_tile(pr, m, None))
            for e in range(2):
                s = st[:, e * half:(e + 1) * half] + bias_ref[2 * pr + e, HIST:HIST + half, :]
                own_ref[buf, e, m] = s
                maxima[e][m] = jnp.max(s, axis=0, keepdims=True)
        return maxima

    def fast_probs(pr, maxima):
        buf = pr % n_stage
        shifts = [[mx.astype(bf16).astype(jnp.float32) for mx in me] for me in maxima]
        rhs = jnp.concatenate([rhs_tile(pr, m, shifts) for m in range(2)], axis=1)
        groups = [[0], [1, 2, 3], [4]]
        for rows in groups:
            ms = visible(rows[0], skip_own=True)
            lhs = jnp.concatenate([lhs_tile(pr, r) for r in rows], axis=0)
            st_all = score_dot(lhs, rhs[:, ms[0] * tq:(ms[-1] + 1) * tq])
            for ri, r in enumerate(rows):
                st = st_all[ri * half:(ri + 1) * half, :]
                for mi, m in enumerate(ms):
                    for e in range(2):
                        kr = slice((r - m) * half, (r - m + 1) * half)
                        c0 = mi * tq + e * half
                        p = jnp.exp2(st[:, c0:c0 + half] + bias_ref[2 * pr + e, kr, :])
                        pt_ref[buf, e, r * half:(r + 1) * half, m * half:(m + 1) * half] = p.astype(bf16)
        for m in range(2):
            for e in range(2):
                p = jnp.exp2(own_ref[buf, e, m] - shifts[e][m])
                pt_ref[buf, e, (own + m) * half:(own + m + 1) * half,
                       m * half:(m + 1) * half] = p.astype(bf16)

    n_pairs = n_heads // 2
    overflow = jnp.zeros((SUBLANES, tq), jnp.float32)
    maxima = {pr: own_scores(pr) for pr in range(min(2, n_pairs))}
    fast_probs(0, maxima.pop(0))
    for pr in range(n_pairs):
        if pr + 2 < n_pairs:
            maxima[pr + 2] = own_scores(pr + 2)
        if pr + 1 < n_pairs:
            fast_probs(pr + 1, maxima.pop(pr + 1))
        for denom in attend(pr):
            overflow = jnp.maximum(overflow, jnp.where(denom < SOFTMAX_SUM_LIMIT, 0.0, 1.0))
    project_out()

    def exact_scores(pr):
        rhs = jnp.concatenate([rhs_tile(pr, m, None) for m in range(2)], axis=1)
        partial = [[None, None], [None, None]]
        for r in range(n_tiles):
            ms = visible(r)
            st = score_dot(lhs_tile(pr, r), rhs[:, ms[0] * tq:(ms[-1] + 1) * tq])
            for mi, m in enumerate(ms):
                for e in range(2):
                    kr = slice((r - m) * half, (r - m + 1) * half)
                    c0 = mi * tq + e * half
                    s = st[:, c0:c0 + half] + bias_ref[2 * pr + e, kr, :]
                    s_ref[e, m, kr, :] = s
                    for i in range(0, half, SUBLANES):
                        tile = s[i:i + SUBLANES, :]
                        prev = partial[e][m]
                        partial[e][m] = tile if prev is None else jnp.maximum(prev, tile)
        return [[jnp.max(p, axis=0, keepdims=True) for p in pe] for pe in partial]

    @pl.when(jnp.max(overflow) > 0.0)
    def _():
        for pr in range(n_pairs):
            exact_maxima = exact_scores(pr)
            for e in range(2):
                for m in range(2):
                    p = jnp.exp2(s_ref[e, m] - exact_maxima[e][m]).astype(bf16)
                    pt_ref[pr % n_stage, e, m * half:m * half + span, m * half:(m + 1) * half] = p
            attend(pr)
        project_out()


def _resident(shape):
    zeros = (0,) * len(shape)
    return pl.BlockSpec(shape, lambda b, j: zeros, pipeline_mode=pl.Buffered(1))


def _row_block(rows, d):
    return pl.BlockSpec((1, rows, d), lambda b, j: (b, j, 0))


_IN_HBM = pl.BlockSpec(memory_space=pl.ANY)


_COMPILER_PARAMS = pltpu.CompilerParams(
    dimension_semantics=("arbitrary", "arbitrary"), vmem_limit_bytes=VMEM_LIMIT_BYTES)


def _pool_layer(x, layer, norm_pre, norm_post, w_in, w_group, scale, w_out):
    b, s, d = x.shape
    width = w_out.shape[1]
    tm = POOL_BLOCK_ROWS
    return pl.pallas_call(
        functools.partial(_pool_layer_kernel, layer),
        grid=(b, s // tm),
        in_specs=[_row_block(tm, d), _resident(norm_pre.shape), _resident(norm_post.shape),
                  _IN_HBM, _IN_HBM, _resident(scale.shape), _IN_HBM],
        out_specs=_row_block(tm, d),
        out_shape=jax.ShapeDtypeStruct(x.shape, x.dtype),
        scratch_shapes=[pltpu.VMEM((d, width), jnp.bfloat16),
                        pltpu.VMEM((d, width), jnp.bfloat16),
                        pltpu.VMEM(w_group.shape[1:], jnp.bfloat16),
                        pltpu.VMEM(w_out.shape[1:], jnp.bfloat16),
                        pltpu.VMEM((WEIGHT_STAGE_SLOTS, d, WEIGHT_TILE_COLS), jnp.float32),
                        pltpu.SemaphoreType.DMA((WEIGHT_STAGE_SLOTS,)),
                        pltpu.VMEM((d, width), jnp.bfloat16),
                        pltpu.VMEM((POOL_HALO, width), jnp.float32),
                        pltpu.VMEM((tm // POOL_SUB_ROWS, POOL_SUB_ROWS, width), jnp.bfloat16)],
        compiler_params=_COMPILER_PARAMS,
        name="pool_layer",
    )(x, norm_pre, norm_post, w_in, w_group, scale, w_out)


def _att_layer(x, layer, norm_pre, norm_post, w_in, rel_bias, w_out):
    b, s, d = x.shape
    width = w_out.shape[1]
    n_heads = width // HEAD_DIM
    tq = ATT_BLOCK_ROWS
    n_slots = HIST // tq + 1
    return pl.pallas_call(
        functools.partial(_att_layer_kernel, layer),
        grid=(b, s // tq),
        in_specs=[_row_block(tq, d), _resident(norm_pre.shape), _resident(norm_post.shape),
                  _IN_HBM, _resident(rel_bias.shape), _IN_HBM],
        out_specs=_row_block(tq, d),
        out_shape=jax.ShapeDtypeStruct(x.shape, x.dtype),
        scratch_shapes=[pltpu.VMEM((d, width), jnp.bfloat16),
                        pltpu.VMEM(w_out.shape[1:], jnp.bfloat16),
                        pltpu.VMEM((WEIGHT_STAGE_SLOTS, d, WEIGHT_TILE_COLS), jnp.float32),
                        pltpu.SemaphoreType.DMA((WEIGHT_STAGE_SLOTS,)),
                        pltpu.VMEM((len(ATT_T_BLOCKS), width, d), jnp.bfloat16),
                        pltpu.VMEM((n_slots, tq, width), jnp.bfloat16),
                        pltpu.VMEM((n_slots, width, tq), jnp.bfloat16),
                        pltpu.VMEM((width, tq), jnp.bfloat16),
                        pltpu.VMEM((width, tq), jnp.float32),
                        pltpu.VMEM((tq, width), jnp.bfloat16),
                        pltpu.VMEM((n_heads, ATT_SPAN, LANES), jnp.float32),
                        pltpu.VMEM((ATT_STAGES, 2, 2, LANES, LANES), jnp.float32),
                        pltpu.VMEM((2, 2, ATT_SPAN, LANES), jnp.float32),
                        pltpu.VMEM((ATT_STAGES, 2, HIST + tq, tq), jnp.bfloat16)],
        compiler_params=_COMPILER_PARAMS,
        name="att_layer",
    )(x, norm_pre, norm_post, w_in, rel_bias, w_out)


def kernel(x, norm_pre, norm_post, pool_w_in, pool_w_group, pool_scale, pool_w_out,
           att_w_in, att_rel_bias, att_w_out):
    x = _pool_layer(x, 0, norm_pre, norm_post, pool_w_in, pool_w_group, pool_scale, pool_w_out)
    x = _att_layer(x, 1, norm_pre, norm_post, att_w_in, att_rel_bias, att_w_out)
    return x
```
